```python
import math
import jax
import jax.numpy as jnp
from jax import lax
import numpy as np

D_MODEL = 1024
BATCH = 16
SEQ = 2048
DEPTH = 4

GRID_W = 64
CTX_LEN = 256
N_MOD = 6
NORM_EPS = 1e-6
GN_EPS = 64e-5

BRANCH_WIDTH = 512
N_BRANCH = 3

SSM_WIDTH = BRANCH_WIDTH
SSM_GROUP = 16
SSM_GROUPS = SSM_WIDTH // SSM_GROUP
SSM_STATE = 64
SSM_DT_MIN = 1e-3
SSM_DT_MAX = 1e-1

RWKV_WIDTH = BRANCH_WIDTH
RWKV_HEAD = 64
RWKV_HEADS = RWKV_WIDTH // RWKV_HEAD
DECAY_LORA = 64
ICLR_LORA = 64
GATE_LORA = 128
RWKV_IN = 3 * RWKV_WIDTH + 2 * DECAY_LORA + 2 * ICLR_LORA + GATE_LORA
RWKV_SPLITS = (RWKV_WIDTH, 2 * RWKV_WIDTH, 3 * RWKV_WIDTH,
               3 * RWKV_WIDTH + 2 * DECAY_LORA,
               3 * RWKV_WIDTH + 2 * DECAY_LORA + 2 * ICLR_LORA)

MLA_HEADS = 8
MLA_NOPE = 64
MLA_ROPE = 32
MLA_V = BRANCH_WIDTH // MLA_HEADS
Q_LORA = 384
KV_LORA = 256
MLA_IN = Q_LORA + KV_LORA + MLA_ROPE
MLA_SCALE = 1.0 / math.sqrt(MLA_NOPE + MLA_ROPE)
ROPE_AXIS_DIMS = MLA_ROPE // 2
ROPE_BASE = 10000.0
Q_BLOCK = 128

RWKV_OFF = SSM_WIDTH
MLA_OFF = RWKV_OFF + RWKV_IN
GATE_OFF = MLA_OFF + MLA_IN
N_IN = GATE_OFF + N_BRANCH * D_MODEL

N_EXPERTS = 16
EXPERT_FF = 1536
EC_CAPACITY = 2

kernel_name = 'hybrid_s5_rwkv7_mla_ecmoe_diffusion_trunk'


def rmsnorm(x, g):
    xf = x.astype(jnp.float32)
    y = xf * lax.rsqrt(jnp.mean(xf * xf, axis=-1, keepdims=True) + NORM_EPS)
    return (y * g.astype(jnp.float32)).astype(x.dtype)


def modulate(h, shift, scale):
    return h * (1.0 + scale) + shift


def centred_shift(p):
    prev = jnp.pad(p[:, :-1], ((0, 0), (1, 0), (0, 0)))
    nxt = jnp.pad(p[:, 1:], ((0, 0), (0, 1), (0, 0)))
    return 0.5 * (prev + nxt)


def axial_rope(rows):
    row = jnp.repeat(jnp.arange(rows), GRID_W).astype(jnp.float32)
    col = jnp.tile(jnp.arange(GRID_W), rows).astype(jnp.float32)
    inv = ROPE_BASE ** (-jnp.arange(0, ROPE_AXIS_DIMS, 2, dtype=jnp.float32) / ROPE_AXIS_DIMS)
    ang = jnp.concatenate([row[:, None] * inv, col[:, None] * inv], axis=-1)
    return jnp.cos(ang), jnp.sin(ang)


def apply_rope(x, cos, sin):
    half = MLA_ROPE // 2
    x1, x2 = x[..., :half], x[..., half:]
    return jnp.concatenate([x1 * cos - x2 * sin, x1 * sin + x2 * cos], axis=-1).astype(x.dtype)


def ssm_discretise(lam_re, lam_im, log_dt, b_re, b_im):
    f32 = jnp.float32
    lr, li = lam_re.astype(f32), lam_im.astype(f32)
    dt = jnp.exp(log_dt.astype(f32))[:, None]
    mag = jnp.exp(lr * dt)
    abar_re, abar_im = mag * jnp.cos(li * dt), mag * jnp.sin(li * dt)
    den = lr * lr + li * li
    nr, ni = abar_re - 1.0, abar_im
    coef_re = (nr * lr + ni * li) / den
    coef_im = (ni * lr - nr * li) / den
    br, bi = b_re.astype(f32), b_im.astype(f32)
    bbar_re = coef_re[..., None] * br - coef_im[..., None] * bi
    bbar_im = coef_re[..., None] * bi + coef_im[..., None] * br
    return abar_re, abar_im, bbar_re, bbar_im


def _ssm_combine(e1, e2):
    a1r, a1i, b1r, b1i = e1
    a2r, a2i, b2r, b2i = e2
    return (a2r * a1r - a2i * a1i, a2r * a1i + a2i * a1r,
            a2r * b1r - a2i * b1i + b2r, a2r * b1i + a2i * b1r + b2i)


def ssm_scan(u, abar_re, abar_im, bbar_re, bbar_im, reverse, s0_re=None, s0_im=None):
    T = u.shape[1]
    bu_re = jnp.einsum('btgc,gpc->btgp', u, bbar_re)
    bu_im = jnp.einsum('btgc,gpc->btgp', u, bbar_im)
    a_re = jnp.broadcast_to(abar_re, (1, T) + abar_re.shape)
    a_im = jnp.broadcast_to(abar_im, (1, T) + abar_im.shape)
    acc_re, acc_im, x_re, x_im = lax.associative_scan(
        _ssm_combine, (a_re, a_im, bu_re, bu_im), reverse=reverse, axis=1)
    if s0_re is not None:
        s_re, s_im = s0_re[:, None], s0_im[:, None]
        x_re = x_re + acc_re * s_re - acc_im * s_im
        x_im = x_im + acc_re * s_im + acc_im * s_re
    return x_re, x_im


def ssm_readout(x_re, x_im, c_re, c_im):
    return (jnp.einsum('btgp,gcp->btgc', x_re, c_re.astype(jnp.float32))
            - jnp.einsum('btgp,gcp->btgc', x_im, c_im.astype(jnp.float32)))


def ssm_glu(y, glu_w, glu_b):
    B, T = y.shape[:2]
    y = jax.nn.gelu(y.reshape(B, T, SSM_WIDTH))
    return y * jax.nn.sigmoid(y @ glu_w + glu_b)


def ssm_branch(u_ctx, u_lat, lam_re, lam_im, log_dt, b_re, b_im, c_re, c_im, d_skip,
               glu_w, glu_b, with_ctx):
    def groups(u):
        return u.astype(jnp.float32).reshape(u.shape[0], u.shape[1], SSM_GROUPS, SSM_GROUP)
    uc, ul = groups(u_ctx), groups(u_lat)
    d = d_skip.astype(jnp.float32).reshape(SSM_GROUPS, SSM_GROUP)
    y_lat = ul * d
    y_ctx = uc * d if with_ctx else None
    for j, reverse in enumerate((False, True)):
        disc = ssm_discretise(lam_re[j], lam_im[j], log_dt[j], b_re, b_im)
        xc_re, xc_im = ssm_scan(uc, *disc, reverse=reverse)
        end = 0 if reverse else -1
        xl_re, xl_im = ssm_scan(ul, *disc, reverse=reverse,
                                s0_re=xc_re[:, end], s0_im=xc_im[:, end])
        y_lat = y_lat + ssm_readout(xl_re, xl_im, c_re[j], c_im[j])
        if with_ctx:
            y_ctx = y_ctx + ssm_readout(xc_re, xc_im, c_re[j], c_im[j])
    out_lat = ssm_glu(y_lat, glu_w, glu_b)
    out_ctx = ssm_glu(y_ctx, glu_w, glu_b) if with_ctx else None
    return out_ctx, out_lat


def rwkv_prepare(p, mu, w0, w2, a0, a2, g2, k_k, k_a):
    B, T, _ = p.shape
    p = p.astype(jnp.float32)
    p = p + mu * (centred_shift(p) - p)
    r, k, v, pw, pa, pg = jnp.split(p, RWKV_SPLITS, axis=-1)
    pw = pw.reshape(B, T, 2, DECAY_LORA)
    pa = pa.reshape(B, T, 2, ICLR_LORA)
    w_log = -jax.nn.softplus(-(w0 + jnp.einsum('btjl,jlc->btjc', jnp.tanh(pw), w2))) - 0.5
    decay = jnp.exp(-jnp.exp(w_log))
    iclr = jax.nn.sigmoid(a0 + jnp.einsum('btjl,jlc->btjc', pa, a2))
    k_dir = k[:, :, None] * (1.0 + (iclr - 1.0) * k_a)
    g = jax.nn.sigmoid(pg) @ g2

    def heads(t):
        return t.reshape(t.shape[:-1] + (RWKV_HEADS, RWKV_HEAD))
    kk = heads(k * k_k)
    kk = kk * lax.rsqrt(jnp.sum(kk * kk, axis=-1, keepdims=True) + 1e-12)
    return heads(r), heads(v), kk, g, heads(decay), heads(iclr), heads(k_dir)


def rwkv_scan(s0, r, w, k, v, kk, a, reverse):
    xs = tuple(jnp.moveaxis(t, 1, 0) for t in (r, w, k, v, kk, a))

    def step(S, inp):
        r_t, w_t, k_t, v_t, kk_t, a_t = inp
        sa = jnp.einsum('bhvk,bhk->bhv', S, -kk_t)
        S = (S * w_t[:, :, None, :] + sa[..., None] * (kk_t * a_t)[:, :, None, :]
             + v_t[..., None] * k_t[:, :, None, :])
        return S, jnp.einsum('bhvk,bhk->bhv', S, r_t)

    s_end, ys = lax.scan(step, s0, xs, reverse=reverse)
    return s_end, jnp.moveaxis(ys, 0, 1)


def rwkv_readout(y, prep, r_k, ln_w, ln_b):
    r, v, _, g, _, _, k_dir = prep
    B, T = y.shape[:2]
    mean = jnp.mean(y, axis=-1, keepdims=True)
    var = jnp.mean(jnp.square(y - mean), axis=-1, keepdims=True)
    yn = ((y - mean) * lax.rsqrt(var + GN_EPS)).reshape(B, T, RWKV_WIDTH) * ln_w + ln_b
    bonus = jnp.sum(r * jnp.sum(k_dir, axis=2) * r_k, axis=-1, keepdims=True) * v
    return (yn + bonus.reshape(B, T, RWKV_WIDTH)) * g


def rwkv_branch(p_ctx, p_lat, mu, w0, w2, a0, a2, g2, k_k, k_a, r_k, ln_w, ln_b, with_ctx):
    pc = rwkv_prepare(p_ctx, mu, w0, w2, a0, a2, g2, k_k, k_a)
    pl = rwkv_prepare(p_lat, mu, w0, w2, a0, a2, g2, k_k, k_a)
    rc, vc, kkc, _, dc, ac, kc = pc
    rl, vl, kkl, _, dl, al, kl = pl
    s_zero = jnp.zeros((p_lat.shape[0], RWKV_HEADS, RWKV_HEAD, RWKV_HEAD), jnp.float32)
    y_lat = jnp.zeros_like(rl)
    y_ctx = jnp.zeros_like(rc) if with_ctx else None
    for j, reverse in enumerate((False, True)):
        s_ctx, yc = rwkv_scan(s_zero, rc, dc[:, :, j], kc[:, :, j], vc, kkc, ac[:, :, j], reverse)
        _, yl = rwkv_scan(s_ctx, rl, dl[:, :, j], kl[:, :, j], vl, kkl, al[:, :, j], reverse)
        y_lat = y_lat + yl
        if with_ctx:
            y_ctx = y_ctx + yc
    out_lat = rwkv_readout(y_lat, pl, r_k, ln_w, ln_b)
    out_ctx = rwkv_readout(y_ctx, pc, r_k, ln_w, ln_b) if with_ctx else None
    return out_ctx, out_lat


def mla_keys(p, kv_norm, w_ukv, kn_nope, kn_rope, cos, sin):
    B, T, _ = p.shape
    ckv = rmsnorm(p[..., Q_LORA:Q_LORA + KV_LORA], kv_norm)
    k_rope = rmsnorm(p[..., Q_LORA + KV_LORA:], kn_rope)
    kv = (ckv @ w_ukv).reshape(B, T, MLA_HEADS, MLA_NOPE + MLA_V)
    k_nope = rmsnorm(kv[..., :MLA_NOPE], kn_nope)
    v = kv[..., MLA_NOPE:]
    if cos is not None:
        k_rope = apply_rope(k_rope, cos, sin)
    return k_nope, k_rope, v


def mla_queries(p, q_norm, w_uq, qn_nope, qn_rope, cos, sin):
    B, T, _ = p.shape
    cq = rmsnorm(p[..., :Q_LORA], q_norm)
    q = (cq @ w_uq).reshape(B, T, MLA_HEADS, MLA_NOPE + MLA_ROPE)
    q_nope = rmsnorm(q[..., :MLA_NOPE], qn_nope)
    q_rope = rmsnorm(q[..., MLA_NOPE:], qn_rope)
    if cos is not None:
        q_rope = apply_rope(q_rope, cos[:, None], sin[:, None])
    return q_nope, q_rope


def mla_attend(q_nope, q_rope, k_nope, k_rope, v):
    s = (jnp.einsum('bqhd,bkhd->bhqk', q_nope, k_nope)
         + jnp.einsum('bqhr,bkr->bhqk', q_rope, k_rope))
    pr = jax.nn.softmax(s.astype(jnp.float32) * MLA_SCALE, axis=-1).astype(v.dtype)
    return jnp.einsum('bhqk,bkhd->bqhd', pr, v)


def mla_branch(p_ctx, p_lat, cos, sin, q_norm, kv_norm, w_uq, w_ukv, qn_nope, kn_nope,
               qn_rope, kn_rope, with_ctx):
    B, T, _ = p_lat.shape
    kn_c, kr_c, v_c = mla_keys(p_ctx, kv_norm, w_ukv, kn_nope, kn_rope, None, None)
    kn_l, kr_l, v_l = mla_keys(p_lat, kv_norm, w_ukv, kn_nope, kn_rope, cos, sin)
    qn_l, qr_l = mla_queries(p_lat, q_norm, w_uq, qn_nope, qn_rope, cos, sin)
    kn = jnp.concatenate([kn_c, kn_l], axis=1)
    kr = jnp.concatenate([kr_c, kr_l], axis=1)
    v = jnp.concatenate([v_c, v_l], axis=1)
    nb = T // Q_BLOCK

    def to_blocks(t):
        return jnp.swapaxes(t.reshape((B, nb, Q_BLOCK) + t.shape[2:]), 0, 1)

    out = lax.map(lambda qs: mla_attend(qs[0], qs[1], kn, kr, v), (to_blocks(qn_l), to_blocks(qr_l)))
    out_lat = jnp.swapaxes(out, 0, 1).reshape(B, T, MLA_HEADS * MLA_V)
    out_ctx = None
    if with_ctx:
        qn_c, qr_c = mla_queries(p_ctx, q_norm, w_uq, qn_nope, qn_rope, None, None)
        out_ctx = mla_attend(qn_c, qr_c, kn_c, kr_c, v_c).reshape(B, p_ctx.shape[1], MLA_HEADS * MLA_V)
    return out_ctx, out_lat


def merge_branches(gate_logits, ys, w_branch, w_out):
    B, T, _ = gate_logits.shape
    gates = jax.nn.sigmoid(gate_logits.reshape(B, T, N_BRANCH, D_MODEL))
    y = jnp.stack([t.astype(gate_logits.dtype) for t in ys], axis=2)
    br = jnp.einsum('btjn,jnd->btjd', y, w_branch)
    return jnp.sum(gates * br, axis=2) @ w_out


def token_mixer(h_ctx, h_lat, cos, sin, with_ctx, w_in, ssm_p, rwkv_p, mla_p, w_branch, w_out):
    p_ctx = h_ctx @ w_in
    p_lat = h_lat @ w_in
    s_c, s_l = ssm_branch(p_ctx[..., :RWKV_OFF], p_lat[..., :RWKV_OFF], *ssm_p, with_ctx=with_ctx)
    r_c, r_l = rwkv_branch(p_ctx[..., RWKV_OFF:MLA_OFF], p_lat[..., RWKV_OFF:MLA_OFF], *rwkv_p,
                           with_ctx=with_ctx)
    m_c, m_l = mla_branch(p_ctx[..., MLA_OFF:GATE_OFF], p_lat[..., MLA_OFF:GATE_OFF], cos, sin,
                          *mla_p, with_ctx=with_ctx)
    out_lat = merge_branches(p_lat[..., GATE_OFF:], (s_l, r_l, m_l), w_branch, w_out)
    out_ctx = merge_branches(p_ctx[..., GATE_OFF:], (s_c, r_c, m_c), w_branch, w_out) if with_ctx else None
    return out_ctx, out_lat


def ec_moe(h, router_w, w1, w3, w2):
    B, T, _ = h.shape
    cap = EC_CAPACITY * T // N_EXPERTS
    aff = jax.nn.softmax((h @ router_w).astype(jnp.float32), axis=-1)
    gate, idx = lax.top_k(jnp.swapaxes(aff, 1, 2), cap)
    bidx = jnp.arange(B)[:, None, None]
    xs = h[bidx, idx]
    hid = jax.nn.silu(jnp.einsum('becd,edf->becf', xs, w1)) * jnp.einsum('becd,edf->becf', xs, w3)
    y = jnp.einsum('becf,efd->becd', hid, w2) * gate[..., None].astype(h.dtype)
    return jnp.zeros_like(h).at[bidx, idx].add(y)


def setup_inputs(seed: int = 0) -> dict:
    key = jax.random.key(seed)
    ks = iter(jax.random.split(key, 64))
    f32 = jnp.float32

    def nrm(shape, scale):
        return scale * jax.random.normal(next(ks), shape, f32)

    def gain(shape):
        return 1.0 + nrm(shape, 0.02)

    L, D = DEPTH, D_MODEL
    G, P, GC = SSM_GROUPS, SSM_STATE, SSM_GROUP
    W, H, N = RWKV_WIDTH, RWKV_HEADS, RWKV_HEAD
    E, F = N_EXPERTS, EXPERT_FF
    return {
        'x': nrm((BATCH, SEQ, D), 1.0),
        'c': nrm((BATCH, D), 1.0),
        'ctx': nrm((BATCH, CTX_LEN, D), 1.0),
        'c_ctx': nrm((D,), 1.0),
        'ada_w': nrm((L, D, N_MOD * D), 0.5 * D ** -0.5),
        'ada_b': nrm((L, N_MOD * D), 0.02),
        'norm1_g': gain((L, D)),
        'norm2_g': gain((L, D)),
        'w_in': nrm((L, D, N_IN), D ** -0.5),
        'ssm_lambda_re': -0.5 + nrm((L, 2, G, P), 0.01),
        'ssm_lambda_im': math.pi * jnp.arange(P, dtype=f32) + nrm((L, 2, G, P), 0.01),
        'ssm_log_dt': jax.random.uniform(next(ks), (L, 2, G), f32,
                                         math.log(SSM_DT_MIN), math.log(SSM_DT_MAX)),
        'ssm_b_re': nrm((L, G, P, GC), (2.0 * GC) ** -0.5),
        'ssm_b_im': nrm((L, G, P, GC), (2.0 * GC) ** -0.5),
        'ssm_c_re': nrm((L, 2, G, GC, P), (2.0 * P) ** -0.5),
        'ssm_c_im': nrm((L, 2, G, GC, P), (2.0 * P) ** -0.5),
        'ssm_d': nrm((L, SSM_WIDTH), 1.0),
        'ssm_glu_w': nrm((L, SSM_WIDTH, SSM_WIDTH), SSM_WIDTH ** -0.5),
        'ssm_glu_b': nrm((L, SSM_WIDTH), 0.02),
        'rwkv_mu': jax.random.uniform(next(ks), (L, RWKV_IN), f32, 0.0, 1.0),
        'rwkv_w0': jax.random.uniform(next(ks), (L, 2, W), f32, -6.0, 0.0),
        'rwkv_w2': nrm((L, 2, DECAY_LORA, W), 0.5 * DECAY_LORA ** -0.5),
        'rwkv_a0': nrm((L, 2, W), 0.1),
        'rwkv_a2': nrm((L, 2, ICLR_LORA, W), 0.5 * ICLR_LORA ** -0.5),
        'rwkv_g2': nrm((L, GATE_LORA, W), GATE_LORA ** -0.5),
        'rwkv_k_k': 0.85 + nrm((L, W), 0.02),
        'rwkv_k_a': 1.0 + nrm((L, W), 0.02),
        'rwkv_r_k': nrm((L, H, N), 0.1),
        'rwkv_ln_w': gain((L, W)),
        'rwkv_ln_b': nrm((L, W), 0.02),
        'mla_q_norm': gain((L, Q_LORA)),
        'mla_kv_norm': gain((L, KV_LORA)),
        'mla_w_uq': nrm((L, Q_LORA, MLA_HEADS * (MLA_NOPE + MLA_ROPE)), Q_LORA ** -0.5),
        'mla_w_ukv': nrm((L, KV_LORA, MLA_HEADS * (MLA_NOPE + MLA_V)), KV_LORA ** -0.5),
        'mla_qn_nope': gain((L, MLA_NOPE)),
        'mla_kn_nope': gain((L, MLA_NOPE)),
        'mla_qn_rope': gain((L, MLA_ROPE)),
        'mla_kn_rope': gain((L, MLA_ROPE)),
        'w_branch': nrm((L, N_BRANCH, BRANCH_WIDTH, D), BRANCH_WIDTH ** -0.5),
        'w_out': nrm((L, D, D), D ** -0.5),
        'router_w': nrm((L, D, E), D ** -0.5),
        'moe_w1': nrm((L, E, D, F), D ** -0.5),
        'moe_w3': nrm((L, E, D, F), D ** -0.5),
        'moe_w2': nrm((L, E, F, D), F ** -0.5),
    }


def reference(x, c, ctx, c_ctx, ada_w, ada_b, norm1_g, norm2_g, w_in,
              ssm_lambda_re, ssm_lambda_im, ssm_log_dt, ssm_b_re, ssm_b_im, ssm_c_re, ssm_c_im,
              ssm_d, ssm_glu_w, ssm_glu_b,
              rwkv_mu, rwkv_w0, rwkv_w2, rwkv_a0, rwkv_a2, rwkv_g2, rwkv_k_k, rwkv_k_a, rwkv_r_k,
              rwkv_ln_w, rwkv_ln_b,
              mla_q_norm, mla_kv_norm, mla_w_uq, mla_w_ukv, mla_qn_nope, mla_kn_nope,
              mla_qn_rope, mla_kn_rope,
              w_branch, w_out, router_w, moe_w1, moe_w3, moe_w2):
    B, T, _ = x.shape
    rows = T // GRID_W
    cos, sin = axial_rope(rows)
    silu_c = jax.nn.silu(c)
    silu_cc = jax.nn.silu(c_ctx)
    for l in range(DEPTH):
        with_ctx = l < DEPTH - 1
        mod_lat = (silu_c @ ada_w[l] + ada_b[l]).reshape(B, N_MOD, 1, D_MODEL)
        mod_ctx = (silu_cc @ ada_w[l] + ada_b[l]).reshape(N_MOD, D_MODEL)
        h_lat = modulate(rmsnorm(x, norm1_g[l]), mod_lat[:, 0], mod_lat[:, 1])
        h_ctx = modulate(rmsnorm(ctx, norm1_g[l]), mod_ctx[0], mod_ctx[1])
        ssm_p = (ssm_lambda_re[l], ssm_lambda_im[l], ssm_log_dt[l], ssm_b_re[l], ssm_b_im[l],
                 ssm_c_re[l], ssm_c_im[l], ssm_d[l], ssm_glu_w[l], ssm_glu_b[l])
        rwkv_p = (rwkv_mu[l], rwkv_w0[l], rwkv_w2[l], rwkv_a0[l], rwkv_a2[l], rwkv_g2[l],
                  rwkv_k_k[l], rwkv_k_a[l], rwkv_r_k[l], rwkv_ln_w[l], rwkv_ln_b[l])
        mla_p = (mla_q_norm[l], mla_kv_norm[l], mla_w_uq[l], mla_w_ukv[l], mla_qn_nope[l],
                 mla_kn_nope[l], mla_qn_rope[l], mla_kn_rope[l])
        mix_ctx, mix_lat = token_mixer(h_ctx, h_lat, cos, sin, with_ctx, w_in[l], ssm_p, rwkv_p,
                                       mla_p, w_branch[l], w_out[l])
        x = x + mod_lat[:, 2] * mix_lat
        h2 = modulate(rmsnorm(x, norm2_g[l]), mod_lat[:, 3], mod_lat[:, 4])
        x = x + mod_lat[:, 5] * ec_moe(h2, router_w[l], moe_w1[l], moe_w3[l], moe_w2[l])
        if with_ctx:
            ctx = ctx + mod_ctx[2] * mix_ctx
            h2c = modulate(rmsnorm(ctx, norm2_g[l]), mod_ctx[3], mod_ctx[4])
            ctx = ctx + mod_ctx[5] * ec_moe(h2c, router_w[l], moe_w1[l], moe_w3[l], moe_w2[l])
    return x
```

```python
import functools
import math

import jax
import jax.numpy as jnp
from jax import lax
from jax.experimental import pallas as pl
from jax.experimental.pallas import tpu as pltpu

F32 = jnp.float32
BF16 = jnp.bfloat16

D_MODEL = 1024
GRID_W = 64
N_MOD = 6
NORM_EPS = 1e-6
GN_EPS = 64e-5
BRANCH_WIDTH = 512
SSM_GROUP = 16
SSM_GROUPS = BRANCH_WIDTH // SSM_GROUP
SSM_STATE = 64
SSM_CHUNK = 16
RWKV_W = BRANCH_WIDTH
RWKV_HEAD = 64
RWKV_HEADS = RWKV_W // RWKV_HEAD
RWKV_CHUNK = 64
LORA_W = 128
RWKV_IN = 3 * RWKV_W + 3 * LORA_W
MLA_HEADS = 8
MLA_NOPE = 64
MLA_ROPE = 32
MLA_V = 64
Q_LORA = 384
KV_LORA = 256
MLA_IN = Q_LORA + KV_LORA + MLA_ROPE
MLA_IN_PAD = 768
MLA_SCALE = 1.0 / math.sqrt(MLA_NOPE + MLA_ROPE)
ROPE_BASE = 10000.0
HEAD_PAD = 128
RWKV_OFF = BRANCH_WIDTH
MLA_OFF = RWKV_OFF + RWKV_IN
GATE_OFF = MLA_OFF + MLA_IN
N_EXPERTS = 16
EXPERT_FF = 1536
EC_CAPACITY = 2
VMEM_LIMIT = 56 * 1024 * 1024


def _params(*sem):
    return pltpu.CompilerParams(dimension_semantics=sem, vmem_limit_bytes=VMEM_LIMIT)


def _dot(a, b):
    return jnp.dot(a, b, preferred_element_type=F32)


def _dot_nt(a, b):
    return lax.dot_general(a, b, (((1,), (1,)), ((), ())), preferred_element_type=F32)


def _dot_tn(a, b):
    return lax.dot_general(a, b, (((0,), (0,)), ((), ())), preferred_element_type=F32)


def _split2(x):
    hi = x.astype(BF16)
    lo = (x - hi.astype(F32)).astype(BF16)
    return hi, lo


def _split3(x):
    hi = x.astype(BF16)
    r1 = x - hi.astype(F32)
    mid = r1.astype(BF16)
    lo = (r1 - mid.astype(F32)).astype(BF16)
    return hi, mid, lo


def _dot_exact_rhs(x, m, parts):
    pieces = _split2(x) if parts == 2 else _split3(x)
    acc = _dot(pieces[0], m)
    for p in pieces[1:]:
        acc = acc + _dot(p, m)
    return acc


def _dot_exact_lhs(m, x, parts):
    pieces = _split2(x) if parts == 2 else _split3(x)
    acc = _dot(m, pieces[0])
    for p in pieces[1:]:
        acc = acc + _dot(m, p)
    return acc


def _sigmoid(x):
    return 1.0 / (1.0 + jnp.exp(-x))


def _norm_mod(x, g, shift, scale):
    y = x * lax.rsqrt(jnp.mean(x * x, axis=-1, keepdims=True) + NORM_EPS)
    return (y * g) * (1.0 + scale) + shift


def _token_tile(ctx_len, seq):
    for tm in (256, 128, 64):
        if ctx_len % tm == 0 and seq % tm == 0:
            return tm
    raise ValueError("context and latent lengths must be multiples of 64")


def _mods_kernel(cc_ref, w_ref, b_ref, o_ref):
    c = cc_ref[...]
    s = (c * _sigmoid(c)).astype(BF16)
    o_ref[0] = _dot(s, w_ref[0].astype(BF16)) + b_ref[0]


def _mods(cc, ada_w, ada_b):
    L, D, N = ada_w.shape
    R = cc.shape[0]
    tn = 1536
    return pl.pallas_call(
        _mods_kernel,
        grid=(L, N // tn),
        in_specs=[pl.BlockSpec((R, D), lambda l, j: (0, 0)),
                  pl.BlockSpec((1, D, tn), lambda l, j: (l, 0, j)),
                  pl.BlockSpec((1, 1, tn), lambda l, j: (l, 0, j))],
        out_specs=pl.BlockSpec((1, R, tn), lambda l, j: (l, 0, j)),
        out_shape=jax.ShapeDtypeStruct((L, R, N), F32),
        compiler_params=_params("parallel", "parallel"),
        name="adaln_mods",
    )(cc, ada_w, ada_b.reshape(L, 1, N))


def _kin_kernel(z_ref, g_ref, mod_ref, ws_ref, wr_ref, wm_ref, os_ref, or_ref, om_ref):
    m = mod_ref[0, 0]
    h = _norm_mod(z_ref[0], g_ref[...], m[0:1], m[1:2]).astype(BF16)
    os_ref[0] = _dot(h, ws_ref[...])
    or_ref[0] = _dot(h, wr_ref[...])
    om_ref[0] = _dot(h, wm_ref[...])


def _input_proj(z, g, modsel, ws, wr, wm, ctx_len):
    B, S, D = z.shape
    tm = _token_tile(ctx_len, S - ctx_len)
    nct = ctx_len // tm
    tok = lambda n: pl.BlockSpec((1, tm, n), lambda b, i: (b, i, 0))
    full = lambda a: pl.BlockSpec(a.shape, lambda b, i: (0,) * a.ndim)
    return pl.pallas_call(
        _kin_kernel,
        grid=(B, S // tm),
        in_specs=[tok(D), full(g),
                  pl.BlockSpec((1, 1, N_MOD, D), lambda b, i: (b, jnp.where(i >= nct, 1, 0), 0, 0)),
                  full(ws), full(wr), full(wm)],
        out_specs=[tok(ws.shape[1]), tok(wr.shape[1]), tok(wm.shape[1])],
        out_shape=[jax.ShapeDtypeStruct((B, S, w.shape[1]), F32) for w in (ws, wr, wm)],
        compiler_params=_params("parallel", "parallel"),
        name="input_proj",
    )(z, g, modsel, ws, wr, wm)


def _cpow(ar, ai, lag, shape, nbits=5):
    pr = jnp.ones(shape, F32)
    pi = jnp.zeros(shape, F32)
    for b in range(nbits):
        bit = ((lag >> b) & 1) == 1
        fr = jnp.where(bit, ar, 1.0)
        fi = jnp.where(bit, ai, 0.0)
        pr, pi = pr * fr - pi * fi, pr * fi + pi * fr
        ar, ai = ar * ar - ai * ai, 2.0 * ar * ai
    return pr, pi


def _ssm_prep_kernel(lc_re_ref, lc_im_ref, lr_re_ref, lr_im_ref, ldt_ref, bt_re_ref, bt_im_ref,
                     ct_re_ref, ct_im_ref, wt_ref, wso_ref, wsi_ref, a_ref):
    C, GC, P = SSM_CHUNK, SSM_GROUP, SSM_STATE
    W = C * GC
    lane = lax.broadcasted_iota(jnp.int32, (1, W), 1)
    quarter = lane // P
    is_re = (quarter == 0) | (quarter == 3)
    jcol = lane // GC
    srow = lax.broadcasted_iota(jnp.int32, (W, 1), 0) // GC
    for d in (0, 1):
        dt = jnp.exp(ldt_ref[d, 0])
        lr, li = lc_re_ref[d, 0], lc_im_ref[d, 0]
        mag = jnp.exp(lr * dt)
        ar, ai = mag * jnp.cos(li * dt), mag * jnp.sin(li * dt)
        cr, ci = ct_re_ref[d, 0], ct_im_ref[d, 0]
        lag_z = jcol if d == 0 else (C - 1) - jcol
        lag_s = jcol + 1 if d == 0 else C - jcol

        def q_of(lag):
            pr, pi = _cpow(ar, ai, lag, (P, W))
            q_re = cr * pr - ci * pi
            q_im = -(cr * pi + ci * pr)
            return q_re, q_im

        qz_re, qz_im = q_of(lag_z)
        qs_re, qs_im = q_of(lag_s)
        lr4, li4 = lr_re_ref[d, 0], lr_im_ref[d, 0]
        mag4 = jnp.exp(lr4 * dt)
        ar4, ai4 = mag4 * jnp.cos(li4 * dt), mag4 * jnp.sin(li4 * dt)
        den = lr4 * lr4 + li4 * li4
        nr, ni = ar4 - 1.0, ai4
        coef_re = (nr * lr4 + ni * li4) / den
        coef_im = (ni * lr4 - nr * li4) / den
        br, bi = bt_re_ref[0], bt_im_ref[0]
        bb_re = coef_re * br - coef_im * bi
        bb_im = coef_re * bi + coef_im * br
        bcat = jnp.where(lane < P, bb_re, bb_im)[:, :2 * P]
        qz = jnp.concatenate([qz_re, qz_im], axis=0)
        z = jnp.dot(bcat, qz, preferred_element_type=F32,
                    precision=lax.Precision.HIGHEST)
        rows = []
        for s in range(C):
            if d == 0:
                sh = pltpu.roll(z, GC * s, axis=1) if s else z
                rows.append(jnp.where(lane >= GC * s, sh, 0.0))
            else:
                m = C - 1 - s
                sh = pltpu.roll(z, W - GC * m, axis=1) if m else z
                rows.append(jnp.where(lane < W - GC * m, sh, 0.0))
        wt_ref[d, 0] = jnp.concatenate(rows, axis=0).astype(BF16)
        wsi_ref[d, 0] = jnp.concatenate(
            [qs_re, qs_im, jnp.zeros((2 * P, W), F32)], axis=0).astype(BF16)
        e_row = (C - 1) - srow if d == 0 else srow
        er, ei = _cpow(ar4, ai4, e_row, (W, W), nbits=4)
        bbx = jnp.where(is_re, bb_re, bb_im)
        bby = jnp.where(is_re, -bb_im, bb_re)
        bbx = jnp.concatenate([bbx] * C, axis=0)
        bby = jnp.concatenate([bby] * C, axis=0)
        wso_ref[d, 0] = (er * bbx + ei * bby).astype(BF16)
        cr16, ci16 = _cpow(ar4, ai4, jnp.full((1, W), C, jnp.int32), (1, W))
        a_ref[d, 0, 0:1, :] = cr16
        a_ref[d, 0, 1:2, :] = jnp.where(is_re, -ci16, ci16)


def _ssm_prep(lam_re, lam_im, log_dt, b_re, b_im, c_re, c_im):
    G, P, GC, C = SSM_GROUPS, SSM_STATE, SSM_GROUP, SSM_CHUNK
    W = C * GC
    lc_re, lc_im = lam_re[..., None], lam_im[..., None]
    lr_re = jnp.tile(lam_re, (1, 1, 4))[:, :, None, :]
    lr_im = jnp.tile(lam_im, (1, 1, 4))[:, :, None, :]
    ldt = log_dt[..., None, None]
    bt_re = jnp.tile(jnp.swapaxes(b_re, 1, 2), (1, 1, 4))
    bt_im = jnp.tile(jnp.swapaxes(b_im, 1, 2), (1, 1, 4))
    ct_re = jnp.tile(jnp.swapaxes(c_re, 2, 3), (1, 1, 1, C))
    ct_im = jnp.tile(jnp.swapaxes(c_im, 2, 3), (1, 1, 1, C))
    d4 = lambda a, b: pl.BlockSpec((2, 1, a, b), lambda g: (0, g, 0, 0))
    mat = jax.ShapeDtypeStruct((2, G, W, W), BF16)
    return pl.pallas_call(
        _ssm_prep_kernel,
        grid=(G,),
        in_specs=[d4(P, 1), d4(P, 1), d4(1, W), d4(1, W), d4(1, 1),
                  pl.BlockSpec((1, GC, W), lambda g: (g, 0, 0)),
                  pl.BlockSpec((1, GC, W), lambda g: (g, 0, 0)),
                  d4(P, W), d4(P, W)],
        out_specs=[d4(W, W), d4(W, W), d4(W, W), d4(2, W)],
        out_shape=[mat, mat, mat, jax.ShapeDtypeStruct((2, G, 2, W), F32)],
        compiler_params=_params("parallel"),
        name="ssm_prep",
    )(lc_re, lc_im, lr_re, lr_im, ldt, bt_re, bt_im, ct_re, ct_im)


def _ssm_kernel(u_ref, wt_ref, wso_ref, wsi_ref, a_ref, d_ref, y_ref, loc_ref, xs_ref, *, B, NC, NCc):
    u = u_ref[0]
    ub = u.astype(BF16)
    y = u * d_ref[0]
    for d in (0, 1):
        loc_ref[...] = _dot(ub, wso_ref[d, 0])
        a1 = a_ref[d, 0, 0:1, :]
        a2 = a_ref[d, 0, 1:2, :]

        def body(i, x, d=d, a1=a1, a2=a2):
            if d == 0:
                k = i
            else:
                k = jnp.where(i < NCc, NCc - 1 - i, NC - 1 - (i - NCc))
            r0 = pl.multiple_of(k * B, B)
            xs_ref[pl.ds(r0, B), :] = x
            return a1 * x + a2 * pltpu.roll(x, 128, axis=1) + loc_ref[pl.ds(r0, B), :]

        lax.fori_loop(0, NC, body, jnp.zeros((B, 256), F32))
        y = y + _dot(ub, wt_ref[d, 0]) + _dot(xs_ref[...].astype(BF16), wsi_ref[d, 0])
    y_ref[0] = y


def _ssm_scan(p_ssm, prep, d_skip, ctx_len):
    B, S, _ = p_ssm.shape
    G, GC, C = SSM_GROUPS, SSM_GROUP, SSM_CHUNK
    W = C * GC
    NC, NCc = S // C, ctx_len // C
    R = NC * B
    wt, wso, wsi, a4 = prep
    u = p_ssm.reshape(B, NC, C, G, GC).transpose(3, 1, 0, 2, 4).reshape(G, R, W)
    d4 = jnp.tile(d_skip.reshape(G, 1, GC), (1, 1, C))
    wspec = pl.BlockSpec((2, 1, W, W), lambda g: (0, g, 0, 0))
    y = pl.pallas_call(
        functools.partial(_ssm_kernel, B=B, NC=NC, NCc=NCc),
        grid=(G,),
        in_specs=[pl.BlockSpec((1, R, W), lambda g: (g, 0, 0)), wspec, wspec, wspec,
                  pl.BlockSpec((2, 1, 2, W), lambda g: (0, g, 0, 0)),
                  pl.BlockSpec((1, 1, W), lambda g: (g, 0, 0))],
        out_specs=pl.BlockSpec((1, R, W), lambda g: (g, 0, 0)),
        out_shape=jax.ShapeDtypeStruct((G, R, W), F32),
        scratch_shapes=[pltpu.VMEM((R, W), F32), pltpu.VMEM((R, W), F32)],
        compiler_params=_params("parallel"),
        name="ssm_scan",
    )(u, wt, wso, wsi, a4, d4)
    return y.reshape(G, NC, B, C, GC).transpose(2, 1, 3, 0, 4).reshape(B, S, G * GC)


def _rwkv_kernel(*refs, d, final, NC, NCc):
    if final:
        (p_ref, hp_ref, hn_ref, yp_ref, mu_ref, w0_ref, w2_ref, a0_ref, a2_ref, g2_ref, pv_ref,
         o_ref, s_ref, xb_ref, yb_ref) = refs
    else:
        (p_ref, hp_ref, hn_ref, mu_ref, w0_ref, w2_ref, a0_ref, a2_ref, g2_ref, pv_ref,
         o_ref, s_ref, xb_ref, yb_ref) = refs
    L, N, H, W = RWKV_CHUNK, RWKV_HEAD, RWKV_HEADS, RWKV_W
    ci = pl.program_id(1)
    if d == 0:
        c = ci
    else:
        c = jnp.where(ci < NCc, NCc - 1 - ci, NC - 1 - (ci - NCc))

    @pl.when(ci == 0)
    def _():
        s_ref[...] = jnp.zeros_like(s_ref)

    p = p_ref[0]
    xb_ref[0:8, :] = hp_ref[0]
    xb_ref[8:8 + L, :] = p
    xb_ref[8 + L:16 + L, :] = hn_ref[0]
    row = lax.broadcasted_iota(jnp.int32, (L, 1), 0)
    seg_first = jnp.where((c == 0) | (c == NCc), 1.0, 0.0)
    seg_last = jnp.where((c == NCc - 1) | (c == NC - 1), 1.0, 0.0)
    keep_prev = 1.0 - seg_first * jnp.where(row == 0, 1.0, 0.0)
    keep_next = 1.0 - seg_last * jnp.where(row == L - 1, 1.0, 0.0)
    prev = xb_ref[7:7 + L, :] * keep_prev
    nxt = xb_ref[9:9 + L, :] * keep_next
    x = p + mu_ref[...] * (0.5 * (prev + nxt) - p)

    r, k, v = x[:, 0:W], x[:, W:2 * W], x[:, 2 * W:3 * W]
    pw = x[:, 3 * W:3 * W + LORA_W]
    pa = x[:, 3 * W + LORA_W:3 * W + 2 * LORA_W]
    pg = x[:, 3 * W + 2 * LORA_W:3 * W + 3 * LORA_W]
    k_k, k_a, r_k = pv_ref[0:1, :], pv_ref[1:2, :], pv_ref[2:3, :]
    ln_w, ln_b = pv_ref[3:4, :], pv_ref[4:5, :]
    pab = pa.astype(BF16)

    zw = w0_ref[d:d + 1, :] + _dot(jnp.tanh(pw).astype(BF16), w2_ref[d])
    nz = -zw
    softplus = jnp.maximum(nz, 0.0) + jnp.log(1.0 + jnp.exp(-jnp.abs(nz)))
    lw = -jnp.exp(-softplus - 0.5)
    a = _sigmoid(a0_ref[d:d + 1, :] + _dot(pab, a2_ref[d]))
    kd = k * (1.0 + (a - 1.0) * k_a)

    hrow = lax.broadcasted_iota(jnp.int32, (W, W), 0) // N
    hcol = lax.broadcasted_iota(jnp.int32, (W, W), 1) // N
    head_ones = jnp.where(hrow == hcol, 1.0, 0.0).astype(BF16)

    kk = k * k_k
    kk = kk * lax.rsqrt(_dot_exact_rhs(kk * kk, head_ones, 2) + 1e-12)
    b = a * kk

    trow = lax.broadcasted_iota(jnp.int32, (L, L), 0)
    tcol = lax.broadcasted_iota(jnp.int32, (L, L), 1)
    if d == 0:
        strict, incl = tcol < trow, tcol <= trow
    else:
        strict, incl = tcol > trow, tcol >= trow
    cum = jnp.where(incl, 1.0, 0.0).astype(BF16)
    cs = _dot_exact_lhs(cum, lw, 3)
    last = L - 1 if d == 0 else 0
    cl = cs[last:last + 1, :]
    e_to_end = jnp.exp(cl - cs)
    e_neg = jnp.exp(-cs)
    rt = (r * jnp.exp(cs)).astype(BF16)
    at = (kk * jnp.exp(cs - lw)).astype(BF16)
    bt = (b * e_neg).astype(BF16)
    kt = (kd * e_neg).astype(BF16)
    kh = (kd * e_to_end).astype(BF16)
    bh = (b * e_to_end).astype(BF16)
    e_chunk = jnp.exp(cl)
    vb = v.astype(BF16)

    for h in range(H):
        sl = slice(h * N, (h + 1) * N)
        A, R, Bt, Kt, V = at[:, sl], rt[:, sl], bt[:, sl], kt[:, sl], vb[:, sl]
        s0 = s_ref[h]
        s0b = s0.astype(BF16)
        nab = jnp.where(strict, _dot_nt(A, Bt), 0.0)
        nak = jnp.where(strict, _dot_nt(A, Kt), 0.0)
        mrk = jnp.where(incl, _dot_nt(R, Kt), 0.0)
        mrb = jnp.where(incl, _dot_nt(R, Bt), 0.0)
        nb = nab.astype(BF16)
        pm = -nab
        q = _dot(nb, nb)
        steps = int(math.log2(L)) - 1
        for it in range(steps):
            qb = q.astype(BF16)
            pm = pm + q + _dot(pm.astype(BF16), qb)
            if it + 1 < steps:
                q = _dot(qb, qb)
        w1 = _dot_nt(A, s0b) + _dot(nak.astype(BF16), V)
        u = w1 + _dot(pm.astype(BF16), w1.astype(BF16))
        ub = u.astype(BF16)
        y = _dot_nt(R, s0b) + _dot(mrk.astype(BF16), V) - _dot(mrb.astype(BF16), ub)
        s_ref[h] = s0 * e_chunk[:, sl] + _dot_tn(V, kh[:, sl]) - _dot_tn(ub, bh[:, sl])
        yb_ref[:, sl] = y

    if not final:
        o_ref[0] = yb_ref[...]
        return
    y = yp_ref[0] + yb_ref[...]
    mean = _dot_exact_rhs(y, head_ones, 2) * (1.0 / N)
    dev = y - mean
    var = _dot_exact_rhs(dev * dev, head_ones, 2) * (1.0 / N)
    yn = dev * lax.rsqrt(var + GN_EPS) * ln_w + ln_b
    o = 1 - d
    a_o = _sigmoid(a0_ref[o:o + 1, :] + _dot(pab, a2_ref[o]))
    kd_sum = kd + k * (1.0 + (a_o - 1.0) * k_a)
    bonus = _dot_exact_rhs(r * kd_sum * r_k, head_ones, 2) * v
    g = _dot(_sigmoid(pg).astype(BF16), g2_ref[...])
    o_ref[0] = (yn + bonus) * g


def _rwkv_dir(p_rwkv, y_prev, params, d, ctx_len):
    B, S, PW = p_rwkv.shape
    L, W = RWKV_CHUNK, RWKV_W
    NC, NCc = S // L, ctx_len // L
    final = y_prev is not None

    def chunk(ci):
        if d == 0:
            return ci
        return jnp.where(ci < NCc, NCc - 1 - ci, NC - 1 - (ci - NCc))

    hb = L // 8
    tok = lambda n: pl.BlockSpec((1, L, n), lambda b, ci: (b, chunk(ci), 0))
    full = lambda a: pl.BlockSpec(a.shape, lambda b, ci: (0,) * a.ndim)
    in_specs = [tok(PW),
                pl.BlockSpec((1, 8, PW), lambda b, ci: (b, jnp.maximum(chunk(ci) * hb - 1, 0), 0)),
                pl.BlockSpec((1, 8, PW), lambda b, ci: (b, jnp.minimum((chunk(ci) + 1) * hb, S // 8 - 1), 0))]
    args = [p_rwkv, p_rwkv, p_rwkv]
    if final:
        in_specs.append(tok(W))
        args.append(y_prev)
    in_specs += [full(a) for a in params]
    args += list(params)
    return pl.pallas_call(
        functools.partial(_rwkv_kernel, d=d, final=final, NC=NC, NCc=NCc),
        grid=(B, NC),
        in_specs=in_specs,
        out_specs=tok(W),
        out_shape=jax.ShapeDtypeStruct((B, S, W), F32),
        scratch_shapes=[pltpu.VMEM((RWKV_HEADS, RWKV_HEAD, RWKV_HEAD), F32),
                        pltpu.VMEM((L + 16, PW), F32),
                        pltpu.VMEM((L, W), F32)],
        compiler_params=_params("parallel", "arbitrary"),
        name="rwkv_rev" if d else "rwkv_fwd",
    )(*args)


def _rwkv_params(mu, w0, w2, a0, a2, g2, k_k, k_a, r_k, ln_w, ln_b):
    W = RWKV_W
    half = LORA_W // 2

    def pad_dir(w):
        out = jnp.zeros((2, LORA_W, W), F32)
        out = out.at[0, :half].set(w[0]).at[1, half:].set(w[1])
        return out.astype(BF16)

    pv = jnp.zeros((8, W), F32)
    pv = pv.at[0].set(k_k).at[1].set(k_a).at[2].set(r_k.reshape(W)).at[3].set(ln_w).at[4].set(ln_b)
    return (mu.reshape(1, RWKV_IN), w0, pad_dir(w2), a0, pad_dir(a2), g2.astype(BF16), pv)


def _mla_prep_kernel(p_ref, cq_ref, sq_ref, ck_ref, sk_ref, qn_ref, kvn_ref, wq_ref, wk_ref, wv_ref,
                     gn_ref, q_ref, k_ref, v_ref):
    HP = HEAD_PAD
    p = p_ref[0]
    lane = lax.broadcasted_iota(jnp.int32, (1, HP), 1)
    m_nope = jnp.where(lane < MLA_NOPE, 1.0, 0.0)
    m_rope = jnp.where(lane < MLA_NOPE, 0.0, jnp.where(lane < MLA_NOPE + MLA_ROPE, 1.0, 0.0))
    g_q, g_kn, g_kr = gn_ref[0:1, :], gn_ref[1:2, :], gn_ref[2:3, :]

    def rms(x, n):
        return lax.rsqrt(jnp.sum(x * x, axis=-1, keepdims=True) * (1.0 / n) + NORM_EPS)

    xq = p[:, :Q_LORA]
    cq = (xq * rms(xq, Q_LORA) * qn_ref[...]).astype(BF16)
    xkv = p[:, Q_LORA:Q_LORA + KV_LORA]
    ckv = (xkv * rms(xkv, KV_LORA) * kvn_ref[...]).astype(BF16)
    q = _dot(cq, wq_ref[...])
    kn = _dot(ckv, wk_ref[...])
    v_ref[0] = _dot(ckv, wv_ref[...]).astype(BF16)

    kr = p[:, Q_LORA + KV_LORA:Q_LORA + KV_LORA + HP]
    krn = kr * rms(kr, MLA_ROPE) * g_kr
    to_rope = pltpu.roll(krn, MLA_NOPE, axis=1)
    sw_a = jnp.where(lane >= 112, 0.0, jnp.where(lane >= 96, pltpu.roll(krn, 80, axis=1), 0.0))
    sw_b = jnp.where(lane >= 112, pltpu.roll(krn, 112, axis=1), 0.0)
    kext = to_rope + sw_a + sw_b
    krot = kext * ck_ref[...] + pltpu.roll(kext, HP - MLA_ROPE, axis=1) * sk_ref[...]

    cq_t, sq_t = cq_ref[...], sq_ref[...]
    for h in range(MLA_HEADS):
        sl = slice(h * HP, (h + 1) * HP)
        qh = q[:, sl]
        scale = m_nope * rms(qh * m_nope, MLA_NOPE) + (1.0 - m_nope) * rms(qh * m_rope, MLA_ROPE)
        qn = qh * scale * g_q
        q_ref[0, :, sl] = (qn * cq_t + pltpu.roll(qn, HP - MLA_ROPE, axis=1) * sq_t).astype(BF16)
        kh = kn[:, sl]
        k_ref[0, :, sl] = (kh * rms(kh, MLA_NOPE) * g_kn + krot).astype(BF16)


def _mla_prep(p_mla, tables, params, ctx_len):
    B, S, PW = p_mla.shape
    tm = _token_tile(ctx_len, S - ctx_len)
    cq_t, sq_t, ck_t, sk_t = tables
    HW = MLA_HEADS * HEAD_PAD
    tok = lambda n: pl.BlockSpec((1, tm, n), lambda b, i: (b, i, 0))
    tab = pl.BlockSpec((tm, HEAD_PAD), lambda b, i: (i, 0))
    full = lambda a: pl.BlockSpec(a.shape, lambda b, i: (0,) * a.ndim)
    return pl.pallas_call(
        _mla_prep_kernel,
        grid=(B, S // tm),
        in_specs=[tok(PW), tab, tab, tab, tab] + [full(a) for a in params],
        out_specs=[tok(HW), tok(HW), tok(MLA_HEADS * MLA_V)],
        out_shape=[jax.ShapeDtypeStruct((B, S, HW), BF16), jax.ShapeDtypeStruct((B, S, HW), BF16),
                   jax.ShapeDtypeStruct((B, S, MLA_HEADS * MLA_V), BF16)],
        compiler_params=_params("parallel", "parallel"),
        name="mla_prep",
    )(p_mla, cq_t, sq_t, ck_t, sk_t, *params)


def _mla_params(q_norm, kv_norm, w_uq, w_ukv, qn_nope, kn_nope, qn_rope, kn_rope):
    H, NP, RP, HP = MLA_HEADS, MLA_NOPE, MLA_ROPE, HEAD_PAD
    half = RP // 2
    swap = jnp.concatenate([jnp.arange(half, RP), jnp.arange(0, half)])
    wq = w_uq.reshape(Q_LORA, H, NP + RP)
    wq = jnp.concatenate([wq, wq[:, :, NP + swap]], axis=-1).reshape(Q_LORA, H * HP)
    wkv = w_ukv.reshape(KV_LORA, H, NP + MLA_V)
    wk = jnp.concatenate([wkv[:, :, :NP], jnp.zeros((KV_LORA, H, HP - NP), F32)], axis=-1)
    wk = wk.reshape(KV_LORA, H * HP)
    wv = wkv[:, :, NP:].reshape(KV_LORA, H * MLA_V)
    gn = jnp.zeros((8, HP), F32)
    gn = gn.at[0].set(jnp.concatenate([qn_nope, qn_rope, qn_rope[swap]]))
    gn = gn.at[1, :NP].set(kn_nope).at[2, :RP].set(kn_rope)
    return (q_norm.reshape(1, Q_LORA), kv_norm.reshape(1, KV_LORA),
            wq.astype(BF16), wk.astype(BF16), wv.astype(BF16), gn)


def _rope_tables(ctx_len, seq):
    rows = seq // GRID_W
    axis_dims = MLA_ROPE // 2
    row = jnp.repeat(jnp.arange(rows), GRID_W).astype(F32)
    col = jnp.tile(jnp.arange(GRID_W), rows).astype(F32)
    inv = ROPE_BASE ** (-jnp.arange(0, axis_dims, 2, dtype=F32) / axis_dims)
    ang = jnp.concatenate([row[:, None] * inv, col[:, None] * inv], axis=-1)
    cos = jnp.concatenate([jnp.ones((ctx_len, axis_dims), F32), jnp.cos(ang)], axis=0)
    sin = jnp.concatenate([jnp.zeros((ctx_len, axis_dims), F32), jnp.sin(ang)], axis=0)
    S = ctx_len + seq
    pad = jnp.zeros((S, HEAD_PAD - MLA_NOPE - MLA_ROPE), F32)
    cos_t = jnp.concatenate([jnp.ones((S, MLA_NOPE), F32), cos, cos, pad], axis=-1)
    sin_t = jnp.concatenate([jnp.zeros((S, MLA_NOPE), F32), -sin, sin, pad], axis=-1)
    return cos_t * MLA_SCALE, sin_t * MLA_SCALE, cos_t, sin_t


def _attn_kernel(q_ref, k_ref, v_ref, o_ref, *, nct, ctx_len):
    S = k_ref.shape[1]
    i = pl.program_id(2)
    col = lax.broadcasted_iota(jnp.int32, (1, S), 1)
    hide = jnp.where(col >= ctx_len, -1e30, 0.0) * jnp.where(i < nct, 1.0, 0.0)
    for hh in range(2):
        qh = q_ref[0, :, hh * HEAD_PAD:(hh + 1) * HEAD_PAD]
        kh = k_ref[0, :, hh * HEAD_PAD:(hh + 1) * HEAD_PAD]
        vh = v_ref[0, :, hh * MLA_V:(hh + 1) * MLA_V]
        s = _dot_nt(qh, kh) + hide
        e = jnp.exp(s - jnp.max(s, axis=-1, keepdims=True))
        o = _dot(e.astype(BF16), vh) / jnp.sum(e, axis=-1, keepdims=True)
        o_ref[0, :, hh * MLA_V:(hh + 1) * MLA_V] = o.astype(BF16)


def _attention(q, k, v, ctx_len):
    B, S, _ = q.shape
    tq = _token_tile(ctx_len, S - ctx_len)
    return pl.pallas_call(
        functools.partial(_attn_kernel, nct=ctx_len // tq, ctx_len=ctx_len),
        grid=(B, MLA_HEADS // 2, S // tq),
        in_specs=[pl.BlockSpec((1, tq, 2 * HEAD_PAD), lambda b, h, i: (b, i, h)),
                  pl.BlockSpec((1, S, 2 * HEAD_PAD), lambda b, h, i: (b, 0, h)),
                  pl.BlockSpec((1, S, 2 * MLA_V), lambda b, h, i: (b, 0, h))],
        out_specs=pl.BlockSpec((1, tq, 2 * MLA_V), lambda b, h, i: (b, i, h)),
        out_shape=jax.ShapeDtypeStruct((B, S, MLA_HEADS * MLA_V), BF16),
        compiler_params=_params("parallel", "parallel", "parallel"),
        name="mla_attention",
    )(q, k, v)


def _merge_kernel(z_ref, g1_ref, g2_ref, mod_ref, ys_ref, yr_ref, ym_ref, wg_ref, wb_ref, wo_ref,
                  glw_ref, glb_ref, rw_ref, zo_ref, h2_ref, lg_ref):
    D = D_MODEL
    z = z_ref[0]
    m = mod_ref[0, 0]
    h = _norm_mod(z, g1_ref[...], m[0:1], m[1:2]).astype(BF16)
    ys = ys_ref[0]
    ys = 0.5 * ys * (1.0 + jnp.tanh(math.sqrt(2.0 / math.pi) * (ys + 0.044715 * ys * ys * ys)))
    ys = ys * _sigmoid(_dot(ys.astype(BF16), glw_ref[...]) + glb_ref[...])
    branches = (ys.astype(BF16), yr_ref[0].astype(BF16), ym_ref[0])
    acc = jnp.zeros(z.shape, F32)
    for j, yj in enumerate(branches):
        gate = _sigmoid(_dot(h, wg_ref[:, j * D:(j + 1) * D]))
        acc = acc + gate * _dot(yj, wb_ref[j])
    zn = z + m[2:3] * _dot(acc.astype(BF16), wo_ref[...])
    zo_ref[0] = zn
    h2 = _norm_mod(zn, g2_ref[...], m[3:4], m[4:5])
    h2_ref[0] = h2.astype(BF16)
    h2p = _split3(h2)
    rw = _split3(rw_ref[...])
    lg = None
    for ia, ha in enumerate(h2p):
        for ib, rb in enumerate(rw):
            if ia + ib <= 2:
                t = _dot_nt(rb, ha)
                lg = t if lg is None else lg + t
    lg_ref[0] = lg


def _merge(z, g1, g2, modsel, ys, yr, ym, wg, wb, wo, glw, glb, rwt, ctx_len):
    B, S, D = z.shape
    tm = _token_tile(ctx_len, S - ctx_len)
    nct = ctx_len // tm
    E = rwt.shape[0]
    tok = lambda n: pl.BlockSpec((1, tm, n), lambda b, i: (b, i, 0))
    full = lambda a: pl.BlockSpec(a.shape, lambda b, i: (0,) * a.ndim)
    W = BRANCH_WIDTH
    return pl.pallas_call(
        _merge_kernel,
        grid=(B, S // tm),
        in_specs=[tok(D), full(g1), full(g2),
                  pl.BlockSpec((1, 1, N_MOD, D), lambda b, i: (b, jnp.where(i >= nct, 1, 0), 0, 0)),
                  tok(W), tok(W), tok(W), full(wg), full(wb), full(wo), full(glw), full(glb), full(rwt)],
        out_specs=[tok(D), tok(D), pl.BlockSpec((1, E, tm), lambda b, i: (b, 0, i))],
        out_shape=[jax.ShapeDtypeStruct((B, S, D), F32), jax.ShapeDtypeStruct((B, S, D), BF16),
                   jax.ShapeDtypeStruct((B, E, S), F32)],
        compiler_params=_params("parallel", "parallel"),
        name="merge",
    )(z, g1, g2, modsel, ys, yr, ym, wg, wb, wo, glw, glb, rwt)


def _route_kernel(lg_ref, tri_ref, slot_ref, gate_ref, *, ctx_len, cap_c, cap_l):
    lg = lg_ref[0]
    E, S = lg.shape
    e = jnp.exp(lg - jnp.max(lg, axis=0, keepdims=True))
    aff = e / jnp.sum(e, axis=0, keepdims=True)
    bits = pltpu.bitcast(aff, jnp.int32)
    lane = lax.broadcasted_iota(jnp.int32, (1, S), 1)
    tri = tri_ref[...]
    slot = jnp.full((E, S), -1.0, F32)
    sel_all = jnp.zeros((E, S), F32)
    for in_set, cap, off in ((lane < ctx_len, cap_c, 0), (lane >= ctx_len, cap_l, cap_c)):
        ms = jnp.where(in_set, 1.0, 0.0)

        def body(_, carry, ms=ms, cap=cap):
            lo, hi = carry
            mid = lo + ((hi - lo + 1) >> 1)
            cnt = jnp.sum(jnp.where(bits >= mid, ms, 0.0), axis=1, keepdims=True)
            ok = cnt >= cap
            return jnp.where(ok, mid, lo), jnp.where(ok, hi, mid - 1)

        lo, _ = lax.fori_loop(0, 31, body, (jnp.zeros((E, 1), jnp.int32),
                                            jnp.full((E, 1), 0x7F800000, jnp.int32)))
        gt = jnp.where(bits > lo, ms, 0.0)
        eq = jnp.where(bits == lo, ms, 0.0)
        need = cap - jnp.sum(gt, axis=1, keepdims=True)
        eq_rank = _dot(eq.astype(BF16), tri) - eq
        sel = gt + eq * jnp.where(eq_rank < need, 1.0, 0.0)
        rank = _dot(sel.astype(BF16), tri) - sel
        slot = jnp.where(sel > 0.0, rank + off, slot)
        sel_all = sel_all + sel
    slot_ref[0] = slot
    gate_ref[0] = aff * sel_all


def _route(logits_t, tri, ctx_len, cap_c, cap_l):
    B, E, S = logits_t.shape
    spec = pl.BlockSpec((1, E, S), lambda b: (b, 0, 0))
    return pl.pallas_call(
        functools.partial(_route_kernel, ctx_len=ctx_len, cap_c=cap_c, cap_l=cap_l),
        grid=(B,),
        in_specs=[spec, pl.BlockSpec((S, S), lambda b: (0, 0))],
        out_specs=[spec, spec],
        out_shape=[jax.ShapeDtypeStruct((B, E, S), F32)] * 2,
        compiler_params=_params("parallel"),
        name="route",
    )(logits_t, tri)


def _expert_kernel(h_ref, slot_ref, gate_ref, w1_ref, w3_ref, w2_ref, o_ref, *, cap):
    slot = slot_ref[0, 0]
    gate = gate_ref[0, 0]
    S = slot.shape[1]
    cidx = lax.broadcasted_iota(jnp.int32, (cap, S), 0).astype(F32)
    hit = slot == cidx
    onehot = jnp.where(hit, 1.0, 0.0).astype(BF16)
    xs = _dot(onehot, h_ref[0]).astype(BF16)
    gc = jnp.sum(jnp.where(hit, gate, 0.0), axis=1, keepdims=True)
    a1 = _dot(xs, w1_ref[0])
    a3 = _dot(xs, w3_ref[0])
    hid = (a1 * _sigmoid(a1) * a3).astype(BF16)
    o_ref[0, 0] = (_dot(hid, w2_ref[0]) * gc).astype(BF16)


def _experts(h2, slot, gate, w1, w3, w2, cap):
    B, S, D = h2.shape
    E, _, F = w1.shape
    row = pl.BlockSpec((1, 1, 1, S), lambda e, b: (b, e, 0, 0))
    return pl.pallas_call(
        functools.partial(_expert_kernel, cap=cap),
        grid=(E, B),
        in_specs=[pl.BlockSpec((1, S, D), lambda e, b: (b, 0, 0)), row, row,
                  pl.BlockSpec((1, D, F), lambda e, b: (e, 0, 0)),
                  pl.BlockSpec((1, D, F), lambda e, b: (e, 0, 0)),
                  pl.BlockSpec((1, F, D), lambda e, b: (e, 0, 0))],
        out_specs=pl.BlockSpec((1, 1, cap, D), lambda e, b: (b, e, 0, 0)),
        out_shape=jax.ShapeDtypeStruct((B, E, cap, D), BF16),
        compiler_params=_params("parallel", "parallel"),
        name="experts",
    )(h2, slot.reshape(B, E, 1, S), gate.reshape(B, E, 1, S), w1, w3, w2)


def _scatter_kernel(z_ref, mod_ref, st_ref, y_ref, o_ref, *, cap):
    st = st_ref[0]
    tm, E = st.shape
    cidx = lax.broadcasted_iota(jnp.int32, (tm, cap), 1).astype(F32)
    acc = jnp.zeros(z_ref.shape[1:], F32)
    for e in range(E):
        onehot = jnp.where(st[:, e:e + 1] == cidx, 1.0, 0.0).astype(BF16)
        acc = acc + _dot(onehot, y_ref[0, e])
    o_ref[0] = z_ref[0] + mod_ref[0, 0][5:6] * acc


def _scatter(z, modsel, slot_t, yc, ctx_len):
    B, S, D = z.shape
    _, E, cap, _ = yc.shape
    tm = _token_tile(ctx_len, S - ctx_len)
    nct = ctx_len // tm
    tok = lambda n: pl.BlockSpec((1, tm, n), lambda b, i: (b, i, 0))
    return pl.pallas_call(
        functools.partial(_scatter_kernel, cap=cap),
        grid=(B, S // tm),
        in_specs=[tok(D),
                  pl.BlockSpec((1, 1, N_MOD, D), lambda b, i: (b, jnp.where(i >= nct, 1, 0), 0, 0)),
                  tok(E),
                  pl.BlockSpec((1, E, cap, D), lambda b, i: (b, 0, 0, 0))],
        out_specs=tok(D),
        out_shape=jax.ShapeDtypeStruct((B, S, D), F32),
        compiler_params=_params("parallel", "parallel"),
        name="moe_scatter",
    )(z, modsel, slot_t, yc)


def kernel(x, c, ctx, c_ctx, ada_w, ada_b, norm1_g, norm2_g, w_in, ssm_lambda_re, ssm_lambda_im, ssm_log_dt, ssm_b_re, ssm_b_im, ssm_c_re, ssm_c_im, ssm_d, ssm_glu_w, ssm_glu_b, rwkv_mu, rwkv_w0, rwkv_w2, rwkv_a0, rwkv_a2, rwkv_g2, rwkv_k_k, rwkv_k_a, rwkv_r_k, rwkv_ln_w, rwkv_ln_b, mla_q_norm, mla_kv_norm, mla_w_uq, mla_w_ukv, mla_qn_nope, mla_kn_nope, mla_qn_rope, mla_kn_rope, w_branch, w_out, router_w, moe_w1, moe_w3, moe_w2):
    B, T, D = x.shape
    CTX = ctx.shape[1]
    S = CTX + T
    depth = ada_w.shape[0]
    cap_c = EC_CAPACITY * CTX // N_EXPERTS
    cap_l = EC_CAPACITY * T // N_EXPERTS
    assert D == D_MODEL and T % GRID_W == 0 and cap_c % 8 == 0 and cap_l % 8 == 0 and B % 8 == 0

    rows = -(-(B + 1) // 8) * 8
    cc = jnp.concatenate([c, c_ctx[None, :], jnp.zeros((rows - B - 1, D), F32)], axis=0)
    mods = _mods(cc, ada_w, ada_b).reshape(depth, rows, N_MOD, D)
    mod_lat = mods[:, :B]
    mod_ctx = jnp.broadcast_to(mods[:, B:B + 1], mod_lat.shape)
    modsel = jnp.stack([mod_ctx, mod_lat], axis=2)

    tables = _rope_tables(CTX, T)
    pos = jnp.arange(S)
    same = (pos[:, None] < CTX) == (pos[None, :] < CTX)
    tri = ((pos[:, None] <= pos[None, :]) & same).astype(BF16)

    z = jnp.concatenate([ctx, x], axis=1)
    for l in range(depth):
        g1, g2 = norm1_g[l].reshape(1, D), norm2_g[l].reshape(1, D)
        wl = w_in[l]
        w_ssm = wl[:, :RWKV_OFF].astype(BF16)
        w_rwkv = wl[:, RWKV_OFF:MLA_OFF].astype(BF16)
        w_mla = jnp.pad(wl[:, MLA_OFF:GATE_OFF], ((0, 0), (0, MLA_IN_PAD - MLA_IN))).astype(BF16)
        w_gate = wl[:, GATE_OFF:].astype(BF16)
        p_ssm, p_rwkv, p_mla = _input_proj(z, g1, modsel[l], w_ssm, w_rwkv, w_mla, CTX)

        prep = _ssm_prep(ssm_lambda_re[l], ssm_lambda_im[l], ssm_log_dt[l], ssm_b_re[l], ssm_b_im[l],
                         ssm_c_re[l], ssm_c_im[l])
        y_ssm = _ssm_scan(p_ssm, prep, ssm_d[l], CTX)

        rp = _rwkv_params(rwkv_mu[l], rwkv_w0[l], rwkv_w2[l], rwkv_a0[l], rwkv_a2[l], rwkv_g2[l],
                          rwkv_k_k[l], rwkv_k_a[l], rwkv_r_k[l], rwkv_ln_w[l], rwkv_ln_b[l])
        y_fwd = _rwkv_dir(p_rwkv, None, rp, 0, CTX)
        y_rwkv = _rwkv_dir(p_rwkv, y_fwd, rp, 1, CTX)

        mp = _mla_params(mla_q_norm[l], mla_kv_norm[l], mla_w_uq[l], mla_w_ukv[l], mla_qn_nope[l],
                         mla_kn_nope[l], mla_qn_rope[l], mla_kn_rope[l])
        q, k, v = _mla_prep(p_mla, tables, mp, CTX)
        y_mla = _attention(q, k, v, CTX)

        z, h2, logits_t = _merge(z, g1, g2, modsel[l], y_ssm, y_rwkv, y_mla, w_gate,
                                 w_branch[l].astype(BF16), w_out[l].astype(BF16),
                                 ssm_glu_w[l].astype(BF16), ssm_glu_b[l].reshape(1, -1),
                                 router_w[l].T, CTX)
        slot, gate = _route(logits_t, tri, CTX, cap_c, cap_l)
        yc = _experts(h2, slot, gate, moe_w1[l].astype(BF16), moe_w3[l].astype(BF16),
                      moe_w2[l].astype(BF16), cap_c + cap_l)
        z = _scatter(z, modsel[l], jnp.swapaxes(slot, 1, 2), yc, CTX)
    return z[:, CTX:]
```

```python
import functools
import math

import jax
import jax.numpy as jnp
from jax import lax
from jax.experimental import pallas as pl
from jax.experimental.pallas import tpu as pltpu

F32 = jnp.float32
BF16 = jnp.bfloat16

D_MODEL = 1024
GRID_W = 64
N_MOD = 6
NORM_EPS = 1e-6
GN_EPS = 64e-5
BRANCH_WIDTH = 512
SSM_GROUP = 16
SSM_GROUPS = BRANCH_WIDTH // SSM_GROUP
SSM_STATE = 64
SSM_CHUNK = 16
RWKV_W = BRANCH_WIDTH
RWKV_HEAD = 64
RWKV_HEADS = RWKV_W // RWKV_HEAD
RWKV_CHUNK = 64
LORA_W = 128
RWKV_IN = 3 * RWKV_W + 3 * LORA_W
MLA_HEADS = 8
MLA_NOPE = 64
MLA_ROPE = 32
MLA_V = 64
Q_LORA = 384
KV_LORA = 256
MLA_IN = Q_LORA + KV_LORA + MLA_ROPE
MLA_IN_PAD = 768
MLA_SCALE = 1.0 / math.sqrt(MLA_NOPE + MLA_ROPE)
ROPE_BASE = 10000.0
HEAD_PAD = 128
RWKV_OFF = BRANCH_WIDTH
MLA_OFF = RWKV_OFF + RWKV_IN
GATE_OFF = MLA_OFF + MLA_IN
N_EXPERTS = 16
EXPERT_FF = 1536
EC_CAPACITY = 2
VMEM_LIMIT = 56 * 1024 * 1024


def _params(*sem):
    return pltpu.CompilerParams(dimension_semantics=sem, vmem_limit_bytes=VMEM_LIMIT)


def _dot(a, b):
    return jnp.dot(a, b, preferred_element_type=F32)


def _dot_nt(a, b):
    return lax.dot_general(a, b, (((1,), (1,)), ((), ())), preferred_element_type=F32)


def _dot_tn(a, b):
    return lax.dot_general(a, b, (((0,), (0,)), ((), ())), preferred_element_type=F32)


def _split2(x):
    hi = x.astype(BF16)
    lo = (x - hi.astype(F32)).astype(BF16)
    return hi, lo


def _split3(x):
    hi = x.astype(BF16)
    r1 = x - hi.astype(F32)
    mid = r1.astype(BF16)
    lo = (r1 - mid.astype(F32)).astype(BF16)
    return hi, mid, lo


def _dot_exact_rhs(x, m, parts):
    pieces = _split2(x) if parts == 2 else _split3(x)
    acc = _dot(pieces[0], m)
    for p in pieces[1:]:
        acc = acc + _dot(p, m)
    return acc


def _dot_exact_lhs(m, x, parts):
    pieces = _split2(x) if parts == 2 else _split3(x)
    acc = _dot(m, pieces[0])
    for p in pieces[1:]:
        acc = acc + _dot(m, p)
    return acc


def _sigmoid(x):
    return 0.5 * jnp.tanh(0.5 * x) + 0.5


def _norm_mod(x, g, shift, scale):
    y = x * lax.rsqrt(jnp.mean(x * x, axis=-1, keepdims=True) + NORM_EPS)
    return (y * g) * (1.0 + scale) + shift


def _token_tile(ctx_len, seq):
    for tm in (256, 128, 64):
        if ctx_len % tm == 0 and seq % tm == 0:
            return tm
    raise ValueError("context and latent lengths must be multiples of 64")


def _mods_kernel(cc_ref, w_ref, b_ref, o_ref):
    c = cc_ref[...]
    s = (c * _sigmoid(c)).astype(BF16)
    o_ref[0] = _dot(s, w_ref[0].astype(BF16)) + b_ref[0]


def _mods(cc, ada_w, ada_b):
    L, D, N = ada_w.shape
    R = cc.shape[0]
    tn = 1536
    return pl.pallas_call(
        _mods_kernel,
        grid=(L, N // tn),
        in_specs=[pl.BlockSpec((R, D), lambda l, j: (0, 0)),
                  pl.BlockSpec((1, D, tn), lambda l, j: (l, 0, j)),
                  pl.BlockSpec((1, 1, tn), lambda l, j: (l, 0, j))],
        out_specs=pl.BlockSpec((1, R, tn), lambda l, j: (l, 0, j)),
        out_shape=jax.ShapeDtypeStruct((L, R, N), F32),
        compiler_params=_params("parallel", "parallel"),
        name="adaln_mods",
    )(cc, ada_w, ada_b.reshape(L, 1, N))


def _kin_kernel(z_ref, g_ref, mod_ref, ws_ref, wr_ref, wm_ref, os_ref, or_ref, om_ref):
    m = mod_ref[0, 0]
    h = _norm_mod(z_ref[0], g_ref[...], m[0:1], m[1:2]).astype(BF16)
    os_ref[...] = _dot(h, ws_ref[...])
    or_ref[0] = _dot(h, wr_ref[...])
    om_ref[0] = _dot(h, wm_ref[...])


def _input_proj(z, g, modsel, ws, wr, wm, ctx_len):
    B, S, D = z.shape
    tm = _token_tile(ctx_len, S - ctx_len)
    nct = ctx_len // tm
    tok = lambda n: pl.BlockSpec((1, tm, n), lambda b, i: (b, i, 0))
    full = lambda a: pl.BlockSpec(a.shape, lambda b, i: (0,) * a.ndim)
    return pl.pallas_call(
        _kin_kernel,
        grid=(B, S // tm),
        in_specs=[tok(D), full(g),
                  pl.BlockSpec((1, 1, N_MOD, D), lambda b, i: (b, jnp.where(i >= nct, 1, 0), 0, 0)),
                  full(ws), full(wr), full(wm)],
        out_specs=[pl.BlockSpec((tm, ws.shape[1]), lambda b, i: (i, b)), tok(wr.shape[1]), tok(wm.shape[1])],
        out_shape=[jax.ShapeDtypeStruct((S, B * ws.shape[1]), F32),
                   jax.ShapeDtypeStruct((B, S, wr.shape[1]), F32),
                   jax.ShapeDtypeStruct((B, S, wm.shape[1]), F32)],
        compiler_params=_params("parallel", "parallel"),
        name="input_proj",
    )(z, g, modsel, ws, wr, wm)


def _cpow(ar, ai, lag, shape, nbits=5):
    pr = jnp.ones(shape, F32)
    pi = jnp.zeros(shape, F32)
    for b in range(nbits):
        bit = ((lag >> b) & 1) == 1
        fr = jnp.where(bit, ar, 1.0)
        fi = jnp.where(bit, ai, 0.0)
        pr, pi = pr * fr - pi * fi, pr * fi + pi * fr
        ar, ai = ar * ar - ai * ai, 2.0 * ar * ai
    return pr, pi


def _ssm_prep_kernel(lc_re_ref, lc_im_ref, lr_re_ref, lr_im_ref, ldt_ref, bt_re_ref, bt_im_ref,
                     ct_re_ref, ct_im_ref, wt_ref, wso_ref, wsi_ref, a_ref):
    C, GC, P = SSM_CHUNK, SSM_GROUP, SSM_STATE
    W = C * GC
    lane = lax.broadcasted_iota(jnp.int32, (1, W), 1)
    quarter = lane // P
    is_re = (quarter == 0) | (quarter == 3)
    jcol = lane // GC
    srow = lax.broadcasted_iota(jnp.int32, (W, 1), 0) // GC
    for d in (0, 1):
        dt = jnp.exp(ldt_ref[d, 0])
        lr, li = lc_re_ref[d, 0], lc_im_ref[d, 0]
        mag = jnp.exp(lr * dt)
        ar, ai = mag * jnp.cos(li * dt), mag * jnp.sin(li * dt)
        cr, ci = ct_re_ref[d, 0], ct_im_ref[d, 0]
        lag_z = jcol if d == 0 else (C - 1) - jcol
        lag_s = jcol + 1 if d == 0 else C - jcol

        def q_of(lag):
            pr, pi = _cpow(ar, ai, lag, (P, W))
            q_re = cr * pr - ci * pi
            q_im = -(cr * pi + ci * pr)
            return q_re, q_im

        qz_re, qz_im = q_of(lag_z)
        qs_re, qs_im = q_of(lag_s)
        lr4, li4 = lr_re_ref[d, 0], lr_im_ref[d, 0]
        mag4 = jnp.exp(lr4 * dt)
        ar4, ai4 = mag4 * jnp.cos(li4 * dt), mag4 * jnp.sin(li4 * dt)
        den = lr4 * lr4 + li4 * li4
        nr, ni = ar4 - 1.0, ai4
        coef_re = (nr * lr4 + ni * li4) / den
        coef_im = (ni * lr4 - nr * li4) / den
        br, bi = bt_re_ref[0], bt_im_ref[0]
        bb_re = coef_re * br - coef_im * bi
        bb_im = coef_re * bi + coef_im * br
        bcat = jnp.where(lane < P, bb_re, bb_im)[:, :2 * P]
        qz = jnp.concatenate([qz_re, qz_im], axis=0)
        z = jnp.dot(bcat, qz, preferred_element_type=F32,
                    precision=lax.Precision.HIGHEST)
        rows = []
        for s in range(C):
            if d == 0:
                sh = pltpu.roll(z, GC * s, axis=1) if s else z
                rows.append(jnp.where(lane >= GC * s, sh, 0.0))
            else:
                m = C - 1 - s
                sh = pltpu.roll(z, W - GC * m, axis=1) if m else z
                rows.append(jnp.where(lane < W - GC * m, sh, 0.0))
        wt_ref[d, 0] = jnp.concatenate(rows, axis=0).astype(BF16)
        wsi_ref[d, 0] = jnp.concatenate([qs_re, qs_im], axis=0).astype(BF16)
        e_row = (C - 1) - srow if d == 0 else srow
        er, ei = _cpow(ar4, ai4, e_row, (W, W), nbits=4)
        bbx = jnp.where(is_re, bb_re, bb_im)
        bby = jnp.where(is_re, -bb_im, bb_re)
        bbx = jnp.concatenate([bbx] * C, axis=0)
        bby = jnp.concatenate([bby] * C, axis=0)
        wso_ref[d, 0] = (er * bbx + ei * bby).astype(BF16)
        cr16, ci16 = _cpow(ar4, ai4, jnp.full((1, W), C, jnp.int32), (1, W))
        a_ref[d, 0, 0:1, :] = cr16
        a_ref[d, 0, 1:2, :] = jnp.where(is_re, -ci16, ci16)


def _ssm_prep(lam_re, lam_im, log_dt, b_re, b_im, c_re, c_im):
    G, P, GC, C = SSM_GROUPS, SSM_STATE, SSM_GROUP, SSM_CHUNK
    W = C * GC
    lc_re, lc_im = lam_re[..., None], lam_im[..., None]
    lr_re = jnp.tile(lam_re, (1, 1, 4))[:, :, None, :]
    lr_im = jnp.tile(lam_im, (1, 1, 4))[:, :, None, :]
    ldt = log_dt[..., None, None]
    bt_re = jnp.tile(jnp.swapaxes(b_re, 1, 2), (1, 1, 4))
    bt_im = jnp.tile(jnp.swapaxes(b_im, 1, 2), (1, 1, 4))
    ct_re = jnp.tile(jnp.swapaxes(c_re, 2, 3), (1, 1, 1, C))
    ct_im = jnp.tile(jnp.swapaxes(c_im, 2, 3), (1, 1, 1, C))
    d4 = lambda a, b: pl.BlockSpec((2, 1, a, b), lambda g: (0, g, 0, 0))
    mat = jax.ShapeDtypeStruct((2, G, W, W), BF16)
    return pl.pallas_call(
        _ssm_prep_kernel,
        grid=(G,),
        in_specs=[d4(P, 1), d4(P, 1), d4(1, W), d4(1, W), d4(1, 1),
                  pl.BlockSpec((1, GC, W), lambda g: (g, 0, 0)),
                  pl.BlockSpec((1, GC, W), lambda g: (g, 0, 0)),
                  d4(P, W), d4(P, W)],
        out_specs=[d4(W, W), d4(W, W), d4(2 * P, W), d4(2, W)],
        out_shape=[mat, mat, jax.ShapeDtypeStruct((2, G, 2 * P, W), BF16),
                   jax.ShapeDtypeStruct((2, G, 2, W), F32)],
        compiler_params=_params("parallel"),
        name="ssm_prep",
    )(lc_re, lc_im, lr_re, lr_im, ldt, bt_re, bt_im, ct_re, ct_im)


def _ssm_kernel(*refs, d, final, KT):
    if final:
        x_ref, yp_ref, wt_ref, wso_ref, wsi_ref, a_ref, dn_ref, o_ref, st_ref = refs
    else:
        x_ref, wt_ref, wso_ref, wsi_ref, a_ref, o_ref, st_ref = refs
    C, GC, P = SSM_CHUNK, SSM_GROUP, SSM_STATE
    BT = x_ref.shape[2]
    GB = x_ref.shape[3] // GC
    R = KT * BT

    @pl.when(pl.program_id(2) == 0)
    def _():
        st_ref[...] = jnp.zeros_like(st_ref)

    xt = [x_ref[:, t].reshape(R, GB * GC) for t in range(C)]
    gs = range(GB)
    ub = [jnp.concatenate([xt[t][:, g * GC:(g + 1) * GC] for t in range(C)], axis=1).astype(BF16) for g in gs]
    loc = [_dot(ub[g], wso_ref[0, g]) for g in gs]
    a1 = [a_ref[0, g, 0:1, :] for g in gs]
    a2 = [a_ref[0, g, 1:2, :] for g in gs]
    x = [st_ref[g] for g in gs]
    xs = [[None] * KT for _ in gs]
    for k in (range(KT) if d == 0 else range(KT - 1, -1, -1)):
        for g in gs:
            xs[g][k] = x[g][:, :2 * P]
            x[g] = a1[g] * x[g] + a2[g] * pltpu.roll(x[g], 2 * P, axis=1) + loc[g][k * BT:(k + 1) * BT]
    ys = []
    for g in gs:
        st_ref[g] = x[g]
        xin = jnp.concatenate(xs[g], axis=0).astype(BF16)
        ys.append(_dot(ub[g], wt_ref[0, g]) + _dot(xin, wsi_ref[0, g]))
    for t in range(C):
        yt = jnp.concatenate([ys[g][:, t * GC:(t + 1) * GC] for g in gs], axis=1)
        if final:
            yt = yt + yp_ref[:, t].reshape(R, GB * GC) + xt[t] * dn_ref[...]
        o_ref[:, t] = yt.reshape(KT, BT, GB * GC)


def _ssm_dir(p_tm, y_prev, prep, d_skip, d, B, ctx_len):
    S = p_tm.shape[0]
    G, GC, C, P = SSM_GROUPS, SSM_GROUP, SSM_CHUNK, SSM_STATE
    W = C * GC
    LW = 128
    GB = LW // GC
    BT = 8
    KT = _token_tile(ctx_len, S - ctx_len) // C
    NT, NTc = S // (C * KT), ctx_len // (C * KT)
    wt, wso, wsi, a4 = prep
    final = y_prev is not None

    def tile(ti):
        if d == 0:
            return ti
        return jnp.where(ti < NTc, NTc - 1 - ti, NT - 1 - (ti - NTc))

    x4 = p_tm.reshape(S // C, C, B, G * GC)
    xspec = pl.BlockSpec((KT, C, BT, LW), lambda gb, bt, ti: (tile(ti), 0, bt, gb))
    wspec = lambda r: pl.BlockSpec((1, GB, r, W), lambda gb, bt, ti: (d, gb, 0, 0))
    in_specs, args = [xspec], [x4]
    if final:
        in_specs.append(xspec)
        args.append(y_prev.reshape(x4.shape))
    in_specs += [wspec(W), wspec(W), wspec(2 * P), wspec(2)]
    args += [wt, wso, wsi, a4]
    if final:
        in_specs.append(pl.BlockSpec((1, LW), lambda gb, bt, ti: (0, gb)))
        args.append(d_skip.reshape(1, G * GC))
    y = pl.pallas_call(
        functools.partial(_ssm_kernel, d=d, final=final, KT=KT),
        grid=(G // GB, B // BT, NT),
        in_specs=in_specs,
        out_specs=xspec,
        out_shape=jax.ShapeDtypeStruct(x4.shape, F32),
        scratch_shapes=[pltpu.VMEM((GB, BT, W), F32)],
        compiler_params=_params("parallel", "parallel", "arbitrary"),
        name="ssm_rev" if d else "ssm_fwd",
    )(*args)
    return y.reshape(p_tm.shape)


def _ssm_scan(p_tm, prep, d_skip, B, ctx_len):
    y_fwd = _ssm_dir(p_tm, None, prep, d_skip, 0, B, ctx_len)
    return _ssm_dir(p_tm, y_fwd, prep, d_skip, 1, B, ctx_len)


def _rwkv_kernel(*refs, d, final, NC, NCc):
    if final:
        (p_ref, hp_ref, hn_ref, yp_ref, mu_ref, w0_ref, w2_ref, a0_ref, a2_ref, g2_ref, pv_ref,
         o_ref, s_ref, xb_ref, yb_ref) = refs
    else:
        (p_ref, hp_ref, hn_ref, mu_ref, w0_ref, w2_ref, a0_ref, a2_ref, g2_ref, pv_ref,
         o_ref, s_ref, xb_ref, yb_ref) = refs
    L, N, H, W = RWKV_CHUNK, RWKV_HEAD, RWKV_HEADS, RWKV_W
    NB = p_ref.shape[0]
    ci = pl.program_id(1)
    if d == 0:
        c = ci
    else:
        c = jnp.where(ci < NCc, NCc - 1 - ci, NC - 1 - (ci - NCc))

    @pl.when(ci == 0)
    def _():
        s_ref[...] = jnp.zeros_like(s_ref)

    row = lax.broadcasted_iota(jnp.int32, (L, 1), 0)
    seg_first = jnp.where((c == 0) | (c == NCc), 1.0, 0.0)
    seg_last = jnp.where((c == NCc - 1) | (c == NC - 1), 1.0, 0.0)
    keep_prev = 1.0 - seg_first * jnp.where(row == 0, 1.0, 0.0)
    keep_next = 1.0 - seg_last * jnp.where(row == L - 1, 1.0, 0.0)
    xs = []
    for bi in range(NB):
        p = p_ref[bi]
        xb_ref[bi, 0:8, :] = hp_ref[bi]
        xb_ref[bi, 8:8 + L, :] = p
        xb_ref[bi, 8 + L:16 + L, :] = hn_ref[bi]
        prev = xb_ref[bi, 7:7 + L, :] * keep_prev
        nxt = xb_ref[bi, 9:9 + L, :] * keep_next
        xs.append(p + mu_ref[...] * (0.5 * (prev + nxt) - p))
    x = jnp.concatenate(xs, axis=0)

    r, k, v = x[:, 0:W], x[:, W:2 * W], x[:, 2 * W:3 * W]
    pw = x[:, 3 * W:3 * W + LORA_W]
    pa = x[:, 3 * W + LORA_W:3 * W + 2 * LORA_W]
    pg = x[:, 3 * W + 2 * LORA_W:3 * W + 3 * LORA_W]
    k_k, k_a, r_k = pv_ref[0:1, :], pv_ref[1:2, :], pv_ref[2:3, :]
    ln_w, ln_b = pv_ref[3:4, :], pv_ref[4:5, :]
    pab = pa.astype(BF16)

    zw = w0_ref[d:d + 1, :] + _dot(jnp.tanh(pw).astype(BF16), w2_ref[d])
    nz = -zw
    softplus = jnp.maximum(nz, 0.0) + jnp.log(1.0 + jnp.exp(-jnp.abs(nz)))
    lw = -jnp.exp(-softplus - 0.5)
    a = _sigmoid(a0_ref[d:d + 1, :] + _dot(pab, a2_ref[d]))
    kd = k * (1.0 + (a - 1.0) * k_a)

    hrow = lax.broadcasted_iota(jnp.int32, (W, W), 0) // N
    hcol = lax.broadcasted_iota(jnp.int32, (W, W), 1) // N
    head_ones = jnp.where(hrow == hcol, 1.0, 0.0).astype(BF16)

    kk = k * k_k
    kk = kk * lax.rsqrt(_dot_exact_rhs(kk * kk, head_ones, 2) + 1e-12)
    b = a * kk

    trow = lax.broadcasted_iota(jnp.int32, (L, L), 0)
    tcol = lax.broadcasted_iota(jnp.int32, (L, L), 1)
    if d == 0:
        strict, incl = tcol < trow, tcol <= trow
    else:
        strict, incl = tcol > trow, tcol >= trow
    brow = lax.broadcasted_iota(jnp.int32, (NB * L, NB * L), 0)
    bcol = lax.broadcasted_iota(jnp.int32, (NB * L, NB * L), 1)
    upto = (bcol <= brow) if d == 0 else (bcol >= brow)
    cum = jnp.where(upto, jnp.where((brow // L) == (bcol // L), 1.0, 0.0), 0.0).astype(BF16)
    cs = _dot_exact_lhs(cum, lw, 3)
    last = L - 1 if d == 0 else 0
    cls = [cs[bi * L + last:bi * L + last + 1, :] for bi in range(NB)]
    cl = jnp.concatenate([jnp.broadcast_to(t, (L, W)) for t in cls], axis=0)
    e_to_end = jnp.exp(cl - cs)
    e_neg = jnp.exp(-cs)
    rt = (r * jnp.exp(cs)).astype(BF16)
    at = (kk * jnp.exp(cs - lw)).astype(BF16)
    bt = (b * e_neg).astype(BF16)
    kt = (kd * e_neg).astype(BF16)
    kh = (kd * e_to_end).astype(BF16)
    bh = (b * e_to_end).astype(BF16)
    e_chunk = [jnp.exp(t) for t in cls]
    vb = v.astype(BF16)

    ids = [(bi, h) for bi in range(NB) for h in range(H)]
    n = range(len(ids))
    rs = [slice(bi * L, (bi + 1) * L) for bi, _ in ids]
    ls = [slice(h * N, (h + 1) * N) for _, h in ids]
    ar = [jnp.concatenate([at[rs[i], ls[i]], rt[rs[i], ls[i]]], axis=0) for i in n]
    s0 = [s_ref[bi, h] for bi, h in ids]
    g_b = [_dot_nt(ar[i], bt[rs[i], ls[i]]) for i in n]
    g_k = [_dot_nt(ar[i], kt[rs[i], ls[i]]) for i in n]
    g_s = [_dot_nt(ar[i], s0[i].astype(BF16)) for i in n]
    nab = [jnp.where(strict, g_b[i][:L], 0.0) for i in n]
    mrb = [jnp.where(incl, g_b[i][L:], 0.0).astype(BF16) for i in n]
    nm = [jnp.concatenate([jnp.where(strict, g_k[i][:L], 0.0), jnp.where(incl, g_k[i][L:], 0.0)],
                          axis=0).astype(BF16) for i in n]
    nv = [g_s[i] + _dot(nm[i], vb[rs[i], ls[i]]) for i in n]
    nb = [t.astype(BF16) for t in nab]
    pm = [-t for t in nab]
    q = [_dot(t, t) for t in nb]
    steps = int(math.log2(L)) - 1
    for it in range(steps):
        qb = [t.astype(BF16) for t in q]
        pm = [pm[i] + q[i] + _dot(pm[i].astype(BF16), qb[i]) for i in n]
        if it + 1 < steps:
            q = [_dot(t, t) for t in qb]
    u = [nv[i][:L] + _dot(pm[i].astype(BF16), nv[i][:L].astype(BF16)) for i in n]
    ub = [t.astype(BF16) for t in u]
    for i in n:
        yb_ref[rs[i], ls[i]] = nv[i][L:] - _dot(mrb[i], ub[i])
    for i, (bi, h) in enumerate(ids):
        vu = jnp.concatenate([vb[rs[i], ls[i]], -ub[i]], axis=0)
        kb = jnp.concatenate([kh[rs[i], ls[i]], bh[rs[i], ls[i]]], axis=0)
        s_ref[bi, h] = s0[i] * e_chunk[bi][:, ls[i]] + _dot_tn(vu, kb)

    if not final:
        o_ref[...] = yb_ref[...].reshape(NB, L, W)
        return
    y = yp_ref[...].reshape(NB * L, W) + yb_ref[...]
    mean = _dot_exact_rhs(y, head_ones, 2) * (1.0 / N)
    dev = y - mean
    var = _dot_exact_rhs(dev * dev, head_ones, 2) * (1.0 / N)
    yn = dev * lax.rsqrt(var + GN_EPS) * ln_w + ln_b
    o = 1 - d
    a_o = _sigmoid(a0_ref[o:o + 1, :] + _dot(pab, a2_ref[o]))
    kd_sum = kd + k * (1.0 + (a_o - 1.0) * k_a)
    bonus = _dot_exact_rhs(r * kd_sum * r_k, head_ones, 2) * v
    g = _dot(_sigmoid(pg).astype(BF16), g2_ref[...])
    o_ref[...] = ((yn + bonus) * g).reshape(NB, L, W)


def _rwkv_dir(p_rwkv, y_prev, params, d, ctx_len):
    B, S, PW = p_rwkv.shape
    L, W = RWKV_CHUNK, RWKV_W
    NC, NCc = S // L, ctx_len // L
    NB = 2
    final = y_prev is not None

    def chunk(ci):
        if d == 0:
            return ci
        return jnp.where(ci < NCc, NCc - 1 - ci, NC - 1 - (ci - NCc))

    hb = L // 8
    tok = lambda n: pl.BlockSpec((NB, L, n), lambda b, ci: (b, chunk(ci), 0))
    full = lambda a: pl.BlockSpec(a.shape, lambda b, ci: (0,) * a.ndim)
    in_specs = [tok(PW),
                pl.BlockSpec((NB, 8, PW), lambda b, ci: (b, jnp.maximum(chunk(ci) * hb - 1, 0), 0)),
                pl.BlockSpec((NB, 8, PW), lambda b, ci: (b, jnp.minimum((chunk(ci) + 1) * hb, S // 8 - 1), 0))]
    args = [p_rwkv, p_rwkv, p_rwkv]
    if final:
        in_specs.append(tok(W))
        args.append(y_prev)
    in_specs += [full(a) for a in params]
    args += list(params)
    return pl.pallas_call(
        functools.partial(_rwkv_kernel, d=d, final=final, NC=NC, NCc=NCc),
        grid=(B // NB, NC),
        in_specs=in_specs,
        out_specs=tok(W),
        out_shape=jax.ShapeDtypeStruct((B, S, W), F32),
        scratch_shapes=[pltpu.VMEM((NB, RWKV_HEADS, RWKV_HEAD, RWKV_HEAD), F32),
                        pltpu.VMEM((NB, L + 16, PW), F32),
                        pltpu.VMEM((NB * L, W), F32)],
        compiler_params=_params("parallel", "arbitrary"),
        name="rwkv_rev" if d else "rwkv_fwd",
    )(*args)


def _rwkv_params(mu, w0, w2, a0, a2, g2, k_k, k_a, r_k, ln_w, ln_b):
    W = RWKV_W
    half = LORA_W // 2

    def pad_dir(w):
        out = jnp.zeros((2, LORA_W, W), F32)
        out = out.at[0, :half].set(w[0]).at[1, half:].set(w[1])
        return out.astype(BF16)

    pv = jnp.zeros((8, W), F32)
    pv = pv.at[0].set(k_k).at[1].set(k_a).at[2].set(r_k.reshape(W)).at[3].set(ln_w).at[4].set(ln_b)
    return (mu.reshape(1, RWKV_IN), w0, pad_dir(w2), a0, pad_dir(a2), g2.astype(BF16), pv)


def _mla_prep_kernel(p_ref, cq_ref, sq_ref, ck_ref, sk_ref, qn_ref, kvn_ref, wq_ref, wk_ref, wv_ref,
                     gn_ref, q_ref, k_ref, v_ref):
    HP = HEAD_PAD
    p = p_ref[0]
    lane = lax.broadcasted_iota(jnp.int32, (1, HP), 1)
    m_nope = jnp.where(lane < MLA_NOPE, 1.0, 0.0)
    m_rope = jnp.where(lane < MLA_NOPE, 0.0, jnp.where(lane < MLA_NOPE + MLA_ROPE, 1.0, 0.0))
    g_q, g_kn, g_kr = gn_ref[0:1, :], gn_ref[1:2, :], gn_ref[2:3, :]

    def rms(x, n):
        return lax.rsqrt(jnp.sum(x * x, axis=-1, keepdims=True) * (1.0 / n) + NORM_EPS)

    xq = p[:, :Q_LORA]
    cq = (xq * rms(xq, Q_LORA) * qn_ref[...]).astype(BF16)
    xkv = p[:, Q_LORA:Q_LORA + KV_LORA]
    ckv = (xkv * rms(xkv, KV_LORA) * kvn_ref[...]).astype(BF16)
    q = _dot(cq, wq_ref[...])
    kn = _dot(ckv, wk_ref[...])
    v_ref[0] = _dot(ckv, wv_ref[...]).astype(BF16)

    kr = p[:, Q_LORA + KV_LORA:Q_LORA + KV_LORA + HP]
    krn = kr * rms(kr, MLA_ROPE) * g_kr
    to_rope = pltpu.roll(krn, MLA_NOPE, axis=1)
    sw_a = jnp.where(lane >= 112, 0.0, jnp.where(lane >= 96, pltpu.roll(krn, 80, axis=1), 0.0))
    sw_b = jnp.where(lane >= 112, pltpu.roll(krn, 112, axis=1), 0.0)
    kext = to_rope + sw_a + sw_b
    krot = kext * ck_ref[...] + pltpu.roll(kext, HP - MLA_ROPE, axis=1) * sk_ref[...]

    cq_t, sq_t = cq_ref[...], sq_ref[...]
    for h in range(MLA_HEADS):
        sl = slice(h * HP, (h + 1) * HP)
        qh = q[:, sl]
        scale = m_nope * rms(qh * m_nope, MLA_NOPE) + (1.0 - m_nope) * rms(qh * m_rope, MLA_ROPE)
        qn = qh * scale * g_q
        q_ref[0, :, sl] = (qn * cq_t + pltpu.roll(qn, HP - MLA_ROPE, axis=1) * sq_t).astype(BF16)
        kh = kn[:, sl]
        k_ref[0, :, sl] = (kh * rms(kh, MLA_NOPE) * g_kn + krot).astype(BF16)


def _mla_prep(p_mla, tables, params, ctx_len):
    B, S, PW = p_mla.shape
    tm = _token_tile(ctx_len, S - ctx_len)
    cq_t, sq_t, ck_t, sk_t = tables
    HW = MLA_HEADS * HEAD_PAD
    tok = lambda n: pl.BlockSpec((1, tm, n), lambda b, i: (b, i, 0))
    tab = pl.BlockSpec((tm, HEAD_PAD), lambda b, i: (i, 0))
    full = lambda a: pl.BlockSpec(a.shape, lambda b, i: (0,) * a.ndim)
    return pl.pallas_call(
        _mla_prep_kernel,
        grid=(B, S // tm),
        in_specs=[tok(PW), tab, tab, tab, tab] + [full(a) for a in params],
        out_specs=[tok(HW), tok(HW), tok(MLA_HEADS * MLA_V)],
        out_shape=[jax.ShapeDtypeStruct((B, S, HW), BF16), jax.ShapeDtypeStruct((B, S, HW), BF16),
                   jax.ShapeDtypeStruct((B, S, MLA_HEADS * MLA_V), BF16)],
        compiler_params=_params("parallel", "parallel"),
        name="mla_prep",
    )(p_mla, cq_t, sq_t, ck_t, sk_t, *params)


def _mla_params(q_norm, kv_norm, w_uq, w_ukv, qn_nope, kn_nope, qn_rope, kn_rope):
    H, NP, RP, HP = MLA_HEADS, MLA_NOPE, MLA_ROPE, HEAD_PAD
    half = RP // 2
    swap = jnp.concatenate([jnp.arange(half, RP), jnp.arange(0, half)])
    wq = w_uq.reshape(Q_LORA, H, NP + RP)
    wq = jnp.concatenate([wq, wq[:, :, NP + swap]], axis=-1).reshape(Q_LORA, H * HP)
    wkv = w_ukv.reshape(KV_LORA, H, NP + MLA_V)
    wk = jnp.concatenate([wkv[:, :, :NP], jnp.zeros((KV_LORA, H, HP - NP), F32)], axis=-1)
    wk = wk.reshape(KV_LORA, H * HP)
    wv = wkv[:, :, NP:].reshape(KV_LORA, H * MLA_V)
    gn = jnp.zeros((8, HP), F32)
    gn = gn.at[0].set(jnp.concatenate([qn_nope, qn_rope, qn_rope[swap]]))
    gn = gn.at[1, :NP].set(kn_nope).at[2, :RP].set(kn_rope)
    return (q_norm.reshape(1, Q_LORA), kv_norm.reshape(1, KV_LORA),
            wq.astype(BF16), wk.astype(BF16), wv.astype(BF16), gn)


def _rope_tables(ctx_len, seq):
    rows = seq // GRID_W
    axis_dims = MLA_ROPE // 2
    row = jnp.repeat(jnp.arange(rows), GRID_W).astype(F32)
    col = jnp.tile(jnp.arange(GRID_W), rows).astype(F32)
    inv = ROPE_BASE ** (-jnp.arange(0, axis_dims, 2, dtype=F32) / axis_dims)
    ang = jnp.concatenate([row[:, None] * inv, col[:, None] * inv], axis=-1)
    cos = jnp.concatenate([jnp.ones((ctx_len, axis_dims), F32), jnp.cos(ang)], axis=0)
    sin = jnp.concatenate([jnp.zeros((ctx_len, axis_dims), F32), jnp.sin(ang)], axis=0)
    S = ctx_len + seq
    pad = jnp.zeros((S, HEAD_PAD - MLA_NOPE - MLA_ROPE), F32)
    cos_t = jnp.concatenate([jnp.ones((S, MLA_NOPE), F32), cos, cos, pad], axis=-1)
    sin_t = jnp.concatenate([jnp.zeros((S, MLA_NOPE), F32), -sin, sin, pad], axis=-1)
    return cos_t * MLA_SCALE, sin_t * MLA_SCALE, cos_t, sin_t


def _attn_kernel(q_ref, k_ref, v_ref, o_ref, *, nct, ctx_len):
    S = k_ref.shape[1]
    i = pl.program_id(2)

    def attend(nk):
        for hh in range(2):
            qh = q_ref[0, :, hh * HEAD_PAD:(hh + 1) * HEAD_PAD]
            kh = k_ref[0, :nk, hh * HEAD_PAD:(hh + 1) * HEAD_PAD]
            vh = v_ref[0, :nk, hh * MLA_V:(hh + 1) * MLA_V]
            s = _dot_nt(qh, kh)
            e = jnp.exp(s - jnp.max(s, axis=-1, keepdims=True))
            o = _dot(e.astype(BF16), vh) / jnp.sum(e, axis=-1, keepdims=True)
            o_ref[0, :, hh * MLA_V:(hh + 1) * MLA_V] = o.astype(BF16)

    @pl.when(i < nct)
    def _():
        attend(ctx_len)

    @pl.when(i >= nct)
    def _():
        attend(S)


def _attention(q, k, v, ctx_len):
    B, S, _ = q.shape
    tq = _token_tile(ctx_len, S - ctx_len)
    return pl.pallas_call(
        functools.partial(_attn_kernel, nct=ctx_len // tq, ctx_len=ctx_len),
        grid=(B, MLA_HEADS // 2, S // tq),
        in_specs=[pl.BlockSpec((1, tq, 2 * HEAD_PAD), lambda b, h, i: (b, i, h)),
                  pl.BlockSpec((1, S, 2 * HEAD_PAD), lambda b, h, i: (b, 0, h)),
                  pl.BlockSpec((1, S, 2 * MLA_V), lambda b, h, i: (b, 0, h))],
        out_specs=pl.BlockSpec((1, tq, 2 * MLA_V), lambda b, h, i: (b, i, h)),
        out_shape=jax.ShapeDtypeStruct((B, S, MLA_HEADS * MLA_V), BF16),
        compiler_params=_params("parallel", "parallel", "parallel"),
        name="mla_attention",
    )(q, k, v)


def _merge_kernel(z_ref, g1_ref, g2_ref, mod_ref, ys_ref, yr_ref, ym_ref, wg_ref, wb_ref, wo_ref,
                  glw_ref, glb_ref, rw_ref, zo_ref, h2_ref, lg_ref):
    D = D_MODEL
    z = z_ref[0]
    m = mod_ref[0, 0]
    h = _norm_mod(z, g1_ref[...], m[0:1], m[1:2]).astype(BF16)
    ys = ys_ref[...]
    ys = 0.5 * ys * (1.0 + jnp.tanh(math.sqrt(2.0 / math.pi) * (ys + 0.044715 * ys * ys * ys)))
    ys = ys * _sigmoid(_dot(ys.astype(BF16), glw_ref[...]) + glb_ref[...])
    branches = (ys.astype(BF16), yr_ref[0].astype(BF16), ym_ref[0])
    acc = jnp.zeros(z.shape, F32)
    for j, yj in enumerate(branches):
        gate = _sigmoid(_dot(h, wg_ref[:, j * D:(j + 1) * D]))
        acc = acc + gate * _dot(yj, wb_ref[j])
    zn = z + m[2:3] * _dot(acc.astype(BF16), wo_ref[...])
    zo_ref[0] = zn
    h2 = _norm_mod(zn, g2_ref[...], m[3:4], m[4:5])
    h2_ref[0] = h2.astype(BF16)
    h2p = _split3(h2)
    rw = _split3(rw_ref[...])
    lg = None
    for ia, ha in enumerate(h2p):
        for ib, rb in enumerate(rw):
            if ia + ib <= 2:
                t = _dot_nt(rb, ha)
                lg = t if lg is None else lg + t
    lg_ref[0] = lg


def _merge(z, g1, g2, modsel, ys, yr, ym, wg, wb, wo, glw, glb, rwt, ctx_len):
    B, S, D = z.shape
    tm = _token_tile(ctx_len, S - ctx_len)
    nct = ctx_len // tm
    E = rwt.shape[0]
    tok = lambda n: pl.BlockSpec((1, tm, n), lambda b, i: (b, i, 0))
    full = lambda a: pl.BlockSpec(a.shape, lambda b, i: (0,) * a.ndim)
    W = BRANCH_WIDTH
    return pl.pallas_call(
        _merge_kernel,
        grid=(B, S // tm),
        in_specs=[tok(D), full(g1), full(g2),
                  pl.BlockSpec((1, 1, N_MOD, D), lambda b, i: (b, jnp.where(i >= nct, 1, 0), 0, 0)),
                  pl.BlockSpec((tm, W), lambda b, i: (i, b)),
                  tok(W), tok(W), full(wg), full(wb), full(wo), full(glw), full(glb), full(rwt)],
        out_specs=[tok(D), tok(D), pl.BlockSpec((1, E, tm), lambda b, i: (b, 0, i))],
        out_shape=[jax.ShapeDtypeStruct((B, S, D), F32), jax.ShapeDtypeStruct((B, S, D), BF16),
                   jax.ShapeDtypeStruct((B, E, S), F32)],
        compiler_params=_params("parallel", "parallel"),
        name="merge",
    )(z, g1, g2, modsel, ys, yr, ym, wg, wb, wo, glw, glb, rwt)


def _route_kernel(lg_ref, tri_ref, slot_ref, gate_ref, *, ctx_len, cap_c, cap_l):
    lg = lg_ref[0]
    E, S = lg.shape
    e = jnp.exp(lg - jnp.max(lg, axis=0, keepdims=True))
    aff = e / jnp.sum(e, axis=0, keepdims=True)
    bits = pltpu.bitcast(aff, jnp.int32)
    lane = lax.broadcasted_iota(jnp.int32, (1, S), 1)
    tri = tri_ref[...]
    slot = jnp.full((E, S), -1.0, F32)
    sel_all = jnp.zeros((E, S), F32)
    for in_set, cap, off in ((lane < ctx_len, cap_c, 0), (lane >= ctx_len, cap_l, cap_c)):
        ms = jnp.where(in_set, 1.0, 0.0)

        def body(_, carry, ms=ms, cap=cap):
            lo, hi = carry
            mid = lo + ((hi - lo + 1) >> 1)
            cnt = jnp.sum(jnp.where(bits >= mid, ms, 0.0), axis=1, keepdims=True)
            ok = cnt >= cap
            return jnp.where(ok, mid, lo), jnp.where(ok, hi, mid - 1)

        lo, _ = lax.fori_loop(0, 31, body, (jnp.zeros((E, 1), jnp.int32),
                                            jnp.full((E, 1), 0x7F800000, jnp.int32)))
        gt = jnp.where(bits > lo, ms, 0.0)
        eq = jnp.where(bits == lo, ms, 0.0)
        need = cap - jnp.sum(gt, axis=1, keepdims=True)
        eq_rank = _dot(eq.astype(BF16), tri) - eq
        sel = gt + eq * jnp.where(eq_rank < need, 1.0, 0.0)
        rank = _dot(sel.astype(BF16), tri) - sel
        slot = jnp.where(sel > 0.0, rank + off, slot)
        sel_all = sel_all + sel
    slot_ref[0] = slot
    gate_ref[0] = aff * sel_all


def _route(logits_t, tri, ctx_len, cap_c, cap_l):
    B, E, S = logits_t.shape
    spec = pl.BlockSpec((1, E, S), lambda b: (b, 0, 0))
    return pl.pallas_call(
        functools.partial(_route_kernel, ctx_len=ctx_len, cap_c=cap_c, cap_l=cap_l),
        grid=(B,),
        in_specs=[spec, pl.BlockSpec((S, S), lambda b: (0, 0))],
        out_specs=[spec, spec],
        out_shape=[jax.ShapeDtypeStruct((B, E, S), F32)] * 2,
        compiler_params=_params("parallel"),
        name="route",
    )(logits_t, tri)


def _expert_kernel(h_ref, slot_ref, gate_ref, w1_ref, w3_ref, w2_ref, o_ref, *, cap, cap_c, ctx_len):
    S = h_ref.shape[1]
    xs, gc = [], []
    for (c0, c1), (t0, t1) in (((0, cap_c), (0, ctx_len)), ((cap_c, cap), (ctx_len, S))):
        slot = slot_ref[0, 0, :, t0:t1]
        gate = gate_ref[0, 0, :, t0:t1]
        cidx = lax.broadcasted_iota(jnp.int32, (c1 - c0, t1 - t0), 0).astype(F32) + float(c0)
        hit = slot == cidx
        onehot = jnp.where(hit, 1.0, 0.0).astype(BF16)
        xs.append(_dot(onehot, h_ref[0, t0:t1, :]))
        gc.append(jnp.sum(jnp.where(hit, gate, 0.0), axis=1, keepdims=True))
    xs = jnp.concatenate(xs, axis=0).astype(BF16)
    gc = jnp.concatenate(gc, axis=0)
    a1 = _dot(xs, w1_ref[0])
    a3 = _dot(xs, w3_ref[0])
    hid = (a1 * _sigmoid(a1) * a3).astype(BF16)
    o_ref[0, 0] = (_dot(hid, w2_ref[0]) * gc).astype(BF16)


def _experts(h2, slot, gate, w1, w3, w2, cap, cap_c, ctx_len):
    B, S, D = h2.shape
    E, _, F = w1.shape
    row = pl.BlockSpec((1, 1, 1, S), lambda e, b: (b, e, 0, 0))
    return pl.pallas_call(
        functools.partial(_expert_kernel, cap=cap, cap_c=cap_c, ctx_len=ctx_len),
        grid=(E, B),
        in_specs=[pl.BlockSpec((1, S, D), lambda e, b: (b, 0, 0)), row, row,
                  pl.BlockSpec((1, D, F), lambda e, b: (e, 0, 0)),
                  pl.BlockSpec((1, D, F), lambda e, b: (e, 0, 0)),
                  pl.BlockSpec((1, F, D), lambda e, b: (e, 0, 0))],
        out_specs=pl.BlockSpec((1, 1, cap, D), lambda e, b: (b, e, 0, 0)),
        out_shape=jax.ShapeDtypeStruct((B, E, cap, D), BF16),
        compiler_params=_params("parallel", "parallel"),
        name="experts",
    )(h2, slot.reshape(B, E, 1, S), gate.reshape(B, E, 1, S), w1, w3, w2)


def _scatter_kernel(z_ref, mod_ref, st_ref, y_ref, o_ref, *, cap_c, cap, nct):
    st = st_ref[0]
    tm, E = st.shape

    def add(lo, hi):
        cidx = lax.broadcasted_iota(jnp.int32, (tm, hi - lo), 1).astype(F32) + float(lo)
        acc = jnp.zeros(z_ref.shape[1:], F32)
        for e in range(E):
            onehot = jnp.where(st[:, e:e + 1] == cidx, 1.0, 0.0).astype(BF16)
            acc = acc + _dot(onehot, y_ref[0, e, lo:hi, :])
        o_ref[0] = z_ref[0] + mod_ref[0, 0][5:6] * acc

    @pl.when(pl.program_id(1) < nct)
    def _():
        add(0, cap_c)

    @pl.when(pl.program_id(1) >= nct)
    def _():
        add(cap_c, cap)


def _scatter(z, modsel, slot_t, yc, ctx_len, cap_c):
    B, S, D = z.shape
    _, E, cap, _ = yc.shape
    tm = _token_tile(ctx_len, S - ctx_len)
    nct = ctx_len // tm
    tok = lambda n: pl.BlockSpec((1, tm, n), lambda b, i: (b, i, 0))
    return pl.pallas_call(
        functools.partial(_scatter_kernel, cap_c=cap_c, cap=cap, nct=nct),
        grid=(B, S // tm),
        in_specs=[tok(D),
                  pl.BlockSpec((1, 1, N_MOD, D), lambda b, i: (b, jnp.where(i >= nct, 1, 0), 0, 0)),
                  tok(E),
                  pl.BlockSpec((1, E, cap, D), lambda b, i: (b, 0, 0, 0))],
        out_specs=tok(D),
        out_shape=jax.ShapeDtypeStruct((B, S, D), F32),
        compiler_params=_params("parallel", "parallel"),
        name="moe_scatter",
    )(z, modsel, slot_t, yc)


def kernel(x, c, ctx, c_ctx, ada_w, ada_b, norm1_g, norm2_g, w_in, ssm_lambda_re, ssm_lambda_im, ssm_log_dt, ssm_b_re, ssm_b_im, ssm_c_re, ssm_c_im, ssm_d, ssm_glu_w, ssm_glu_b, rwkv_mu, rwkv_w0, rwkv_w2, rwkv_a0, rwkv_a2, rwkv_g2, rwkv_k_k, rwkv_k_a, rwkv_r_k, rwkv_ln_w, rwkv_ln_b, mla_q_norm, mla_kv_norm, mla_w_uq, mla_w_ukv, mla_qn_nope, mla_kn_nope, mla_qn_rope, mla_kn_rope, w_branch, w_out, router_w, moe_w1, moe_w3, moe_w2):
    B, T, D = x.shape
    CTX = ctx.shape[1]
    S = CTX + T
    depth = ada_w.shape[0]
    cap_c = EC_CAPACITY * CTX // N_EXPERTS
    cap_l = EC_CAPACITY * T // N_EXPERTS
    assert D == D_MODEL and T % GRID_W == 0 and cap_c % 8 == 0 and cap_l % 8 == 0 and B % 8 == 0

    rows = -(-(B + 1) // 8) * 8
    cc = jnp.concatenate([c, c_ctx[None, :], jnp.zeros((rows - B - 1, D), F32)], axis=0)
    mods = _mods(cc, ada_w, ada_b).reshape(depth, rows, N_MOD, D)
    mod_lat = mods[:, :B]
    mod_ctx = jnp.broadcast_to(mods[:, B:B + 1], mod_lat.shape)
    modsel = jnp.stack([mod_ctx, mod_lat], axis=2)

    tables = _rope_tables(CTX, T)
    pos = jnp.arange(S)
    same = (pos[:, None] < CTX) == (pos[None, :] < CTX)
    tri = ((pos[:, None] <= pos[None, :]) & same).astype(BF16)

    z = jnp.concatenate([ctx, x], axis=1)
    for l in range(depth):
        g1, g2 = norm1_g[l].reshape(1, D), norm2_g[l].reshape(1, D)
        wl = w_in[l]
        w_ssm = wl[:, :RWKV_OFF].astype(BF16)
        w_rwkv = wl[:, RWKV_OFF:MLA_OFF].astype(BF16)
        w_mla = jnp.pad(wl[:, MLA_OFF:GATE_OFF], ((0, 0), (0, MLA_IN_PAD - MLA_IN))).astype(BF16)
        w_gate = wl[:, GATE_OFF:].astype(BF16)
        p_ssm, p_rwkv, p_mla = _input_proj(z, g1, modsel[l], w_ssm, w_rwkv, w_mla, CTX)

        prep = _ssm_prep(ssm_lambda_re[l], ssm_lambda_im[l], ssm_log_dt[l], ssm_b_re[l], ssm_b_im[l],
                         ssm_c_re[l], ssm_c_im[l])
        y_ssm = _ssm_scan(p_ssm, prep, ssm_d[l], B, CTX)

        rp = _rwkv_params(rwkv_mu[l], rwkv_w0[l], rwkv_w2[l], rwkv_a0[l], rwkv_a2[l], rwkv_g2[l],
                          rwkv_k_k[l], rwkv_k_a[l], rwkv_r_k[l], rwkv_ln_w[l], rwkv_ln_b[l])
        y_fwd = _rwkv_dir(p_rwkv, None, rp, 0, CTX)
        y_rwkv = _rwkv_dir(p_rwkv, y_fwd, rp, 1, CTX)

        mp = _mla_params(mla_q_norm[l], mla_kv_norm[l], mla_w_uq[l], mla_w_ukv[l], mla_qn_nope[l],
                         mla_kn_nope[l], mla_qn_rope[l], mla_kn_rope[l])
        q, k, v = _mla_prep(p_mla, tables, mp, CTX)
        y_mla = _attention(q, k, v, CTX)

        z, h2, logits_t = _merge(z, g1, g2, modsel[l], y_ssm, y_rwkv, y_mla, w_gate,
                                 w_branch[l].astype(BF16), w_out[l].astype(BF16),
                                 ssm_glu_w[l].astype(BF16), ssm_glu_b[l].reshape(1, -1),
                                 router_w[l].T, CTX)
        slot, gate = _route(logits_t, tri, CTX, cap_c, cap_l)
        yc = _experts(h2, slot, gate, moe_w1[l].astype(BF16), moe_w3[l].astype(BF16),
                      moe_w2[l].astype(BF16), cap_c + cap_l, cap_c, CTX)
        z = _scatter(z, modsel[l], jnp.swapaxes(slot, 1, 2), yc, CTX, cap_c)
    return z[:, CTX:]
```

```python
import functools
import math

import jax
import jax.numpy as jnp
from jax import lax
from jax.experimental import pallas as pl
from jax.experimental.pallas import tpu as pltpu

F32 = jnp.float32
BF16 = jnp.bfloat16

D_MODEL = 1024
GRID_W = 64
N_MOD = 6
NORM_EPS = 1e-6
GN_EPS = 64e-5
BRANCH_WIDTH = 512
SSM_GROUP = 16
SSM_GROUPS = BRANCH_WIDTH // SSM_GROUP
SSM_STATE = 64
SSM_CHUNK = 16
RWKV_W = BRANCH_WIDTH
RWKV_HEAD = 64
RWKV_HEADS = RWKV_W // RWKV_HEAD
RWKV_CHUNK = 64
LORA_W = 128
RWKV_IN = 3 * RWKV_W + 3 * LORA_W
MLA_HEADS = 8
MLA_NOPE = 64
MLA_ROPE = 32
MLA_V = 64
Q_LORA = 384
KV_LORA = 256
MLA_IN = Q_LORA + KV_LORA + MLA_ROPE
MLA_IN_PAD = 768
MLA_SCALE = 1.0 / math.sqrt(MLA_NOPE + MLA_ROPE)
ROPE_BASE = 10000.0
HEAD_PAD = 128
RWKV_OFF = BRANCH_WIDTH
MLA_OFF = RWKV_OFF + RWKV_IN
GATE_OFF = MLA_OFF + MLA_IN
N_EXPERTS = 16
EXPERT_FF = 1536
EC_CAPACITY = 2
VMEM_LIMIT = 56 * 1024 * 1024


def _params(*sem):
    return pltpu.CompilerParams(dimension_semantics=sem, vmem_limit_bytes=VMEM_LIMIT)


def _dot(a, b):
    return jnp.dot(a, b, preferred_element_type=F32)


def _dot_nt(a, b):
    return lax.dot_general(a, b, (((1,), (1,)), ((), ())), preferred_element_type=F32)


def _dot_tn(a, b):
    return lax.dot_general(a, b, (((0,), (0,)), ((), ())), preferred_element_type=F32)


def _split2(x):
    hi = x.astype(BF16)
    lo = (x - hi.astype(F32)).astype(BF16)
    return hi, lo


def _split3(x):
    hi = x.astype(BF16)
    r1 = x - hi.astype(F32)
    mid = r1.astype(BF16)
    lo = (r1 - mid.astype(F32)).astype(BF16)
    return hi, mid, lo


def _dot_exact_rhs(xs, m):
    rows = xs[0].shape[0]
    parts = [p for x in xs for p in _split2(x)]
    res = _dot(jnp.concatenate(parts, axis=0), m)
    return [res[(2 * i) * rows:(2 * i + 1) * rows] + res[(2 * i + 1) * rows:(2 * i + 2) * rows]
            for i in range(len(xs))]


def _dot_exact_lhs(m, x, parts):
    pieces = _split2(x) if parts == 2 else _split3(x)
    acc = _dot(m, pieces[0])
    for p in pieces[1:]:
        acc = acc + _dot(m, p)
    return acc


def _sigmoid(x):
    return 0.5 * jnp.tanh(0.5 * x) + 0.5


def _norm_mod(x, g, shift, scale):
    y = x * lax.rsqrt(jnp.mean(x * x, axis=-1, keepdims=True) + NORM_EPS)
    return (y * g) * (1.0 + scale) + shift


def _token_tile(ctx_len, seq):
    for tm in (256, 128, 64):
        if ctx_len % tm == 0 and seq % tm == 0:
            return tm
    raise ValueError("context and latent lengths must be multiples of 64")


def _mods_kernel(cc_ref, w_ref, b_ref, o_ref):
    c = cc_ref[...]
    s = (c * _sigmoid(c)).astype(BF16)
    o_ref[0] = _dot(s, w_ref[0].astype(BF16)) + b_ref[0]


def _mods(cc, ada_w, ada_b):
    L, D, N = ada_w.shape
    R = cc.shape[0]
    tn = 1536
    return pl.pallas_call(
        _mods_kernel,
        grid=(L, N // tn),
        in_specs=[pl.BlockSpec((R, D), lambda l, j: (0, 0)),
                  pl.BlockSpec((1, D, tn), lambda l, j: (l, 0, j)),
                  pl.BlockSpec((1, 1, tn), lambda l, j: (l, 0, j))],
        out_specs=pl.BlockSpec((1, R, tn), lambda l, j: (l, 0, j)),
        out_shape=jax.ShapeDtypeStruct((L, R, N), F32),
        compiler_params=_params("parallel", "parallel"),
        name="adaln_mods",
    )(cc, ada_w, ada_b.reshape(L, 1, N))


def _kin_kernel(z_ref, g_ref, mod_ref, ws_ref, wr_ref, wm_ref, os_ref, or_ref, om_ref):
    m = mod_ref[0, 0]
    h = _norm_mod(z_ref[0], g_ref[...], m[0:1], m[1:2]).astype(BF16)
    os_ref[...] = _dot(h, ws_ref[...])
    or_ref[0] = _dot(h, wr_ref[...])
    om_ref[0] = _dot(h, wm_ref[...])


def _input_proj(z, g, modsel, ws, wr, wm, ctx_len):
    B, S, D = z.shape
    tm = _token_tile(ctx_len, S - ctx_len)
    nct = ctx_len // tm
    tok = lambda n: pl.BlockSpec((1, tm, n), lambda b, i: (b, i, 0))
    full = lambda a: pl.BlockSpec(a.shape, lambda b, i: (0,) * a.ndim)
    return pl.pallas_call(
        _kin_kernel,
        grid=(B, S // tm),
        in_specs=[tok(D), full(g),
                  pl.BlockSpec((1, 1, N_MOD, D), lambda b, i: (b, jnp.where(i >= nct, 1, 0), 0, 0)),
                  full(ws), full(wr), full(wm)],
        out_specs=[pl.BlockSpec((tm, ws.shape[1]), lambda b, i: (i, b)), tok(wr.shape[1]), tok(wm.shape[1])],
        out_shape=[jax.ShapeDtypeStruct((S, B * ws.shape[1]), F32),
                   jax.ShapeDtypeStruct((B, S, wr.shape[1]), F32),
                   jax.ShapeDtypeStruct((B, S, wm.shape[1]), F32)],
        compiler_params=_params("parallel", "parallel"),
        name="input_proj",
    )(z, g, modsel, ws, wr, wm)


def _cpow(ar, ai, lag, shape, nbits=5):
    pr = jnp.ones(shape, F32)
    pi = jnp.zeros(shape, F32)
    for b in range(nbits):
        bit = ((lag >> b) & 1) == 1
        fr = jnp.where(bit, ar, 1.0)
        fi = jnp.where(bit, ai, 0.0)
        pr, pi = pr * fr - pi * fi, pr * fi + pi * fr
        ar, ai = ar * ar - ai * ai, 2.0 * ar * ai
    return pr, pi


def _ssm_prep_kernel(lc_re_ref, lc_im_ref, lr_re_ref, lr_im_ref, ldt_ref, bt_re_ref, bt_im_ref,
                     ct_re_ref, ct_im_ref, wt_ref, wso_ref, wsi_ref, a_ref):
    C, GC, P = SSM_CHUNK, SSM_GROUP, SSM_STATE
    W = C * GC
    lane = lax.broadcasted_iota(jnp.int32, (1, W), 1)
    quarter = lane // P
    is_re = (quarter == 0) | (quarter == 3)
    jcol = lane // GC
    srow = lax.broadcasted_iota(jnp.int32, (W, 1), 0) // GC
    for d in (0, 1):
        dt = jnp.exp(ldt_ref[d, 0])
        lr, li = lc_re_ref[d, 0], lc_im_ref[d, 0]
        mag = jnp.exp(lr * dt)
        ar, ai = mag * jnp.cos(li * dt), mag * jnp.sin(li * dt)
        cr, ci = ct_re_ref[d, 0], ct_im_ref[d, 0]
        lag_z = jcol if d == 0 else (C - 1) - jcol
        lag_s = jcol + 1 if d == 0 else C - jcol

        def q_of(lag):
            pr, pi = _cpow(ar, ai, lag, (P, W))
            q_re = cr * pr - ci * pi
            q_im = -(cr * pi + ci * pr)
            return q_re, q_im

        qz_re, qz_im = q_of(lag_z)
        qs_re, qs_im = q_of(lag_s)
        lr4, li4 = lr_re_ref[d, 0], lr_im_ref[d, 0]
        mag4 = jnp.exp(lr4 * dt)
        ar4, ai4 = mag4 * jnp.cos(li4 * dt), mag4 * jnp.sin(li4 * dt)
        den = lr4 * lr4 + li4 * li4
        nr, ni = ar4 - 1.0, ai4
        coef_re = (nr * lr4 + ni * li4) / den
        coef_im = (ni * lr4 - nr * li4) / den
        br, bi = bt_re_ref[0], bt_im_ref[0]
        bb_re = coef_re * br - coef_im * bi
        bb_im = coef_re * bi + coef_im * br
        bcat = jnp.where(lane < P, bb_re, bb_im)[:, :2 * P]
        qz = jnp.concatenate([qz_re, qz_im], axis=0)
        z = jnp.dot(bcat, qz, preferred_element_type=F32,
                    precision=lax.Precision.HIGHEST)
        rows = []
        for s in range(C):
            if d == 0:
                sh = pltpu.roll(z, GC * s, axis=1) if s else z
                rows.append(jnp.where(lane >= GC * s, sh, 0.0))
            else:
                m = C - 1 - s
                sh = pltpu.roll(z, W - GC * m, axis=1) if m else z
                rows.append(jnp.where(lane < W - GC * m, sh, 0.0))
        wt_ref[d, 0] = jnp.concatenate(rows, axis=0).astype(BF16)
        wsi_ref[d, 0] = jnp.concatenate([qs_re, qs_im], axis=0).astype(BF16)
        e_row = (C - 1) - srow if d == 0 else srow
        er, ei = _cpow(ar4, ai4, e_row, (W, W), nbits=4)
        bbx = jnp.where(is_re, bb_re, bb_im)
        bby = jnp.where(is_re, -bb_im, bb_re)
        bbx = jnp.concatenate([bbx] * C, axis=0)
        bby = jnp.concatenate([bby] * C, axis=0)
        wso_ref[d, 0] = (er * bbx + ei * bby).astype(BF16)
        cr16, ci16 = _cpow(ar4, ai4, jnp.full((1, W), C, jnp.int32), (1, W))
        a_ref[d, 0, 0:1, :] = cr16
        a_ref[d, 0, 1:2, :] = jnp.where(is_re, -ci16, ci16)


def _ssm_prep(lam_re, lam_im, log_dt, b_re, b_im, c_re, c_im):
    G, P, GC, C = SSM_GROUPS, SSM_STATE, SSM_GROUP, SSM_CHUNK
    W = C * GC
    lc_re, lc_im = lam_re[..., None], lam_im[..., None]
    lr_re = jnp.tile(lam_re, (1, 1, 4))[:, :, None, :]
    lr_im = jnp.tile(lam_im, (1, 1, 4))[:, :, None, :]
    ldt = log_dt[..., None, None]
    bt_re = jnp.tile(jnp.swapaxes(b_re, 1, 2), (1, 1, 4))
    bt_im = jnp.tile(jnp.swapaxes(b_im, 1, 2), (1, 1, 4))
    ct_re = jnp.tile(jnp.swapaxes(c_re, 2, 3), (1, 1, 1, C))
    ct_im = jnp.tile(jnp.swapaxes(c_im, 2, 3), (1, 1, 1, C))
    d4 = lambda a, b: pl.BlockSpec((2, 1, a, b), lambda g: (0, g, 0, 0))
    mat = jax.ShapeDtypeStruct((2, G, W, W), BF16)
    return pl.pallas_call(
        _ssm_prep_kernel,
        grid=(G,),
        in_specs=[d4(P, 1), d4(P, 1), d4(1, W), d4(1, W), d4(1, 1),
                  pl.BlockSpec((1, GC, W), lambda g: (g, 0, 0)),
                  pl.BlockSpec((1, GC, W), lambda g: (g, 0, 0)),
                  d4(P, W), d4(P, W)],
        out_specs=[d4(W, W), d4(W, W), d4(2 * P, W), d4(2, W)],
        out_shape=[mat, mat, jax.ShapeDtypeStruct((2, G, 2 * P, W), BF16),
                   jax.ShapeDtypeStruct((2, G, 2, W), F32)],
        compiler_params=_params("parallel"),
        name="ssm_prep",
    )(lc_re, lc_im, lr_re, lr_im, ldt, bt_re, bt_im, ct_re, ct_im)


def _ssm_kernel(*refs, d, final, KT):
    if final:
        x_ref, yp_ref, wt_ref, wso_ref, wsi_ref, a_ref, dn_ref, o_ref, st_ref = refs
    else:
        x_ref, wt_ref, wso_ref, wsi_ref, a_ref, o_ref, st_ref = refs
    C, GC, P = SSM_CHUNK, SSM_GROUP, SSM_STATE
    BT = x_ref.shape[2]
    GB = x_ref.shape[3] // GC
    R = KT * BT

    @pl.when(pl.program_id(2) == 0)
    def _():
        st_ref[...] = jnp.zeros_like(st_ref)

    xt = [x_ref[:, t].reshape(R, GB * GC) for t in range(C)]
    gs = range(GB)
    ub = [jnp.concatenate([xt[t][:, g * GC:(g + 1) * GC] for t in range(C)], axis=1).astype(BF16) for g in gs]
    loc = [_dot(ub[g], wso_ref[0, g]) for g in gs]
    a1 = [a_ref[0, g, 0:1, :] for g in gs]
    a2 = [a_ref[0, g, 1:2, :] for g in gs]
    x = [st_ref[g] for g in gs]
    xs = [[None] * KT for _ in gs]
    for k in (range(KT) if d == 0 else range(KT - 1, -1, -1)):
        for g in gs:
            xs[g][k] = x[g][:, :2 * P]
            x[g] = a1[g] * x[g] + a2[g] * pltpu.roll(x[g], 2 * P, axis=1) + loc[g][k * BT:(k + 1) * BT]
    ys = []
    for g in gs:
        st_ref[g] = x[g]
        xin = jnp.concatenate(xs[g], axis=0).astype(BF16)
        ys.append(_dot(ub[g], wt_ref[0, g]) + _dot(xin, wsi_ref[0, g]))
    for t in range(C):
        yt = jnp.concatenate([ys[g][:, t * GC:(t + 1) * GC] for g in gs], axis=1)
        if final:
            yt = yt + yp_ref[:, t].reshape(R, GB * GC) + xt[t] * dn_ref[...]
        o_ref[:, t] = yt.reshape(KT, BT, GB * GC)


def _ssm_dir(p_tm, y_prev, prep, d_skip, d, B, ctx_len):
    S = p_tm.shape[0]
    G, GC, C, P = SSM_GROUPS, SSM_GROUP, SSM_CHUNK, SSM_STATE
    W = C * GC
    LW = 128
    GB = LW // GC
    BT = 8
    KT = _token_tile(ctx_len, S - ctx_len) // C
    NT, NTc = S // (C * KT), ctx_len // (C * KT)
    wt, wso, wsi, a4 = prep
    final = y_prev is not None

    def tile(ti):
        if d == 0:
            return ti
        return jnp.where(ti < NTc, NTc - 1 - ti, NT - 1 - (ti - NTc))

    x4 = p_tm.reshape(S // C, C, B, G * GC)
    xspec = pl.BlockSpec((KT, C, BT, LW), lambda gb, bt, ti: (tile(ti), 0, bt, gb))
    wspec = lambda r: pl.BlockSpec((1, GB, r, W), lambda gb, bt, ti: (d, gb, 0, 0))
    in_specs, args = [xspec], [x4]
    if final:
        in_specs.append(xspec)
        args.append(y_prev.reshape(x4.shape))
    in_specs += [wspec(W), wspec(W), wspec(2 * P), wspec(2)]
    args += [wt, wso, wsi, a4]
    if final:
        in_specs.append(pl.BlockSpec((1, LW), lambda gb, bt, ti: (0, gb)))
        args.append(d_skip.reshape(1, G * GC))
    y = pl.pallas_call(
        functools.partial(_ssm_kernel, d=d, final=final, KT=KT),
        grid=(G // GB, B // BT, NT),
        in_specs=in_specs,
        out_specs=xspec,
        out_shape=jax.ShapeDtypeStruct(x4.shape, F32),
        scratch_shapes=[pltpu.VMEM((GB, BT, W), F32)],
        compiler_params=_params("parallel", "parallel", "arbitrary"),
        name="ssm_rev" if d else "ssm_fwd",
    )(*args)
    return y.reshape(p_tm.shape)


def _ssm_scan(p_tm, prep, d_skip, B, ctx_len):
    y_fwd = _ssm_dir(p_tm, None, prep, d_skip, 0, B, ctx_len)
    return _ssm_dir(p_tm, y_fwd, prep, d_skip, 1, B, ctx_len)


def _rwkv_kernel(*refs, d, final, NC, NCc):
    if final:
        (p_ref, hp_ref, hn_ref, yp_ref, mu_ref, w0_ref, w2_ref, a0_ref, a2_ref, g2_ref, pv_ref,
         o_ref, s_ref, xb_ref, yb_ref) = refs
    else:
        (p_ref, hp_ref, hn_ref, mu_ref, w0_ref, w2_ref, a0_ref, a2_ref, g2_ref, pv_ref,
         o_ref, s_ref, xb_ref, yb_ref) = refs
    L, N, H, W = RWKV_CHUNK, RWKV_HEAD, RWKV_HEADS, RWKV_W
    NB = p_ref.shape[0]
    ci = pl.program_id(1)
    if d == 0:
        c = ci
    else:
        c = jnp.where(ci < NCc, NCc - 1 - ci, NC - 1 - (ci - NCc))

    @pl.when(ci == 0)
    def _():
        s_ref[...] = jnp.zeros_like(s_ref)

    row = lax.broadcasted_iota(jnp.int32, (L, 1), 0)
    seg_first = jnp.where((c == 0) | (c == NCc), 1.0, 0.0)
    seg_last = jnp.where((c == NCc - 1) | (c == NC - 1), 1.0, 0.0)
    keep_prev = 1.0 - seg_first * jnp.where(row == 0, 1.0, 0.0)
    keep_next = 1.0 - seg_last * jnp.where(row == L - 1, 1.0, 0.0)
    xs = []
    for bi in range(NB):
        p = p_ref[bi]
        xb_ref[bi, 0:8, :] = hp_ref[bi]
        xb_ref[bi, 8:8 + L, :] = p
        xb_ref[bi, 8 + L:16 + L, :] = hn_ref[bi]
        prev = xb_ref[bi, 7:7 + L, :] * keep_prev
        nxt = xb_ref[bi, 9:9 + L, :] * keep_next
        xs.append(p + mu_ref[...] * (0.5 * (prev + nxt) - p))
    x = jnp.concatenate(xs, axis=0)

    r, k, v = x[:, 0:W], x[:, W:2 * W], x[:, 2 * W:3 * W]
    pw = x[:, 3 * W:3 * W + LORA_W]
    pa = x[:, 3 * W + LORA_W:3 * W + 2 * LORA_W]
    pg = x[:, 3 * W + 2 * LORA_W:3 * W + 3 * LORA_W]
    k_k, k_a, r_k = pv_ref[0:1, :], pv_ref[1:2, :], pv_ref[2:3, :]
    ln_w, ln_b = pv_ref[3:4, :], pv_ref[4:5, :]
    pab = pa.astype(BF16)

    zw = w0_ref[d:d + 1, :] + _dot(jnp.tanh(pw).astype(BF16), w2_ref[d])
    nz = -zw
    softplus = jnp.maximum(nz, 0.0) + jnp.log(1.0 + jnp.exp(-jnp.abs(nz)))
    lw = -jnp.exp(-softplus - 0.5)
    a = _sigmoid(a0_ref[d:d + 1, :] + _dot(pab, a2_ref[d]))
    kd = k * (1.0 + (a - 1.0) * k_a)

    hrow = lax.broadcasted_iota(jnp.int32, (W, W), 0) // N
    hcol = lax.broadcasted_iota(jnp.int32, (W, W), 1) // N
    head_ones = jnp.where(hrow == hcol, 1.0, 0.0).astype(BF16)

    kk = k * k_k
    kk = kk * lax.rsqrt(_dot_exact_rhs([kk * kk], head_ones)[0] + 1e-12)
    b = a * kk

    PW2 = 2 * N
    trow = lax.broadcasted_iota(jnp.int32, (L, PW2), 0)
    lane2 = lax.broadcasted_iota(jnp.int32, (L, PW2), 1)
    left = lane2 < N
    tcol = jnp.where(left, lane2, lane2 - N)
    if d == 0:
        strict, incl = tcol < trow, tcol <= trow
    else:
        strict, incl = tcol > trow, tcol >= trow

    def bdiag(t):
        return jnp.concatenate([jnp.where(left, t, 0.0), jnp.where(left, 0.0, t)], axis=0).astype(BF16)

    brow = lax.broadcasted_iota(jnp.int32, (NB * L, NB * L), 0)
    bcol = lax.broadcasted_iota(jnp.int32, (NB * L, NB * L), 1)
    upto = (bcol <= brow) if d == 0 else (bcol >= brow)
    cum = jnp.where(upto, jnp.where((brow // L) == (bcol // L), 1.0, 0.0), 0.0).astype(BF16)
    cs = _dot_exact_lhs(cum, lw, 3)
    last = L - 1 if d == 0 else 0
    cls = [cs[bi * L + last:bi * L + last + 1, :] for bi in range(NB)]
    cl = jnp.concatenate([jnp.broadcast_to(t, (L, W)) for t in cls], axis=0)
    e_to_end = jnp.exp(cl - cs)
    e_neg = jnp.exp(-cs)
    rt = (r * jnp.exp(cs)).astype(BF16)
    at = (kk * jnp.exp(cs - lw)).astype(BF16)
    bt = b * e_neg
    kt = kd * e_neg
    kh = (kd * e_to_end).astype(BF16)
    bh = (b * e_to_end).astype(BF16)
    e_chunk = [jnp.exp(t) for t in cls]
    vb = v.astype(BF16)

    ids = [(bi, j) for bi in range(NB) for j in range(H // 2)]
    n = range(len(ids))
    rs = [slice(bi * L, (bi + 1) * L) for bi, _ in ids]
    ls = [slice(j * PW2, (j + 1) * PW2) for _, j in ids]
    ar = [jnp.concatenate([at[rs[i], ls[i]], rt[rs[i], ls[i]]], axis=0) for i in n]
    s0 = [s_ref[bi, j] for bi, j in ids]
    g_b = [_dot_nt(ar[i], bdiag(bt[rs[i], ls[i]])) for i in n]
    g_k = [_dot_nt(ar[i], bdiag(kt[rs[i], ls[i]])) for i in n]
    g_s = [_dot_nt(ar[i], bdiag(s0[i])) for i in n]
    nab = [jnp.where(strict, g_b[i][:L], 0.0) for i in n]
    mrb = [jnp.where(incl, g_b[i][L:], 0.0).astype(BF16) for i in n]
    nm = [jnp.concatenate([jnp.where(strict, g_k[i][:L], 0.0), jnp.where(incl, g_k[i][L:], 0.0)],
                          axis=0).astype(BF16) for i in n]
    nv = [g_s[i] + _dot(nm[i], bdiag(v[rs[i], ls[i]])) for i in n]
    pm = [-t for t in nab]
    q = [_dot(t.astype(BF16), bdiag(t)) for t in nab]
    steps = int(math.log2(L)) - 1
    for it in range(steps):
        qd = [bdiag(t) for t in q]
        if it + 1 < steps:
            pq = [_dot(jnp.concatenate([pm[i], q[i]], axis=0).astype(BF16), qd[i]) for i in n]
            pm = [pm[i] + q[i] + pq[i][:L] for i in n]
            q = [t[L:] for t in pq]
        else:
            pm = [pm[i] + q[i] + _dot(pm[i].astype(BF16), qd[i]) for i in n]
    u = [nv[i][:L] + _dot(pm[i].astype(BF16), bdiag(nv[i][:L])) for i in n]
    for i in n:
        yb_ref[rs[i], ls[i]] = nv[i][L:] - _dot(mrb[i], bdiag(u[i]))
    for i, (bi, j) in enumerate(ids):
        vu = jnp.concatenate([vb[rs[i], ls[i]], -u[i].astype(BF16)], axis=0)
        kb = jnp.concatenate([kh[rs[i], ls[i]], bh[rs[i], ls[i]]], axis=0)
        full = _dot_tn(vu, kb)
        s_ref[bi, j] = s0[i] * e_chunk[bi][:, ls[i]] + jnp.where(left, full[:L], full[L:])

    if not final:
        o_ref[...] = yb_ref[...].reshape(NB, L, W)
        return
    y = yp_ref[...].reshape(NB * L, W) + yb_ref[...]
    o = 1 - d
    a_o = _sigmoid(a0_ref[o:o + 1, :] + _dot(pab, a2_ref[o]))
    kd_sum = kd + k * (1.0 + (a_o - 1.0) * k_a)
    y_sum, rk_sum = _dot_exact_rhs([y, r * kd_sum * r_k], head_ones)
    dev = y - y_sum * (1.0 / N)
    var = _dot_exact_rhs([dev * dev], head_ones)[0] * (1.0 / N)
    yn = dev * lax.rsqrt(var + GN_EPS) * ln_w + ln_b
    bonus = rk_sum * v
    g = _dot(_sigmoid(pg).astype(BF16), g2_ref[...])
    o_ref[...] = ((yn + bonus) * g).reshape(NB, L, W)


def _rwkv_dir(p_rwkv, y_prev, params, d, ctx_len):
    B, S, PW = p_rwkv.shape
    L, W = RWKV_CHUNK, RWKV_W
    NC, NCc = S // L, ctx_len // L
    NB = 2
    final = y_prev is not None

    def chunk(ci):
        if d == 0:
            return ci
        return jnp.where(ci < NCc, NCc - 1 - ci, NC - 1 - (ci - NCc))

    hb = L // 8
    tok = lambda n: pl.BlockSpec((NB, L, n), lambda b, ci: (b, chunk(ci), 0))
    full = lambda a: pl.BlockSpec(a.shape, lambda b, ci: (0,) * a.ndim)
    in_specs = [tok(PW),
                pl.BlockSpec((NB, 8, PW), lambda b, ci: (b, jnp.maximum(chunk(ci) * hb - 1, 0), 0)),
                pl.BlockSpec((NB, 8, PW), lambda b, ci: (b, jnp.minimum((chunk(ci) + 1) * hb, S // 8 - 1), 0))]
    args = [p_rwkv, p_rwkv, p_rwkv]
    if final:
        in_specs.append(tok(W))
        args.append(y_prev)
    in_specs += [full(a) for a in params]
    args += list(params)
    return pl.pallas_call(
        functools.partial(_rwkv_kernel, d=d, final=final, NC=NC, NCc=NCc),
        grid=(B // NB, NC),
        in_specs=in_specs,
        out_specs=tok(W),
        out_shape=jax.ShapeDtypeStruct((B, S, W), F32),
        scratch_shapes=[pltpu.VMEM((NB, RWKV_HEADS // 2, RWKV_HEAD, 2 * RWKV_HEAD), F32),
                        pltpu.VMEM((NB, L + 16, PW), F32),
                        pltpu.VMEM((NB * L, W), F32)],
        compiler_params=_params("parallel", "arbitrary"),
        name="rwkv_rev" if d else "rwkv_fwd",
    )(*args)


def _rwkv_params(mu, w0, w2, a0, a2, g2, k_k, k_a, r_k, ln_w, ln_b):
    W = RWKV_W
    half = LORA_W // 2

    def pad_dir(w):
        out = jnp.zeros((2, LORA_W, W), F32)
        out = out.at[0, :half].set(w[0]).at[1, half:].set(w[1])
        return out.astype(BF16)

    pv = jnp.zeros((8, W), F32)
    pv = pv.at[0].set(k_k).at[1].set(k_a).at[2].set(r_k.reshape(W)).at[3].set(ln_w).at[4].set(ln_b)
    return (mu.reshape(1, RWKV_IN), w0, pad_dir(w2), a0, pad_dir(a2), g2.astype(BF16), pv)


def _mla_prep_kernel(p_ref, cq_ref, sq_ref, ck_ref, sk_ref, qn_ref, kvn_ref, wq_ref, wk_ref, wv_ref,
                     gn_ref, q_ref, k_ref, v_ref):
    HP = HEAD_PAD
    p = p_ref[0]
    lane = lax.broadcasted_iota(jnp.int32, (1, HP), 1)
    m_nope = jnp.where(lane < MLA_NOPE, 1.0, 0.0)
    m_rope = jnp.where(lane < MLA_NOPE, 0.0, jnp.where(lane < MLA_NOPE + MLA_ROPE, 1.0, 0.0))
    g_q, g_kn, g_kr = gn_ref[0:1, :], gn_ref[1:2, :], gn_ref[2:3, :]

    def rms(x, n):
        return lax.rsqrt(jnp.sum(x * x, axis=-1, keepdims=True) * (1.0 / n) + NORM_EPS)

    xq = p[:, :Q_LORA]
    cq = (xq * rms(xq, Q_LORA) * qn_ref[...]).astype(BF16)
    xkv = p[:, Q_LORA:Q_LORA + KV_LORA]
    ckv = (xkv * rms(xkv, KV_LORA) * kvn_ref[...]).astype(BF16)
    q = _dot(cq, wq_ref[...])
    kn = _dot(ckv, wk_ref[...])
    v_ref[0] = _dot_nt(wv_ref[...], ckv).astype(BF16)

    kr = p[:, Q_LORA + KV_LORA:Q_LORA + KV_LORA + HP]
    krn = kr * rms(kr, MLA_ROPE) * g_kr
    to_rope = pltpu.roll(krn, MLA_NOPE, axis=1)
    sw_a = jnp.where(lane >= 112, 0.0, jnp.where(lane >= 96, pltpu.roll(krn, 80, axis=1), 0.0))
    sw_b = jnp.where(lane >= 112, pltpu.roll(krn, 112, axis=1), 0.0)
    kext = to_rope + sw_a + sw_b
    krot = kext * ck_ref[...] + pltpu.roll(kext, HP - MLA_ROPE, axis=1) * sk_ref[...]

    cq_t, sq_t = cq_ref[...], sq_ref[...]
    for h in range(MLA_HEADS):
        sl = slice(h * HP, (h + 1) * HP)
        qh = q[:, sl]
        scale = m_nope * rms(qh * m_nope, MLA_NOPE) + (1.0 - m_nope) * rms(qh * m_rope, MLA_ROPE)
        qn = qh * scale * g_q
        q_ref[0, :, sl] = (qn * cq_t + pltpu.roll(qn, HP - MLA_ROPE, axis=1) * sq_t).astype(BF16)
        kh = kn[:, sl]
        k_ref[0, :, sl] = (kh * rms(kh, MLA_NOPE) * g_kn + krot).astype(BF16)


def _mla_prep(p_mla, tables, params, ctx_len):
    B, S, PW = p_mla.shape
    tm = _token_tile(ctx_len, S - ctx_len)
    cq_t, sq_t, ck_t, sk_t = tables
    HW = MLA_HEADS * HEAD_PAD
    tok = lambda n: pl.BlockSpec((1, tm, n), lambda b, i: (b, i, 0))
    tab = pl.BlockSpec((tm, HEAD_PAD), lambda b, i: (i, 0))
    full = lambda a: pl.BlockSpec(a.shape, lambda b, i: (0,) * a.ndim)
    return pl.pallas_call(
        _mla_prep_kernel,
        grid=(B, S // tm),
        in_specs=[tok(PW), tab, tab, tab, tab] + [full(a) for a in params],
        out_specs=[tok(HW), tok(HW), pl.BlockSpec((1, MLA_HEADS * MLA_V, tm), lambda b, i: (b, 0, i))],
        out_shape=[jax.ShapeDtypeStruct((B, S, HW), BF16), jax.ShapeDtypeStruct((B, S, HW), BF16),
                   jax.ShapeDtypeStruct((B, MLA_HEADS * MLA_V, S), BF16)],
        compiler_params=_params("parallel", "parallel"),
        name="mla_prep",
    )(p_mla, cq_t, sq_t, ck_t, sk_t, *params)


def _mla_params(q_norm, kv_norm, w_uq, w_ukv, qn_nope, kn_nope, qn_rope, kn_rope):
    H, NP, RP, HP = MLA_HEADS, MLA_NOPE, MLA_ROPE, HEAD_PAD
    half = RP // 2
    swap = jnp.concatenate([jnp.arange(half, RP), jnp.arange(0, half)])
    wq = w_uq.reshape(Q_LORA, H, NP + RP)
    wq = jnp.concatenate([wq, wq[:, :, NP + swap]], axis=-1).reshape(Q_LORA, H * HP)
    wkv = w_ukv.reshape(KV_LORA, H, NP + MLA_V)
    wk = jnp.concatenate([wkv[:, :, :NP], jnp.zeros((KV_LORA, H, HP - NP), F32)], axis=-1)
    wk = wk.reshape(KV_LORA, H * HP)
    wv = wkv[:, :, NP:].reshape(KV_LORA, H * MLA_V)
    gn = jnp.zeros((8, HP), F32)
    gn = gn.at[0].set(jnp.concatenate([qn_nope, qn_rope, qn_rope[swap]]))
    gn = gn.at[1, :NP].set(kn_nope).at[2, :RP].set(kn_rope)
    return (q_norm.reshape(1, Q_LORA), kv_norm.reshape(1, KV_LORA),
            wq.astype(BF16), wk.astype(BF16), wv.T.astype(BF16), gn)


def _rope_tables(ctx_len, seq):
    rows = seq // GRID_W
    axis_dims = MLA_ROPE // 2
    row = jnp.repeat(jnp.arange(rows), GRID_W).astype(F32)
    col = jnp.tile(jnp.arange(GRID_W), rows).astype(F32)
    inv = ROPE_BASE ** (-jnp.arange(0, axis_dims, 2, dtype=F32) / axis_dims)
    ang = jnp.concatenate([row[:, None] * inv, col[:, None] * inv], axis=-1)
    cos = jnp.concatenate([jnp.ones((ctx_len, axis_dims), F32), jnp.cos(ang)], axis=0)
    sin = jnp.concatenate([jnp.zeros((ctx_len, axis_dims), F32), jnp.sin(ang)], axis=0)
    S = ctx_len + seq
    pad = jnp.zeros((S, HEAD_PAD - MLA_NOPE - MLA_ROPE), F32)
    cos_t = jnp.concatenate([jnp.ones((S, MLA_NOPE), F32), cos, cos, pad], axis=-1)
    sin_t = jnp.concatenate([jnp.zeros((S, MLA_NOPE), F32), -sin, sin, pad], axis=-1)
    qs = MLA_SCALE * math.log2(math.e)
    return cos_t * qs, sin_t * qs, cos_t, sin_t


def _attn_kernel(q_ref, k_ref, vt_ref, o_ref, *, nct, ctx_len):
    S = k_ref.shape[1]
    i = pl.program_id(2)

    def attend(nk):
        hs = range(q_ref.shape[2] // HEAD_PAD)
        s = [_dot_nt(k_ref[0, :nk, h * HEAD_PAD:(h + 1) * HEAD_PAD],
                     q_ref[0, :, h * HEAD_PAD:(h + 1) * HEAD_PAD]) for h in hs]
        m = [jnp.max(t, axis=0, keepdims=True) for t in s]
        e = [jnp.exp2(s[h] - m[h]) for h in hs]
        l = [jnp.sum(t, axis=0, keepdims=True) for t in e]
        o = [_dot(vt_ref[0, h * MLA_V:(h + 1) * MLA_V, :nk], e[h].astype(BF16)) / l[h] for h in hs]
        for j in range(len(hs) // 2):
            pair = jnp.concatenate([o[2 * j], o[2 * j + 1]], axis=0)
            o_ref[0, :, j * 2 * MLA_V:(j + 1) * 2 * MLA_V] = pair.T.astype(BF16)

    @pl.when(i < nct)
    def _():
        attend(ctx_len)

    @pl.when(i >= nct)
    def _():
        attend(S)


def _attention(q, k, v, ctx_len):
    B, S, _ = q.shape
    tq = _token_tile(ctx_len, S - ctx_len)
    HS = 4
    return pl.pallas_call(
        functools.partial(_attn_kernel, nct=ctx_len // tq, ctx_len=ctx_len),
        grid=(B, MLA_HEADS // HS, S // tq),
        in_specs=[pl.BlockSpec((1, tq, HS * HEAD_PAD), lambda b, h, i: (b, i, h)),
                  pl.BlockSpec((1, S, HS * HEAD_PAD), lambda b, h, i: (b, 0, h)),
                  pl.BlockSpec((1, HS * MLA_V, S), lambda b, h, i: (b, h, 0))],
        out_specs=pl.BlockSpec((1, tq, HS * MLA_V), lambda b, h, i: (b, i, h)),
        out_shape=jax.ShapeDtypeStruct((B, S, MLA_HEADS * MLA_V), BF16),
        compiler_params=_params("parallel", "parallel", "parallel"),
        name="mla_attention",
    )(q, k, v)


def _merge_kernel(z_ref, g1_ref, g2_ref, mod_ref, ys_ref, yr_ref, ym_ref, wg_ref, wb_ref, wo_ref,
                  glw_ref, glb_ref, rw_ref, zo_ref, h2_ref, lg_ref):
    D = D_MODEL
    z = z_ref[0]
    m = mod_ref[0, 0]
    h = _norm_mod(z, g1_ref[...], m[0:1], m[1:2]).astype(BF16)
    ys = ys_ref[...]
    ys = 0.5 * ys * (1.0 + jnp.tanh(math.sqrt(2.0 / math.pi) * (ys + 0.044715 * ys * ys * ys)))
    ys = ys * _sigmoid(_dot(ys.astype(BF16), glw_ref[...]) + glb_ref[...])
    branches = (ys.astype(BF16), yr_ref[0].astype(BF16), ym_ref[0])
    acc = jnp.zeros(z.shape, F32)
    for j, yj in enumerate(branches):
        gate = _sigmoid(_dot(h, wg_ref[:, j * D:(j + 1) * D]))
        acc = acc + gate * _dot(yj, wb_ref[j])
    zn = z + m[2:3] * _dot(acc.astype(BF16), wo_ref[...])
    zo_ref[0] = zn
    h2 = _norm_mod(zn, g2_ref[...], m[3:4], m[4:5])
    h2_ref[0] = h2.astype(BF16)
    h2p = _split3(h2)
    rw = _split3(rw_ref[...])
    lg = None
    for ia, ha in enumerate(h2p):
        for ib, rb in enumerate(rw):
            if ia + ib <= 2:
                t = _dot_nt(rb, ha)
                lg = t if lg is None else lg + t
    lg_ref[0] = lg


def _merge(z, g1, g2, modsel, ys, yr, ym, wg, wb, wo, glw, glb, rwt, ctx_len):
    B, S, D = z.shape
    tm = _token_tile(ctx_len, S - ctx_len)
    nct = ctx_len // tm
    E = rwt.shape[0]
    tok = lambda n: pl.BlockSpec((1, tm, n), lambda b, i: (b, i, 0))
    full = lambda a: pl.BlockSpec(a.shape, lambda b, i: (0,) * a.ndim)
    W = BRANCH_WIDTH
    return pl.pallas_call(
        _merge_kernel,
        grid=(B, S // tm),
        in_specs=[tok(D), full(g1), full(g2),
                  pl.BlockSpec((1, 1, N_MOD, D), lambda b, i: (b, jnp.where(i >= nct, 1, 0), 0, 0)),
                  pl.BlockSpec((tm, W), lambda b, i: (i, b)),
                  tok(W), tok(W), full(wg), full(wb), full(wo), full(glw), full(glb), full(rwt)],
        out_specs=[tok(D), tok(D), pl.BlockSpec((1, E, tm), lambda b, i: (b, 0, i))],
        out_shape=[jax.ShapeDtypeStruct((B, S, D), F32), jax.ShapeDtypeStruct((B, S, D), BF16),
                   jax.ShapeDtypeStruct((B, E, S), F32)],
        compiler_params=_params("parallel", "parallel"),
        name="merge",
    )(z, g1, g2, modsel, ys, yr, ym, wg, wb, wo, glw, glb, rwt)


def _route_kernel(lg_ref, tri_ref, slot_ref, gate_ref, *, ctx_len, cap_c, cap_l):
    lg = lg_ref[0]
    E, S = lg.shape
    e = jnp.exp(lg - jnp.max(lg, axis=0, keepdims=True))
    aff = e / jnp.sum(e, axis=0, keepdims=True)
    bits = pltpu.bitcast(aff, jnp.int32)
    lane = lax.broadcasted_iota(jnp.int32, (1, S), 1)
    tri = tri_ref[...]
    slot = jnp.full((E, S), -1.0, F32)
    sel_all = jnp.zeros((E, S), F32)
    for in_set, cap, off in ((lane < ctx_len, cap_c, 0), (lane >= ctx_len, cap_l, cap_c)):
        ms = jnp.where(in_set, 1.0, 0.0)

        def body(_, carry, ms=ms, cap=cap):
            lo, hi = carry
            mid = lo + ((hi - lo + 1) >> 1)
            cnt = jnp.sum(jnp.where(bits >= mid, ms, 0.0), axis=1, keepdims=True)
            ok = cnt >= cap
            return jnp.where(ok, mid, lo), jnp.where(ok, hi, mid - 1)

        lo, _ = lax.fori_loop(0, 31, body, (jnp.zeros((E, 1), jnp.int32),
                                            jnp.full((E, 1), 0x7F800000, jnp.int32)))
        gt = jnp.where(bits > lo, ms, 0.0)
        eq = jnp.where(bits == lo, ms, 0.0)
        need = cap - jnp.sum(gt, axis=1, keepdims=True)
        eq_rank = _dot(eq.astype(BF16), tri) - eq
        sel = gt + eq * jnp.where(eq_rank < need, 1.0, 0.0)
        rank = _dot(sel.astype(BF16), tri) - sel
        slot = jnp.where(sel > 0.0, rank + off, slot)
        sel_all = sel_all + sel
    slot_ref[0] = slot
    gate_ref[0] = aff * sel_all


def _route(logits_t, tri, ctx_len, cap_c, cap_l):
    B, E, S = logits_t.shape
    spec = pl.BlockSpec((1, E, S), lambda b: (b, 0, 0))
    return pl.pallas_call(
        functools.partial(_route_kernel, ctx_len=ctx_len, cap_c=cap_c, cap_l=cap_l),
        grid=(B,),
        in_specs=[spec, pl.BlockSpec((S, S), lambda b: (0, 0))],
        out_specs=[spec, spec],
        out_shape=[jax.ShapeDtypeStruct((B, E, S), F32)] * 2,
        compiler_params=_params("parallel"),
        name="route",
    )(logits_t, tri)


def _expert_kernel(h_ref, slot_ref, gate_ref, w1_ref, w3_ref, w2_ref, o_ref, *, cap, cap_c, ctx_len):
    S = h_ref.shape[1]
    xs, gc = [], []
    for (c0, c1), (t0, t1) in (((0, cap_c), (0, ctx_len)), ((cap_c, cap), (ctx_len, S))):
        slot = slot_ref[0, 0, :, t0:t1]
        gate = gate_ref[0, 0, :, t0:t1]
        cidx = lax.broadcasted_iota(jnp.int32, (c1 - c0, t1 - t0), 0).astype(F32) + float(c0)
        hit = slot == cidx
        onehot = jnp.where(hit, 1.0, 0.0).astype(BF16)
        xs.append(_dot(onehot, h_ref[0, t0:t1, :]))
        gc.append(jnp.sum(jnp.where(hit, gate, 0.0), axis=1, keepdims=True))
    xs = jnp.concatenate(xs, axis=0).astype(BF16)
    gc = jnp.concatenate(gc, axis=0)
    a1 = _dot(xs, w1_ref[0])
    a3 = _dot(xs, w3_ref[0])
    hid = (a1 * _sigmoid(a1) * a3).astype(BF16)
    o_ref[0, 0] = (_dot(hid, w2_ref[0]) * gc).astype(BF16)


def _experts(h2, slot, gate, w1, w3, w2, cap, cap_c, ctx_len):
    B, S, D = h2.shape
    E, _, F = w1.shape
    row = pl.BlockSpec((1, 1, 1, S), lambda e, b: (b, e, 0, 0))
    return pl.pallas_call(
        functools.partial(_expert_kernel, cap=cap, cap_c=cap_c, ctx_len=ctx_len),
        grid=(E, B),
        in_specs=[pl.BlockSpec((1, S, D), lambda e, b: (b, 0, 0)), row, row,
                  pl.BlockSpec((1, D, F), lambda e, b: (e, 0, 0)),
                  pl.BlockSpec((1, D, F), lambda e, b: (e, 0, 0)),
                  pl.BlockSpec((1, F, D), lambda e, b: (e, 0, 0))],
        out_specs=pl.BlockSpec((1, 1, cap, D), lambda e, b: (b, e, 0, 0)),
        out_shape=jax.ShapeDtypeStruct((B, E, cap, D), BF16),
        compiler_params=_params("parallel", "parallel"),
        name="experts",
    )(h2, slot.reshape(B, E, 1, S), gate.reshape(B, E, 1, S), w1, w3, w2)


def _scatter_kernel(z_ref, mod_ref, st_ref, y_ref, o_ref, *, cap_c, cap, nct):
    st = st_ref[0]
    tm, E = st.shape

    def add(lo, hi):
        cidx = lax.broadcasted_iota(jnp.int32, (tm, hi - lo), 1).astype(F32) + float(lo)
        acc = jnp.zeros(z_ref.shape[1:], F32)
        for e in range(E):
            onehot = jnp.where(st[:, e:e + 1] == cidx, 1.0, 0.0).astype(BF16)
            acc = acc + _dot(onehot, y_ref[0, e, lo:hi, :])
        o_ref[0] = z_ref[0] + mod_ref[0, 0][5:6] * acc

    @pl.when(pl.program_id(1) < nct)
    def _():
        add(0, cap_c)

    @pl.when(pl.program_id(1) >= nct)
    def _():
        add(cap_c, cap)


def _scatter(z, modsel, slot_t, yc, ctx_len, cap_c):
    B, S, D = z.shape
    _, E, cap, _ = yc.shape
    tm = _token_tile(ctx_len, S - ctx_len)
    nct = ctx_len // tm
    tok = lambda n: pl.BlockSpec((1, tm, n), lambda b, i: (b, i, 0))
    return pl.pallas_call(
        functools.partial(_scatter_kernel, cap_c=cap_c, cap=cap, nct=nct),
        grid=(B, S // tm),
        in_specs=[tok(D),
                  pl.BlockSpec((1, 1, N_MOD, D), lambda b, i: (b, jnp.where(i >= nct, 1, 0), 0, 0)),
                  tok(E),
                  pl.BlockSpec((1, E, cap, D), lambda b, i: (b, 0, 0, 0))],
        out_specs=tok(D),
        out_shape=jax.ShapeDtypeStruct((B, S, D), F32),
        compiler_params=_params("parallel", "parallel"),
        name="moe_scatter",
    )(z, modsel, slot_t, yc)


def kernel(x, c, ctx, c_ctx, ada_w, ada_b, norm1_g, norm2_g, w_in, ssm_lambda_re, ssm_lambda_im, ssm_log_dt, ssm_b_re, ssm_b_im, ssm_c_re, ssm_c_im, ssm_d, ssm_glu_w, ssm_glu_b, rwkv_mu, rwkv_w0, rwkv_w2, rwkv_a0, rwkv_a2, rwkv_g2, rwkv_k_k, rwkv_k_a, rwkv_r_k, rwkv_ln_w, rwkv_ln_b, mla_q_norm, mla_kv_norm, mla_w_uq, mla_w_ukv, mla_qn_nope, mla_kn_nope, mla_qn_rope, mla_kn_rope, w_branch, w_out, router_w, moe_w1, moe_w3, moe_w2):
    B, T, D = x.shape
    CTX = ctx.shape[1]
    S = CTX + T
    depth = ada_w.shape[0]
    cap_c = EC_CAPACITY * CTX // N_EXPERTS
    cap_l = EC_CAPACITY * T // N_EXPERTS
    assert D == D_MODEL and T % GRID_W == 0 and cap_c % 8 == 0 and cap_l % 8 == 0 and B % 8 == 0

    rows = -(-(B + 1) // 8) * 8
    cc = jnp.concatenate([c, c_ctx[None, :], jnp.zeros((rows - B - 1, D), F32)], axis=0)
    mods = _mods(cc, ada_w, ada_b).reshape(depth, rows, N_MOD, D)
    mod_lat = mods[:, :B]
    mod_ctx = jnp.broadcast_to(mods[:, B:B + 1], mod_lat.shape)
    modsel = jnp.stack([mod_ctx, mod_lat], axis=2)

    tables = _rope_tables(CTX, T)
    pos = jnp.arange(S)
    same = (pos[:, None] < CTX) == (pos[None, :] < CTX)
    tri = ((pos[:, None] <= pos[None, :]) & same).astype(BF16)

    z = jnp.concatenate([ctx, x], axis=1)
    for l in range(depth):
        g1, g2 = norm1_g[l].reshape(1, D), norm2_g[l].reshape(1, D)
        wl = w_in[l]
        w_ssm = wl[:, :RWKV_OFF].astype(BF16)
        w_rwkv = wl[:, RWKV_OFF:MLA_OFF].astype(BF16)
        w_mla = jnp.pad(wl[:, MLA_OFF:GATE_OFF], ((0, 0), (0, MLA_IN_PAD - MLA_IN))).astype(BF16)
        w_gate = wl[:, GATE_OFF:].astype(BF16)
        p_ssm, p_rwkv, p_mla = _input_proj(z, g1, modsel[l], w_ssm, w_rwkv, w_mla, CTX)

        prep = _ssm_prep(ssm_lambda_re[l], ssm_lambda_im[l], ssm_log_dt[l], ssm_b_re[l], ssm_b_im[l],
                         ssm_c_re[l], ssm_c_im[l])
        y_ssm = _ssm_scan(p_ssm, prep, ssm_d[l], B, CTX)

        rp = _rwkv_params(rwkv_mu[l], rwkv_w0[l], rwkv_w2[l], rwkv_a0[l], rwkv_a2[l], rwkv_g2[l],
                          rwkv_k_k[l], rwkv_k_a[l], rwkv_r_k[l], rwkv_ln_w[l], rwkv_ln_b[l])
        y_fwd = _rwkv_dir(p_rwkv, None, rp, 0, CTX)
        y_rwkv = _rwkv_dir(p_rwkv, y_fwd, rp, 1, CTX)

        mp = _mla_params(mla_q_norm[l], mla_kv_norm[l], mla_w_uq[l], mla_w_ukv[l], mla_qn_nope[l],
                         mla_kn_nope[l], mla_qn_rope[l], mla_kn_rope[l])
        q, k, v = _mla_prep(p_mla, tables, mp, CTX)
        y_mla = _attention(q, k, v, CTX)

        z, h2, logits_t = _merge(z, g1, g2, modsel[l], y_ssm, y_rwkv, y_mla, w_gate,
                                 w_branch[l].astype(BF16), w_out[l].astype(BF16),
                                 ssm_glu_w[l].astype(BF16), ssm_glu_b[l].reshape(1, -1),
                                 router_w[l].T, CTX)
        slot, gate = _route(logits_t, tri, CTX, cap_c, cap_l)
        yc = _experts(h2, slot, gate, moe_w1[l].astype(BF16), moe_w3[l].astype(BF16),
                      moe_w2[l].astype(BF16), cap_c + cap_l, cap_c, CTX)
        z = _scatter(z, modsel[l], jnp.swapaxes(slot, 1, 2), yc, CTX, cap_c)
    return z[:, CTX:]
```

```python
import functools
import math

import jax
import jax.numpy as jnp
from jax import lax
from jax.experimental import pallas as pl
from jax.experimental.pallas import tpu as pltpu

F32 = jnp.float32
BF16 = jnp.bfloat16

D_MODEL = 1024
GRID_W = 64
N_MOD = 6
NORM_EPS = 1e-6
GN_EPS = 64e-5
BRANCH_WIDTH = 512
SSM_GROUP = 16
SSM_GROUPS = BRANCH_WIDTH // SSM_GROUP
SSM_STATE = 64
SSM_CHUNK = 16
RWKV_W = BRANCH_WIDTH
RWKV_HEAD = 64
RWKV_HEADS = RWKV_W // RWKV_HEAD
RWKV_CHUNK = 64
LORA_W = 128
RWKV_IN = 3 * RWKV_W + 3 * LORA_W
MLA_HEADS = 8
MLA_NOPE = 64
MLA_ROPE = 32
MLA_V = 64
Q_LORA = 384
KV_LORA = 256
MLA_IN = Q_LORA + KV_LORA + MLA_ROPE
MLA_IN_PAD = 768
MLA_SCALE = 1.0 / math.sqrt(MLA_NOPE + MLA_ROPE)
ROPE_BASE = 10000.0
HEAD_PAD = 128
RWKV_OFF = BRANCH_WIDTH
MLA_OFF = RWKV_OFF + RWKV_IN
GATE_OFF = MLA_OFF + MLA_IN
N_EXPERTS = 16
EXPERT_FF = 1536
EC_CAPACITY = 2
VMEM_LIMIT = 56 * 1024 * 1024


def _params(*sem):
    return pltpu.CompilerParams(dimension_semantics=sem, vmem_limit_bytes=VMEM_LIMIT)


def _dot(a, b):
    return jnp.dot(a, b, preferred_element_type=F32)


def _dot_nt(a, b):
    return lax.dot_general(a, b, (((1,), (1,)), ((), ())), preferred_element_type=F32)


def _dot_tn(a, b):
    return lax.dot_general(a, b, (((0,), (0,)), ((), ())), preferred_element_type=F32)


def _split2(x):
    hi = x.astype(BF16)
    lo = (x - hi.astype(F32)).astype(BF16)
    return hi, lo


def _split3(x):
    hi = x.astype(BF16)
    r1 = x - hi.astype(F32)
    mid = r1.astype(BF16)
    lo = (r1 - mid.astype(F32)).astype(BF16)
    return hi, mid, lo


def _dot_exact_rhs(xs, m):
    rows = xs[0].shape[0]
    parts = [p for x in xs for p in _split2(x)]
    res = _dot(jnp.concatenate(parts, axis=0), m)
    return [res[(2 * i) * rows:(2 * i + 1) * rows] + res[(2 * i + 1) * rows:(2 * i + 2) * rows]
            for i in range(len(xs))]


def _dot_exact_lhs(m, x, parts):
    pieces = _split2(x) if parts == 2 else _split3(x)
    acc = _dot(m, pieces[0])
    for p in pieces[1:]:
        acc = acc + _dot(m, p)
    return acc


def _sigmoid(x):
    return 0.5 * jnp.tanh(0.5 * x) + 0.5


def _norm_mod(x, g, shift, scale):
    y = x * lax.rsqrt(jnp.mean(x * x, axis=-1, keepdims=True) + NORM_EPS)
    return (y * g) * (1.0 + scale) + shift


def _token_tile(ctx_len, seq):
    for tm in (256, 128, 64):
        if ctx_len % tm == 0 and seq % tm == 0:
            return tm
    raise ValueError("context and latent lengths must be multiples of 64")


def _mods_kernel(cc_ref, w_ref, b_ref, o_ref):
    c = cc_ref[...]
    s = (c * _sigmoid(c)).astype(BF16)
    o_ref[0] = _dot(s, w_ref[0].astype(BF16)) + b_ref[0]


def _mods(cc, ada_w, ada_b):
    L, D, N = ada_w.shape
    R = cc.shape[0]
    tn = 1536
    return pl.pallas_call(
        _mods_kernel,
        grid=(L, N // tn),
        in_specs=[pl.BlockSpec((R, D), lambda l, j: (0, 0)),
                  pl.BlockSpec((1, D, tn), lambda l, j: (l, 0, j)),
                  pl.BlockSpec((1, 1, tn), lambda l, j: (l, 0, j))],
        out_specs=pl.BlockSpec((1, R, tn), lambda l, j: (l, 0, j)),
        out_shape=jax.ShapeDtypeStruct((L, R, N), F32),
        compiler_params=_params("parallel", "parallel"),
        name="adaln_mods",
    )(cc, ada_w, ada_b.reshape(L, 1, N))


def _kin_kernel(z_ref, g_ref, mod_ref, ws_ref, wr_ref, wm_ref, os_ref, or_ref, om_ref):
    NB, tm = z_ref.shape[0], z_ref.shape[1]
    W = ws_ref.shape[1]
    h = jnp.concatenate([_norm_mod(z_ref[i], g_ref[...], mod_ref[i, 0][0:1], mod_ref[i, 0][1:2])
                         for i in range(NB)], axis=0).astype(BF16)
    ps, pr, pm = _dot(h, ws_ref[...]), _dot(h, wr_ref[...]), _dot(h, wm_ref[...])
    for i in range(NB):
        os_ref[:, i * W:(i + 1) * W] = ps[i * tm:(i + 1) * tm]
        or_ref[i] = pr[i * tm:(i + 1) * tm]
        om_ref[i] = pm[i * tm:(i + 1) * tm]


def _input_proj(z, g, modsel, ws, wr, wm, ctx_len):
    B, S, D = z.shape
    tm = _token_tile(ctx_len, S - ctx_len)
    nct = ctx_len // tm
    NB = 4
    tok = lambda n: pl.BlockSpec((NB, tm, n), lambda b, i: (b, i, 0))
    full = lambda a: pl.BlockSpec(a.shape, lambda b, i: (0,) * a.ndim)
    return pl.pallas_call(
        _kin_kernel,
        grid=(B // NB, S // tm),
        in_specs=[tok(D), full(g),
                  pl.BlockSpec((NB, 1, N_MOD, D), lambda b, i: (b, jnp.where(i >= nct, 1, 0), 0, 0)),
                  full(ws), full(wr), full(wm)],
        out_specs=[pl.BlockSpec((tm, NB * ws.shape[1]), lambda b, i: (i, b)), tok(wr.shape[1]), tok(wm.shape[1])],
        out_shape=[jax.ShapeDtypeStruct((S, B * ws.shape[1]), F32),
                   jax.ShapeDtypeStruct((B, S, wr.shape[1]), F32),
                   jax.ShapeDtypeStruct((B, S, wm.shape[1]), F32)],
        compiler_params=_params("parallel", "parallel"),
        name="input_proj",
    )(z, g, modsel, ws, wr, wm)


def _cpow(ar, ai, lag, shape, nbits=5):
    pr = jnp.ones(shape, F32)
    pi = jnp.zeros(shape, F32)
    for b in range(nbits):
        bit = ((lag >> b) & 1) == 1
        fr = jnp.where(bit, ar, 1.0)
        fi = jnp.where(bit, ai, 0.0)
        pr, pi = pr * fr - pi * fi, pr * fi + pi * fr
        ar, ai = ar * ar - ai * ai, 2.0 * ar * ai
    return pr, pi


def _ssm_prep_kernel(lc_re_ref, lc_im_ref, lr_re_ref, lr_im_ref, ldt_ref, bt_re_ref, bt_im_ref,
                     ct_re_ref, ct_im_ref, wt_ref, wso_ref, wsi_ref, a_ref):
    C, GC, P = SSM_CHUNK, SSM_GROUP, SSM_STATE
    W = C * GC
    lane = lax.broadcasted_iota(jnp.int32, (1, W), 1)
    quarter = lane // P
    is_re = (quarter == 0) | (quarter == 3)
    jcol = lane // GC
    srow = lax.broadcasted_iota(jnp.int32, (W, 1), 0) // GC
    for d in (0, 1):
        dt = jnp.exp(ldt_ref[d, 0])
        lr, li = lc_re_ref[d, 0], lc_im_ref[d, 0]
        mag = jnp.exp(lr * dt)
        ar, ai = mag * jnp.cos(li * dt), mag * jnp.sin(li * dt)
        cr, ci = ct_re_ref[d, 0], ct_im_ref[d, 0]
        lag_z = jcol if d == 0 else (C - 1) - jcol
        lag_s = jcol + 1 if d == 0 else C - jcol

        def q_of(lag):
            pr, pi = _cpow(ar, ai, lag, (P, W))
            q_re = cr * pr - ci * pi
            q_im = -(cr * pi + ci * pr)
            return q_re, q_im

        qz_re, qz_im = q_of(lag_z)
        qs_re, qs_im = q_of(lag_s)
        lr4, li4 = lr_re_ref[d, 0], lr_im_ref[d, 0]
        mag4 = jnp.exp(lr4 * dt)
        ar4, ai4 = mag4 * jnp.cos(li4 * dt), mag4 * jnp.sin(li4 * dt)
        den = lr4 * lr4 + li4 * li4
        nr, ni = ar4 - 1.0, ai4
        coef_re = (nr * lr4 + ni * li4) / den
        coef_im = (ni * lr4 - nr * li4) / den
        br, bi = bt_re_ref[0], bt_im_ref[0]
        bb_re = coef_re * br - coef_im * bi
        bb_im = coef_re * bi + coef_im * br
        bcat = jnp.where(lane < P, bb_re, bb_im)[:, :2 * P]
        qz = jnp.concatenate([qz_re, qz_im], axis=0)
        z = jnp.dot(bcat, qz, preferred_element_type=F32,
                    precision=lax.Precision.HIGHEST)
        rows = []
        for s in range(C):
            if d == 0:
                sh = pltpu.roll(z, GC * s, axis=1) if s else z
                rows.append(jnp.where(lane >= GC * s, sh, 0.0))
            else:
                m = C - 1 - s
                sh = pltpu.roll(z, W - GC * m, axis=1) if m else z
                rows.append(jnp.where(lane < W - GC * m, sh, 0.0))
        wt_ref[d, 0] = jnp.concatenate(rows, axis=0).astype(BF16)
        wsi_ref[d, 0] = jnp.concatenate([qs_re, qs_im], axis=0).astype(BF16)
        e_row = (C - 1) - srow if d == 0 else srow
        er, ei = _cpow(ar4, ai4, e_row, (W, W), nbits=4)
        bbx = jnp.where(is_re, bb_re, bb_im)
        bby = jnp.where(is_re, -bb_im, bb_re)
        bbx = jnp.concatenate([bbx] * C, axis=0)
        bby = jnp.concatenate([bby] * C, axis=0)
        wso_ref[d, 0] = (er * bbx + ei * bby).astype(BF16)
        cr16, ci16 = _cpow(ar4, ai4, jnp.full((1, W), C, jnp.int32), (1, W))
        a_ref[d, 0, 0:1, :] = cr16
        a_ref[d, 0, 1:2, :] = jnp.where(is_re, -ci16, ci16)


def _ssm_prep(lam_re, lam_im, log_dt, b_re, b_im, c_re, c_im):
    G, P, GC, C = SSM_GROUPS, SSM_STATE, SSM_GROUP, SSM_CHUNK
    W = C * GC
    lc_re, lc_im = lam_re[..., None], lam_im[..., None]
    lr_re = jnp.tile(lam_re, (1, 1, 4))[:, :, None, :]
    lr_im = jnp.tile(lam_im, (1, 1, 4))[:, :, None, :]
    ldt = log_dt[..., None, None]
    bt_re = jnp.tile(jnp.swapaxes(b_re, 1, 2), (1, 1, 4))
    bt_im = jnp.tile(jnp.swapaxes(b_im, 1, 2), (1, 1, 4))
    ct_re = jnp.tile(jnp.swapaxes(c_re, 2, 3), (1, 1, 1, C))
    ct_im = jnp.tile(jnp.swapaxes(c_im, 2, 3), (1, 1, 1, C))
    d4 = lambda a, b: pl.BlockSpec((2, 1, a, b), lambda g: (0, g, 0, 0))
    mat = jax.ShapeDtypeStruct((2, G, W, W), BF16)
    return pl.pallas_call(
        _ssm_prep_kernel,
        grid=(G,),
        in_specs=[d4(P, 1), d4(P, 1), d4(1, W), d4(1, W), d4(1, 1),
                  pl.BlockSpec((1, GC, W), lambda g: (g, 0, 0)),
                  pl.BlockSpec((1, GC, W), lambda g: (g, 0, 0)),
                  d4(P, W), d4(P, W)],
        out_specs=[d4(W, W), d4(W, W), d4(2 * P, W), d4(2, W)],
        out_shape=[mat, mat, jax.ShapeDtypeStruct((2, G, 2 * P, W), BF16),
                   jax.ShapeDtypeStruct((2, G, 2, W), F32)],
        compiler_params=_params("parallel"),
        name="ssm_prep",
    )(lc_re, lc_im, lr_re, lr_im, ldt, bt_re, bt_im, ct_re, ct_im)


def _ssm_kernel(*refs, d, final, KT):
    if final:
        x_ref, yp_ref, wt_ref, wso_ref, wsi_ref, a_ref, dn_ref, o_ref, st_ref = refs
    else:
        x_ref, wt_ref, wso_ref, wsi_ref, a_ref, o_ref, st_ref = refs
    C, GC, P = SSM_CHUNK, SSM_GROUP, SSM_STATE
    BT = x_ref.shape[2]
    GB = x_ref.shape[3] // GC
    R = KT * BT

    @pl.when(pl.program_id(2) == 0)
    def _():
        st_ref[...] = jnp.zeros_like(st_ref)

    xt = [x_ref[:, t].reshape(R, GB * GC) for t in range(C)]
    gs = range(GB)
    ub = [jnp.concatenate([xt[t][:, g * GC:(g + 1) * GC] for t in range(C)], axis=1).astype(BF16) for g in gs]
    loc = [_dot(ub[g], wso_ref[0, g]) for g in gs]
    a1 = [a_ref[0, g, 0:1, :] for g in gs]
    a2 = [a_ref[0, g, 1:2, :] for g in gs]
    x = [st_ref[g] for g in gs]
    xs = [[None] * KT for _ in gs]
    for k in (range(KT) if d == 0 else range(KT - 1, -1, -1)):
        for g in gs:
            xs[g][k] = x[g][:, :2 * P]
            x[g] = a1[g] * x[g] + a2[g] * pltpu.roll(x[g], 2 * P, axis=1) + loc[g][k * BT:(k + 1) * BT]
    ys = []
    for g in gs:
        st_ref[g] = x[g]
        xin = jnp.concatenate(xs[g], axis=0).astype(BF16)
        ys.append(_dot(ub[g], wt_ref[0, g]) + _dot(xin, wsi_ref[0, g]))
    for t in range(C):
        yt = jnp.concatenate([ys[g][:, t * GC:(t + 1) * GC] for g in gs], axis=1)
        if final:
            yt = yt + yp_ref[:, t].reshape(R, GB * GC) + xt[t] * dn_ref[...]
        o_ref[:, t] = yt.reshape(KT, BT, GB * GC)


def _ssm_dir(p_tm, y_prev, prep, d_skip, d, B, ctx_len):
    S = p_tm.shape[0]
    G, GC, C, P = SSM_GROUPS, SSM_GROUP, SSM_CHUNK, SSM_STATE
    W = C * GC
    LW = 128
    GB = LW // GC
    BT = 8
    KT = _token_tile(ctx_len, S - ctx_len) // C
    NT, NTc = S // (C * KT), ctx_len // (C * KT)
    wt, wso, wsi, a4 = prep
    final = y_prev is not None

    def tile(ti):
        if d == 0:
            return ti
        return jnp.where(ti < NTc, NTc - 1 - ti, NT - 1 - (ti - NTc))

    x4 = p_tm.reshape(S // C, C, B, G * GC)
    xspec = pl.BlockSpec((KT, C, BT, LW), lambda gb, bt, ti: (tile(ti), 0, bt, gb))
    wspec = lambda r: pl.BlockSpec((1, GB, r, W), lambda gb, bt, ti: (d, gb, 0, 0))
    in_specs, args = [xspec], [x4]
    if final:
        in_specs.append(xspec)
        args.append(y_prev.reshape(x4.shape))
    in_specs += [wspec(W), wspec(W), wspec(2 * P), wspec(2)]
    args += [wt, wso, wsi, a4]
    if final:
        in_specs.append(pl.BlockSpec((1, LW), lambda gb, bt, ti: (0, gb)))
        args.append(d_skip.reshape(1, G * GC))
    y = pl.pallas_call(
        functools.partial(_ssm_kernel, d=d, final=final, KT=KT),
        grid=(G // GB, B // BT, NT),
        in_specs=in_specs,
        out_specs=xspec,
        out_shape=jax.ShapeDtypeStruct(x4.shape, F32),
        scratch_shapes=[pltpu.VMEM((GB, BT, W), F32)],
        compiler_params=_params("parallel", "parallel", "arbitrary"),
        name="ssm_rev" if d else "ssm_fwd",
    )(*args)
    return y.reshape(p_tm.shape)


def _ssm_scan(p_tm, prep, d_skip, B, ctx_len):
    y_fwd = _ssm_dir(p_tm, None, prep, d_skip, 0, B, ctx_len)
    return _ssm_dir(p_tm, y_fwd, prep, d_skip, 1, B, ctx_len)


def _rwkv_kernel(*refs, d, final, NC, NCc):
    if final:
        (p_ref, hp_ref, hn_ref, yp_ref, mu_ref, w0_ref, w2_ref, a0_ref, a2_ref, g2_ref, pv_ref,
         o_ref, s_ref, xb_ref, yb_ref) = refs
    else:
        (p_ref, hp_ref, hn_ref, mu_ref, w0_ref, w2_ref, a0_ref, a2_ref, g2_ref, pv_ref,
         o_ref, s_ref, xb_ref, yb_ref) = refs
    L, N, H, W = RWKV_CHUNK, RWKV_HEAD, RWKV_HEADS, RWKV_W
    NB = p_ref.shape[0]
    ci = pl.program_id(1)
    if d == 0:
        c = ci
    else:
        c = jnp.where(ci < NCc, NCc - 1 - ci, NC - 1 - (ci - NCc))

    @pl.when(ci == 0)
    def _():
        s_ref[...] = jnp.zeros_like(s_ref)

    keep_prev = jnp.where((c == 0) | (c == NCc), 0.0, 1.0)
    keep_next = jnp.where((c == NCc - 1) | (c == NC - 1), 0.0, 1.0)
    xs = []
    for bi in range(NB):
        p = p_ref[bi]
        xb_ref[bi, 0:8, :] = hp_ref[bi] * keep_prev
        xb_ref[bi, 8:8 + L, :] = p
        xb_ref[bi, 8 + L:16 + L, :] = hn_ref[bi] * keep_next
        prev = xb_ref[bi, 7:7 + L, :]
        nxt = xb_ref[bi, 9:9 + L, :]
        xs.append(p + mu_ref[...] * (0.5 * (prev + nxt) - p))
    x = jnp.concatenate(xs, axis=0)

    r, k, v = x[:, 0:W], x[:, W:2 * W], x[:, 2 * W:3 * W]
    pw = x[:, 3 * W:3 * W + LORA_W]
    pa = x[:, 3 * W + LORA_W:3 * W + 2 * LORA_W]
    pg = x[:, 3 * W + 2 * LORA_W:3 * W + 3 * LORA_W]
    k_k, k_a, r_k = pv_ref[0:1, :], pv_ref[1:2, :], pv_ref[2:3, :]
    ln_w, ln_b = pv_ref[3:4, :], pv_ref[4:5, :]
    pab = pa.astype(BF16)

    zw = w0_ref[d:d + 1, :] + _dot(jnp.tanh(pw).astype(BF16), w2_ref[d])
    nz = -zw
    softplus = jnp.maximum(nz, 0.0) + jnp.log(1.0 + jnp.exp(-jnp.abs(nz)))
    lw = -jnp.exp(-softplus - 0.5)
    a = _sigmoid(a0_ref[d:d + 1, :] + _dot(pab, a2_ref[d]))
    kd = k * (1.0 + (a - 1.0) * k_a)

    hrow = lax.broadcasted_iota(jnp.int32, (W, W), 0) // N
    hcol = lax.broadcasted_iota(jnp.int32, (W, W), 1) // N
    head_ones = jnp.where(hrow == hcol, 1.0, 0.0).astype(BF16)

    kk = k * k_k
    kk = kk * lax.rsqrt(_dot_exact_rhs([kk * kk], head_ones)[0] + 1e-12)
    b = a * kk

    PW2 = 2 * N
    trow = lax.broadcasted_iota(jnp.int32, (L, PW2), 0)
    lane2 = lax.broadcasted_iota(jnp.int32, (L, PW2), 1)
    left = lane2 < N
    tcol = jnp.where(left, lane2, lane2 - N)
    if d == 0:
        strict, incl = tcol < trow, tcol <= trow
    else:
        strict, incl = tcol > trow, tcol >= trow

    def bdiag(t):
        return jnp.concatenate([jnp.where(left, t, 0.0), jnp.where(left, 0.0, t)], axis=0).astype(BF16)

    brow = lax.broadcasted_iota(jnp.int32, (NB * L, NB * L), 0)
    bcol = lax.broadcasted_iota(jnp.int32, (NB * L, NB * L), 1)
    upto = (bcol <= brow) if d == 0 else (bcol >= brow)
    cum = jnp.where(upto, jnp.where((brow // L) == (bcol // L), 1.0, 0.0), 0.0).astype(BF16)
    cs = _dot_exact_lhs(cum, lw, 3)
    last = L - 1 if d == 0 else 0
    cls = [cs[bi * L + last:bi * L + last + 1, :] for bi in range(NB)]
    cl = jnp.concatenate([jnp.broadcast_to(t, (L, W)) for t in cls], axis=0)
    e_to_end = jnp.exp(cl - cs)
    e_neg = jnp.exp(-cs)
    rt = (r * jnp.exp(cs)).astype(BF16)
    at = (kk * jnp.exp(cs - lw)).astype(BF16)
    bt = b * e_neg
    kt = kd * e_neg
    kh = (kd * e_to_end).astype(BF16)
    bh = (b * e_to_end).astype(BF16)
    e_chunk = [jnp.exp(t) for t in cls]
    vb = v.astype(BF16)

    ids = [(bi, j) for bi in range(NB) for j in range(H // 2)]
    n = range(len(ids))
    rs = [slice(bi * L, (bi + 1) * L) for bi, _ in ids]
    ls = [slice(j * PW2, (j + 1) * PW2) for _, j in ids]
    ar = [jnp.concatenate([at[rs[i], ls[i]], rt[rs[i], ls[i]]], axis=0) for i in n]
    s0 = [s_ref[bi, j] for bi, j in ids]
    g_b = [_dot_nt(ar[i], bdiag(bt[rs[i], ls[i]])) for i in n]
    g_k = [_dot_nt(ar[i], bdiag(kt[rs[i], ls[i]])) for i in n]
    g_s = [_dot_nt(ar[i], bdiag(s0[i])) for i in n]
    nab = [jnp.where(strict, g_b[i][:L], 0.0) for i in n]
    mrb = [jnp.where(incl, g_b[i][L:], 0.0).astype(BF16) for i in n]
    nm = [jnp.concatenate([jnp.where(strict, g_k[i][:L], 0.0), jnp.where(incl, g_k[i][L:], 0.0)],
                          axis=0).astype(BF16) for i in n]
    nv = [g_s[i] + _dot(nm[i], bdiag(v[rs[i], ls[i]])) for i in n]
    pm = [-t for t in nab]
    q = [_dot(t.astype(BF16), bdiag(t)) for t in nab]
    steps = int(math.log2(L)) - 1
    for it in range(steps):
        qd = [bdiag(t) for t in q]
        if it + 1 < steps:
            pq = [_dot(jnp.concatenate([pm[i], q[i]], axis=0).astype(BF16), qd[i]) for i in n]
            pm = [pm[i] + q[i] + pq[i][:L] for i in n]
            q = [t[L:] for t in pq]
        else:
            pm = [pm[i] + q[i] + _dot(pm[i].astype(BF16), qd[i]) for i in n]
    u = [nv[i][:L] + _dot(pm[i].astype(BF16), bdiag(nv[i][:L])) for i in n]
    for i in n:
        yb_ref[rs[i], ls[i]] = nv[i][L:] - _dot(mrb[i], bdiag(u[i]))
    for i, (bi, j) in enumerate(ids):
        vu = jnp.concatenate([vb[rs[i], ls[i]], -u[i].astype(BF16)], axis=0)
        kb = jnp.concatenate([kh[rs[i], ls[i]], bh[rs[i], ls[i]]], axis=0)
        full = _dot_tn(vu, kb)
        s_ref[bi, j] = s0[i] * e_chunk[bi][:, ls[i]] + jnp.where(left, full[:L], full[L:])

    if not final:
        o_ref[...] = yb_ref[...].reshape(NB, L, W)
        return
    y = yp_ref[...].reshape(NB * L, W) + yb_ref[...]
    o = 1 - d
    a_o = _sigmoid(a0_ref[o:o + 1, :] + _dot(pab, a2_ref[o]))
    kd_sum = kd + k * (1.0 + (a_o - 1.0) * k_a)
    y_sum, rk_sum = _dot_exact_rhs([y, r * kd_sum * r_k], head_ones)
    dev = y - y_sum * (1.0 / N)
    var = _dot_exact_rhs([dev * dev], head_ones)[0] * (1.0 / N)
    yn = dev * lax.rsqrt(var + GN_EPS) * ln_w + ln_b
    bonus = rk_sum * v
    g = _dot(_sigmoid(pg).astype(BF16), g2_ref[...])
    o_ref[...] = ((yn + bonus) * g).reshape(NB, L, W)


def _rwkv_dir(p_rwkv, y_prev, params, d, ctx_len):
    B, S, PW = p_rwkv.shape
    L, W = RWKV_CHUNK, RWKV_W
    NC, NCc = S // L, ctx_len // L
    NB = 2
    final = y_prev is not None

    def chunk(ci):
        if d == 0:
            return ci
        return jnp.where(ci < NCc, NCc - 1 - ci, NC - 1 - (ci - NCc))

    hb = L // 8
    tok = lambda n: pl.BlockSpec((NB, L, n), lambda b, ci: (b, chunk(ci), 0))
    full = lambda a: pl.BlockSpec(a.shape, lambda b, ci: (0,) * a.ndim)
    in_specs = [tok(PW),
                pl.BlockSpec((NB, 8, PW), lambda b, ci: (b, jnp.maximum(chunk(ci) * hb - 1, 0), 0)),
                pl.BlockSpec((NB, 8, PW), lambda b, ci: (b, jnp.minimum((chunk(ci) + 1) * hb, S // 8 - 1), 0))]
    args = [p_rwkv, p_rwkv, p_rwkv]
    if final:
        in_specs.append(tok(W))
        args.append(y_prev)
    in_specs += [full(a) for a in params]
    args += list(params)
    return pl.pallas_call(
        functools.partial(_rwkv_kernel, d=d, final=final, NC=NC, NCc=NCc),
        grid=(B // NB, NC),
        in_specs=in_specs,
        out_specs=tok(W),
        out_shape=jax.ShapeDtypeStruct((B, S, W), F32),
        scratch_shapes=[pltpu.VMEM((NB, RWKV_HEADS // 2, RWKV_HEAD, 2 * RWKV_HEAD), F32),
                        pltpu.VMEM((NB, L + 16, PW), F32),
                        pltpu.VMEM((NB * L, W), F32)],
        compiler_params=_params("parallel", "arbitrary"),
        name="rwkv_rev" if d else "rwkv_fwd",
    )(*args)


def _rwkv_params(mu, w0, w2, a0, a2, g2, k_k, k_a, r_k, ln_w, ln_b):
    W = RWKV_W
    half = LORA_W // 2

    def pad_dir(w):
        out = jnp.zeros((2, LORA_W, W), F32)
        out = out.at[0, :half].set(w[0]).at[1, half:].set(w[1])
        return out.astype(BF16)

    pv = jnp.zeros((8, W), F32)
    pv = pv.at[0].set(k_k).at[1].set(k_a).at[2].set(r_k.reshape(W)).at[3].set(ln_w).at[4].set(ln_b)
    return (mu.reshape(1, RWKV_IN), w0, pad_dir(w2), a0, pad_dir(a2), g2.astype(BF16), pv)


def _mla_prep_kernel(p_ref, cq_ref, sq_ref, ck_ref, sk_ref, qn_ref, kvn_ref, wq_ref, wk_ref, wv_ref,
                     gn_ref, q_ref, k_ref, v_ref):
    HP = HEAD_PAD
    p = p_ref[0]
    lane = lax.broadcasted_iota(jnp.int32, (1, HP), 1)
    m_nope = jnp.where(lane < MLA_NOPE, 1.0, 0.0)
    m_rope = jnp.where(lane < MLA_NOPE, 0.0, jnp.where(lane < MLA_NOPE + MLA_ROPE, 1.0, 0.0))
    g_q, g_kn, g_kr = gn_ref[0:1, :], gn_ref[1:2, :], gn_ref[2:3, :]

    def rms(x, n):
        return lax.rsqrt(jnp.sum(x * x, axis=-1, keepdims=True) * (1.0 / n) + NORM_EPS)

    xq = p[:, :Q_LORA]
    cq = (xq * rms(xq, Q_LORA) * qn_ref[...]).astype(BF16)
    xkv = p[:, Q_LORA:Q_LORA + KV_LORA]
    ckv = (xkv * rms(xkv, KV_LORA) * kvn_ref[...]).astype(BF16)
    q = _dot(cq, wq_ref[...])
    kn = _dot(ckv, wk_ref[...])
    v_ref[0] = _dot_nt(wv_ref[...], ckv).astype(BF16)

    kr = p[:, Q_LORA + KV_LORA:Q_LORA + KV_LORA + HP]
    krn = kr * rms(kr, MLA_ROPE) * g_kr
    to_rope = pltpu.roll(krn, MLA_NOPE, axis=1)
    sw_a = jnp.where(lane >= 112, 0.0, jnp.where(lane >= 96, pltpu.roll(krn, 80, axis=1), 0.0))
    sw_b = jnp.where(lane >= 112, pltpu.roll(krn, 112, axis=1), 0.0)
    kext = to_rope + sw_a + sw_b
    krot = kext * ck_ref[...] + pltpu.roll(kext, HP - MLA_ROPE, axis=1) * sk_ref[...]

    cq_t, sq_t = cq_ref[...], sq_ref[...]
    for h in range(MLA_HEADS):
        sl = slice(h * HP, (h + 1) * HP)
        qh = q[:, sl]
        scale = m_nope * rms(qh * m_nope, MLA_NOPE) + (1.0 - m_nope) * rms(qh * m_rope, MLA_ROPE)
        qn = qh * scale * g_q
        q_ref[0, :, sl] = (qn * cq_t + pltpu.roll(qn, HP - MLA_ROPE, axis=1) * sq_t).astype(BF16)
        kh = kn[:, sl]
        k_ref[0, :, sl] = (kh * rms(kh, MLA_NOPE) * g_kn + krot).astype(BF16)


def _mla_prep(p_mla, tables, params, ctx_len):
    B, S, PW = p_mla.shape
    tm = _token_tile(ctx_len, S - ctx_len)
    cq_t, sq_t, ck_t, sk_t = tables
    HW = MLA_HEADS * HEAD_PAD
    tok = lambda n: pl.BlockSpec((1, tm, n), lambda b, i: (b, i, 0))
    tab = pl.BlockSpec((tm, HEAD_PAD), lambda b, i: (i, 0))
    full = lambda a: pl.BlockSpec(a.shape, lambda b, i: (0,) * a.ndim)
    return pl.pallas_call(
        _mla_prep_kernel,
        grid=(B, S // tm),
        in_specs=[tok(PW), tab, tab, tab, tab] + [full(a) for a in params],
        out_specs=[tok(HW), tok(HW), pl.BlockSpec((1, MLA_HEADS * MLA_V, tm), lambda b, i: (b, 0, i))],
        out_shape=[jax.ShapeDtypeStruct((B, S, HW), BF16), jax.ShapeDtypeStruct((B, S, HW), BF16),
                   jax.ShapeDtypeStruct((B, MLA_HEADS * MLA_V, S), BF16)],
        compiler_params=_params("parallel", "parallel"),
        name="mla_prep",
    )(p_mla, cq_t, sq_t, ck_t, sk_t, *params)


def _mla_params(q_norm, kv_norm, w_uq, w_ukv, qn_nope, kn_nope, qn_rope, kn_rope):
    H, NP, RP, HP = MLA_HEADS, MLA_NOPE, MLA_ROPE, HEAD_PAD
    half = RP // 2
    swap = jnp.concatenate([jnp.arange(half, RP), jnp.arange(0, half)])
    wq = w_uq.reshape(Q_LORA, H, NP + RP)
    wq = jnp.concatenate([wq, wq[:, :, NP + swap]], axis=-1).reshape(Q_LORA, H * HP)
    wkv = w_ukv.reshape(KV_LORA, H, NP + MLA_V)
    wk = jnp.concatenate([wkv[:, :, :NP], jnp.zeros((KV_LORA, H, HP - NP), F32)], axis=-1)
    wk = wk.reshape(KV_LORA, H * HP)
    wv = wkv[:, :, NP:].reshape(KV_LORA, H * MLA_V)
    gn = jnp.zeros((8, HP), F32)
    gn = gn.at[0].set(jnp.concatenate([qn_nope, qn_rope, qn_rope[swap]]))
    gn = gn.at[1, :NP].set(kn_nope).at[2, :RP].set(kn_rope)
    return (q_norm.reshape(1, Q_LORA), kv_norm.reshape(1, KV_LORA),
            wq.astype(BF16), wk.astype(BF16), wv.T.astype(BF16), gn)


def _rope_tables(ctx_len, seq):
    rows = seq // GRID_W
    axis_dims = MLA_ROPE // 2
    row = jnp.repeat(jnp.arange(rows), GRID_W).astype(F32)
    col = jnp.tile(jnp.arange(GRID_W), rows).astype(F32)
    inv = ROPE_BASE ** (-jnp.arange(0, axis_dims, 2, dtype=F32) / axis_dims)
    ang = jnp.concatenate([row[:, None] * inv, col[:, None] * inv], axis=-1)
    cos = jnp.concatenate([jnp.ones((ctx_len, axis_dims), F32), jnp.cos(ang)], axis=0)
    sin = jnp.concatenate([jnp.zeros((ctx_len, axis_dims), F32), jnp.sin(ang)], axis=0)
    S = ctx_len + seq
    pad = jnp.zeros((S, HEAD_PAD - MLA_NOPE - MLA_ROPE), F32)
    cos_t = jnp.concatenate([jnp.ones((S, MLA_NOPE), F32), cos, cos, pad], axis=-1)
    sin_t = jnp.concatenate([jnp.zeros((S, MLA_NOPE), F32), -sin, sin, pad], axis=-1)
    qs = MLA_SCALE * math.log2(math.e)
    return cos_t * qs, sin_t * qs, cos_t, sin_t


def _attn_kernel(q_ref, k_ref, vt_ref, o_ref, *, nct, ctx_len):
    S = k_ref.shape[1]
    i = pl.program_id(2)

    def attend(nk):
        hs = range(q_ref.shape[2] // HEAD_PAD)
        s = [_dot_nt(k_ref[0, :nk, h * HEAD_PAD:(h + 1) * HEAD_PAD],
                     q_ref[0, :, h * HEAD_PAD:(h + 1) * HEAD_PAD]) for h in hs]
        m = [jnp.max(t, axis=0, keepdims=True) for t in s]
        e = [jnp.exp2(s[h] - m[h]) for h in hs]
        l = [jnp.sum(t, axis=0, keepdims=True) for t in e]
        o = [_dot(vt_ref[0, h * MLA_V:(h + 1) * MLA_V, :nk], e[h].astype(BF16)) / l[h] for h in hs]
        for j in range(len(hs) // 2):
            pair = jnp.concatenate([o[2 * j], o[2 * j + 1]], axis=0)
            o_ref[0, :, j * 2 * MLA_V:(j + 1) * 2 * MLA_V] = pair.T.astype(BF16)

    @pl.when(i < nct)
    def _():
        attend(ctx_len)

    @pl.when(i >= nct)
    def _():
        attend(S)


def _attention(q, k, v, ctx_len):
    B, S, _ = q.shape
    tq = _token_tile(ctx_len, S - ctx_len)
    HS = 4
    return pl.pallas_call(
        functools.partial(_attn_kernel, nct=ctx_len // tq, ctx_len=ctx_len),
        grid=(B, MLA_HEADS // HS, S // tq),
        in_specs=[pl.BlockSpec((1, tq, HS * HEAD_PAD), lambda b, h, i: (b, i, h)),
                  pl.BlockSpec((1, S, HS * HEAD_PAD), lambda b, h, i: (b, 0, h)),
                  pl.BlockSpec((1, HS * MLA_V, S), lambda b, h, i: (b, h, 0))],
        out_specs=pl.BlockSpec((1, tq, HS * MLA_V), lambda b, h, i: (b, i, h)),
        out_shape=jax.ShapeDtypeStruct((B, S, MLA_HEADS * MLA_V), BF16),
        compiler_params=_params("parallel", "parallel", "parallel"),
        name="mla_attention",
    )(q, k, v)


def _merge_kernel(z_ref, g1_ref, g2_ref, mod_ref, ys_ref, yr_ref, ym_ref, wg_ref, wb_ref, wo_ref,
                  glw_ref, glb_ref, rw_ref, zo_ref, h2_ref, lg_ref):
    D, W = D_MODEL, BRANCH_WIDTH
    NB, tm = z_ref.shape[0], z_ref.shape[1]
    ns = range(NB)
    rows = lambda t: jnp.concatenate(t, axis=0)
    z = [z_ref[i] for i in ns]
    m = [mod_ref[i, 0] for i in ns]
    h = rows([_norm_mod(z[i], g1_ref[...], m[i][0:1], m[i][1:2]) for i in ns]).astype(BF16)
    ys = rows([ys_ref[:, i * W:(i + 1) * W] for i in ns])
    ys = 0.5 * ys * (1.0 + jnp.tanh(math.sqrt(2.0 / math.pi) * (ys + 0.044715 * ys * ys * ys)))
    ys = ys * _sigmoid(_dot(ys.astype(BF16), glw_ref[...]) + glb_ref[...])
    branches = (ys.astype(BF16), rows([yr_ref[i] for i in ns]).astype(BF16), rows([ym_ref[i] for i in ns]))
    acc = jnp.zeros((NB * tm, D), F32)
    for j, yj in enumerate(branches):
        gate = _sigmoid(_dot(h, wg_ref[:, j * D:(j + 1) * D]))
        acc = acc + gate * _dot(yj, wb_ref[j])
    mix = _dot(acc.astype(BF16), wo_ref[...])
    zn = [z[i] + m[i][2:3] * mix[i * tm:(i + 1) * tm] for i in ns]
    h2 = rows([_norm_mod(zn[i], g2_ref[...], m[i][3:4], m[i][4:5]) for i in ns])
    E = rw_ref.shape[0]
    rw_rows = jnp.concatenate(_split3(rw_ref[...]), axis=0)
    hp = _split3(h2)
    t0 = _dot_nt(rw_rows, hp[0])
    t1 = _dot_nt(rw_rows[:2 * E], hp[1])
    t2 = _dot_nt(rw_rows[:E], hp[2])
    lg = t0[:E] + t0[E:2 * E] + t0[2 * E:] + t1[:E] + t1[E:] + t2
    for i in ns:
        zo_ref[i] = zn[i]
        h2_ref[i] = hp[0][i * tm:(i + 1) * tm]
        lg_ref[i] = lg[:, i * tm:(i + 1) * tm]


def _merge(z, g1, g2, modsel, ys, yr, ym, wg, wb, wo, glw, glb, rwt, ctx_len):
    B, S, D = z.shape
    tm = _token_tile(ctx_len, S - ctx_len)
    nct = ctx_len // tm
    E = rwt.shape[0]
    NB = 2
    tok = lambda n: pl.BlockSpec((NB, tm, n), lambda b, i: (b, i, 0))
    full = lambda a: pl.BlockSpec(a.shape, lambda b, i: (0,) * a.ndim)
    W = BRANCH_WIDTH
    return pl.pallas_call(
        _merge_kernel,
        grid=(B // NB, S // tm),
        in_specs=[tok(D), full(g1), full(g2),
                  pl.BlockSpec((NB, 1, N_MOD, D), lambda b, i: (b, jnp.where(i >= nct, 1, 0), 0, 0)),
                  pl.BlockSpec((tm, NB * W), lambda b, i: (i, b)),
                  tok(W), tok(W), full(wg), full(wb), full(wo), full(glw), full(glb), full(rwt)],
        out_specs=[tok(D), tok(D), pl.BlockSpec((NB, E, tm), lambda b, i: (b, 0, i))],
        out_shape=[jax.ShapeDtypeStruct((B, S, D), F32), jax.ShapeDtypeStruct((B, S, D), BF16),
                   jax.ShapeDtypeStruct((B, E, S), F32)],
        compiler_params=_params("parallel", "parallel"),
        name="merge",
    )(z, g1, g2, modsel, ys, yr, ym, wg, wb, wo, glw, glb, rwt)


def _route_kernel(lg_ref, slot_ref, gate_ref, *, ctx_len, cap_c, cap_l):
    lg = lg_ref[0]
    E, S = lg.shape
    e = jnp.exp(lg - jnp.max(lg, axis=0, keepdims=True))
    aff = e / jnp.sum(e, axis=0, keepdims=True)
    bits = pltpu.bitcast(aff, jnp.int32)
    lane = lax.broadcasted_iota(jnp.int32, (1, S), 1)
    in_c = lane < ctx_len
    ms_c = jnp.where(in_c, 1.0, 0.0)
    LT = 128
    tr = lax.broadcasted_iota(jnp.int32, (LT, LT), 0)
    tc = lax.broadcasted_iota(jnp.int32, (LT, LT), 1)
    tri = jnp.where(tr <= tc, 1.0, 0.0).astype(BF16)

    def counts(x):
        n_c = jnp.sum(x * ms_c, axis=1, keepdims=True)
        return n_c, jnp.sum(x, axis=1, keepdims=True) - n_c

    def prefix(x):
        tiles = [x[:, t * LT:(t + 1) * LT] for t in range(S // LT)]
        inc = _dot(jnp.concatenate(tiles, axis=0).astype(BF16), tri)
        off = jnp.zeros((E, 1), F32)
        out = []
        for t, xt in enumerate(tiles):
            it = inc[t * E:(t + 1) * E]
            out.append(it - xt + off)
            off = off + it[:, LT - 1:LT]
        n_c, _ = counts(x)
        return jnp.concatenate(out, axis=1) - jnp.where(in_c, 0.0, n_c)

    def body(_, carry):
        lo_c, hi_c, lo_l, hi_l = carry
        mid_c = lo_c + ((hi_c - lo_c + 1) >> 1)
        mid_l = lo_l + ((hi_l - lo_l + 1) >> 1)
        n_c, n_l = counts(jnp.where(bits >= jnp.where(in_c, mid_c, mid_l), 1.0, 0.0))
        ok_c, ok_l = n_c >= cap_c, n_l >= cap_l
        return (jnp.where(ok_c, mid_c, lo_c), jnp.where(ok_c, hi_c, mid_c - 1),
                jnp.where(ok_l, mid_l, lo_l), jnp.where(ok_l, hi_l, mid_l - 1))

    zero = jnp.zeros((E, 1), jnp.int32)
    top = jnp.full((E, 1), 0x7F800000, jnp.int32)
    lo_c, _, lo_l, _ = lax.fori_loop(0, 31, body, (zero, top, zero, top))
    thr = jnp.where(in_c, lo_c, lo_l)
    gt = jnp.where(bits > thr, 1.0, 0.0)
    eq = jnp.where(bits == thr, 1.0, 0.0)
    g_c, g_l = counts(gt)
    need = jnp.where(in_c, cap_c - g_c, cap_l - g_l)
    sel = gt + eq * jnp.where(prefix(eq) < need, 1.0, 0.0)
    rank = prefix(sel) + jnp.where(in_c, 0.0, float(cap_c))
    slot_ref[0] = jnp.where(sel > 0.0, rank, -1.0)
    gate_ref[0] = aff * sel


def _route(logits_t, ctx_len, cap_c, cap_l):
    B, E, S = logits_t.shape
    assert S % 128 == 0
    spec = pl.BlockSpec((1, E, S), lambda b: (b, 0, 0))
    return pl.pallas_call(
        functools.partial(_route_kernel, ctx_len=ctx_len, cap_c=cap_c, cap_l=cap_l),
        grid=(B,),
        in_specs=[spec],
        out_specs=[spec, spec],
        out_shape=[jax.ShapeDtypeStruct((B, E, S), F32)] * 2,
        compiler_params=_params("parallel"),
        name="route",
    )(logits_t)


def _expert_kernel(h_ref, slot_ref, gate_ref, w1_ref, w3_ref, w2_ref, o_ref, *, cap, cap_c, ctx_len):
    S = h_ref.shape[1]
    xs, gc = [], []
    for (c0, c1), (t0, t1) in (((0, cap_c), (0, ctx_len)), ((cap_c, cap), (ctx_len, S))):
        slot = slot_ref[0, 0, :, t0:t1]
        gate = gate_ref[0, 0, :, t0:t1]
        cidx = lax.broadcasted_iota(jnp.int32, (c1 - c0, t1 - t0), 0).astype(F32) + float(c0)
        hit = slot == cidx
        onehot = jnp.where(hit, 1.0, 0.0).astype(BF16)
        xs.append(_dot(onehot, h_ref[0, t0:t1, :]))
        gc.append(jnp.sum(jnp.where(hit, gate, 0.0), axis=1, keepdims=True))
    xs = jnp.concatenate(xs, axis=0).astype(BF16)
    gc = jnp.concatenate(gc, axis=0)
    a1 = _dot(xs, w1_ref[0])
    a3 = _dot(xs, w3_ref[0])
    hid = (a1 * _sigmoid(a1) * a3).astype(BF16)
    o_ref[0, 0] = (_dot(hid, w2_ref[0]) * gc).astype(BF16)


def _experts(h2, slot, gate, w1, w3, w2, cap, cap_c, ctx_len):
    B, S, D = h2.shape
    E, _, F = w1.shape
    row = pl.BlockSpec((1, 1, 1, S), lambda e, b: (b, e, 0, 0))
    return pl.pallas_call(
        functools.partial(_expert_kernel, cap=cap, cap_c=cap_c, ctx_len=ctx_len),
        grid=(E, B),
        in_specs=[pl.BlockSpec((1, S, D), lambda e, b: (b, 0, 0)), row, row,
                  pl.BlockSpec((1, D, F), lambda e, b: (e, 0, 0)),
                  pl.BlockSpec((1, D, F), lambda e, b: (e, 0, 0)),
                  pl.BlockSpec((1, F, D), lambda e, b: (e, 0, 0))],
        out_specs=pl.BlockSpec((1, 1, cap, D), lambda e, b: (b, e, 0, 0)),
        out_shape=jax.ShapeDtypeStruct((B, E, cap, D), BF16),
        compiler_params=_params("parallel", "parallel"),
        name="experts",
    )(h2, slot.reshape(B, E, 1, S), gate.reshape(B, E, 1, S), w1, w3, w2)


def _scatter_kernel(z_ref, mod_ref, st_ref, y_ref, o_ref, *, cap_c, cap, nct):
    st = st_ref[0]
    tm, E = st.shape

    def add(lo, hi):
        cidx = lax.broadcasted_iota(jnp.int32, (tm, hi - lo), 1).astype(F32) + float(lo)
        onehot = [jnp.where(st[:, e:e + 1] == cidx, 1.0, 0.0).astype(BF16) for e in range(E)]
        acc = jnp.zeros(z_ref.shape[1:], F32)
        for e in range(E):
            acc = acc + _dot(onehot[e], y_ref[0, e, lo:hi, :])
        o_ref[0] = z_ref[0] + mod_ref[0, 0][5:6] * acc

    @pl.when(pl.program_id(1) < nct)
    def _():
        add(0, cap_c)

    @pl.when(pl.program_id(1) >= nct)
    def _():
        add(cap_c, cap)


def _scatter(z, modsel, slot_t, yc, ctx_len, cap_c):
    B, S, D = z.shape
    _, E, cap, _ = yc.shape
    tm = _token_tile(ctx_len, S - ctx_len)
    nct = ctx_len // tm
    tok = lambda n: pl.BlockSpec((1, tm, n), lambda b, i: (b, i, 0))
    return pl.pallas_call(
        functools.partial(_scatter_kernel, cap_c=cap_c, cap=cap, nct=nct),
        grid=(B, S // tm),
        in_specs=[tok(D),
                  pl.BlockSpec((1, 1, N_MOD, D), lambda b, i: (b, jnp.where(i >= nct, 1, 0), 0, 0)),
                  tok(E),
                  pl.BlockSpec((1, E, cap, D), lambda b, i: (b, 0, 0, 0))],
        out_specs=tok(D),
        out_shape=jax.ShapeDtypeStruct((B, S, D), F32),
        compiler_params=_params("parallel", "parallel"),
        name="moe_scatter",
    )(z, modsel, slot_t, yc)


def kernel(x, c, ctx, c_ctx, ada_w, ada_b, norm1_g, norm2_g, w_in, ssm_lambda_re, ssm_lambda_im, ssm_log_dt, ssm_b_re, ssm_b_im, ssm_c_re, ssm_c_im, ssm_d, ssm_glu_w, ssm_glu_b, rwkv_mu, rwkv_w0, rwkv_w2, rwkv_a0, rwkv_a2, rwkv_g2, rwkv_k_k, rwkv_k_a, rwkv_r_k, rwkv_ln_w, rwkv_ln_b, mla_q_norm, mla_kv_norm, mla_w_uq, mla_w_ukv, mla_qn_nope, mla_kn_nope, mla_qn_rope, mla_kn_rope, w_branch, w_out, router_w, moe_w1, moe_w3, moe_w2):
    B, T, D = x.shape
    CTX = ctx.shape[1]
    S = CTX + T
    depth = ada_w.shape[0]
    cap_c = EC_CAPACITY * CTX // N_EXPERTS
    cap_l = EC_CAPACITY * T // N_EXPERTS
    assert D == D_MODEL and T % GRID_W == 0 and cap_c % 8 == 0 and cap_l % 8 == 0 and B % 8 == 0

    rows = -(-(B + 1) // 8) * 8
    cc = jnp.concatenate([c, c_ctx[None, :], jnp.zeros((rows - B - 1, D), F32)], axis=0)
    mods = _mods(cc, ada_w, ada_b).reshape(depth, rows, N_MOD, D)
    mod_lat = mods[:, :B]
    mod_ctx = jnp.broadcast_to(mods[:, B:B + 1], mod_lat.shape)
    modsel = jnp.stack([mod_ctx, mod_lat], axis=2)

    tables = _rope_tables(CTX, T)

    z = jnp.concatenate([ctx, x], axis=1)
    for l in range(depth):
        g1, g2 = norm1_g[l].reshape(1, D), norm2_g[l].reshape(1, D)
        wl = w_in[l]
        w_ssm = wl[:, :RWKV_OFF].astype(BF16)
        w_rwkv = wl[:, RWKV_OFF:MLA_OFF].astype(BF16)
        w_mla = jnp.pad(wl[:, MLA_OFF:GATE_OFF], ((0, 0), (0, MLA_IN_PAD - MLA_IN))).astype(BF16)
        w_gate = wl[:, GATE_OFF:].astype(BF16)
        p_ssm, p_rwkv, p_mla = _input_proj(z, g1, modsel[l], w_ssm, w_rwkv, w_mla, CTX)

        prep = _ssm_prep(ssm_lambda_re[l], ssm_lambda_im[l], ssm_log_dt[l], ssm_b_re[l], ssm_b_im[l],
                         ssm_c_re[l], ssm_c_im[l])
        y_ssm = _ssm_scan(p_ssm, prep, ssm_d[l], B, CTX)

        rp = _rwkv_params(rwkv_mu[l], rwkv_w0[l], rwkv_w2[l], rwkv_a0[l], rwkv_a2[l], rwkv_g2[l],
                          rwkv_k_k[l], rwkv_k_a[l], rwkv_r_k[l], rwkv_ln_w[l], rwkv_ln_b[l])
        y_fwd = _rwkv_dir(p_rwkv, None, rp, 0, CTX)
        y_rwkv = _rwkv_dir(p_rwkv, y_fwd, rp, 1, CTX)

        mp = _mla_params(mla_q_norm[l], mla_kv_norm[l], mla_w_uq[l], mla_w_ukv[l], mla_qn_nope[l],
                         mla_kn_nope[l], mla_qn_rope[l], mla_kn_rope[l])
        q, k, v = _mla_prep(p_mla, tables, mp, CTX)
        y_mla = _attention(q, k, v, CTX)

        z, h2, logits_t = _merge(z, g1, g2, modsel[l], y_ssm, y_rwkv, y_mla, w_gate,
                                 w_branch[l].astype(BF16), w_out[l].astype(BF16),
                                 ssm_glu_w[l].astype(BF16), ssm_glu_b[l].reshape(1, -1),
                                 router_w[l].T, CTX)
        slot, gate = _route(logits_t, CTX, cap_c, cap_l)
        yc = _experts(h2, slot, gate, moe_w1[l].astype(BF16), moe_w3[l].astype(BF16),
                      moe_w2[l].astype(BF16), cap_c + cap_l, cap_c, CTX)
        z = _scatter(z, modsel[l], jnp.swapaxes(slot, 1, 2), yc, CTX, cap_c)
    return z[:, CTX:]
```

```python
import functools
import math

import jax
import jax.numpy as jnp
from jax import lax
from jax.experimental import pallas as pl
from jax.experimental.pallas import tpu as pltpu

F32 = jnp.float32
BF16 = jnp.bfloat16

D_MODEL = 1024
GRID_W = 64
N_MOD = 6
NORM_EPS = 1e-6
GN_EPS = 64e-5
BRANCH_WIDTH = 512
SSM_GROUP = 16
SSM_GROUPS = BRANCH_WIDTH // SSM_GROUP
SSM_STATE = 64
SSM_CHUNK = 16
RWKV_W = BRANCH_WIDTH
RWKV_HEAD = 64
RWKV_HEADS = RWKV_W // RWKV_HEAD
RWKV_CHUNK = 64
LORA_W = 128
RWKV_IN = 3 * RWKV_W + 3 * LORA_W
MLA_HEADS = 8
MLA_NOPE = 64
MLA_ROPE = 32
MLA_V = 64
Q_LORA = 384
KV_LORA = 256
MLA_IN = Q_LORA + KV_LORA + MLA_ROPE
MLA_IN_PAD = 768
MLA_SCALE = 1.0 / math.sqrt(MLA_NOPE + MLA_ROPE)
ROPE_BASE = 10000.0
HEAD_PAD = 128
RWKV_OFF = BRANCH_WIDTH
MLA_OFF = RWKV_OFF + RWKV_IN
GATE_OFF = MLA_OFF + MLA_IN
N_EXPERTS = 16
EXPERT_FF = 1536
EC_CAPACITY = 2
VMEM_LIMIT = 56 * 1024 * 1024


def _params(*sem):
    return pltpu.CompilerParams(dimension_semantics=sem, vmem_limit_bytes=VMEM_LIMIT)


def _dot(a, b):
    return jnp.dot(a, b, preferred_element_type=F32)


def _dot_nt(a, b):
    return lax.dot_general(a, b, (((1,), (1,)), ((), ())), preferred_element_type=F32)


def _dot_tn(a, b):
    return lax.dot_general(a, b, (((0,), (0,)), ((), ())), preferred_element_type=F32)


def _split2(x):
    hi = x.astype(BF16)
    lo = (x - hi.astype(F32)).astype(BF16)
    return hi, lo


def _split3(x):
    hi = x.astype(BF16)
    r1 = x - hi.astype(F32)
    mid = r1.astype(BF16)
    lo = (r1 - mid.astype(F32)).astype(BF16)
    return hi, mid, lo


def _dot_exact_rhs(xs, m):
    rows, width = xs[0].shape
    k = m.shape[0]
    nt = width // k
    parts = [p[:, j * k:(j + 1) * k] for x in xs for p in _split2(x) for j in range(nt)]
    res = _dot(jnp.concatenate(parts, axis=0), m)
    blk = lambda i: res[i * rows:(i + 1) * rows]
    return [jnp.concatenate([blk((2 * i) * nt + j) + blk((2 * i + 1) * nt + j) for j in range(nt)], axis=1)
            for i in range(len(xs))]


def _dot_exact_lhs(m, x, parts):
    pieces = _split2(x) if parts == 2 else _split3(x)
    acc = _dot(m, pieces[0])
    for p in pieces[1:]:
        acc = acc + _dot(m, p)
    return acc


def _sigmoid(x):
    return 0.5 * jnp.tanh(0.5 * x) + 0.5


def _norm_mod(x, g, shift, scale):
    y = x * lax.rsqrt(jnp.mean(x * x, axis=-1, keepdims=True) + NORM_EPS)
    return (y * g) * (1.0 + scale) + shift


def _token_tile(ctx_len, seq):
    for tm in (256, 128, 64):
        if ctx_len % tm == 0 and seq % tm == 0:
            return tm
    raise ValueError("context and latent lengths must be multiples of 64")


def _mods_kernel(cc_ref, w_ref, b_ref, o_ref):
    c = cc_ref[...]
    s = (c * _sigmoid(c)).astype(BF16)
    o_ref[0] = _dot(s, w_ref[0].astype(BF16)) + b_ref[0]


def _mods(cc, ada_w, ada_b):
    L, D, N = ada_w.shape
    R = cc.shape[0]
    tn = 1536
    return pl.pallas_call(
        _mods_kernel,
        grid=(L, N // tn),
        in_specs=[pl.BlockSpec((R, D), lambda l, j: (0, 0)),
                  pl.BlockSpec((1, D, tn), lambda l, j: (l, 0, j)),
                  pl.BlockSpec((1, 1, tn), lambda l, j: (l, 0, j))],
        out_specs=pl.BlockSpec((1, R, tn), lambda l, j: (l, 0, j)),
        out_shape=jax.ShapeDtypeStruct((L, R, N), F32),
        compiler_params=_params("parallel", "parallel"),
        name="adaln_mods",
    )(cc, ada_w, ada_b.reshape(L, 1, N))


def _kin_kernel(z_ref, g_ref, mod_ref, ws_ref, wr_ref, wm_ref, os_ref, or_ref, om_ref):
    NB, tm = z_ref.shape[0], z_ref.shape[1]
    W = ws_ref.shape[1]
    h = jnp.concatenate([_norm_mod(z_ref[i], g_ref[...], mod_ref[i, 0][0:1], mod_ref[i, 0][1:2])
                         for i in range(NB)], axis=0).astype(BF16)
    ps, pr, pm = _dot(h, ws_ref[...]), _dot(h, wr_ref[...]), _dot(h, wm_ref[...])
    for i in range(NB):
        os_ref[:, i * W:(i + 1) * W] = ps[i * tm:(i + 1) * tm]
        or_ref[i] = pr[i * tm:(i + 1) * tm]
        om_ref[i] = pm[i * tm:(i + 1) * tm]


def _input_proj(z, g, modsel, ws, wr, wm, ctx_len):
    B, S, D = z.shape
    tm = _token_tile(ctx_len, S - ctx_len)
    nct = ctx_len // tm
    NB = 4
    tok = lambda n: pl.BlockSpec((NB, tm, n), lambda b, i: (b, i, 0))
    full = lambda a: pl.BlockSpec(a.shape, lambda b, i: (0,) * a.ndim)
    return pl.pallas_call(
        _kin_kernel,
        grid=(B // NB, S // tm),
        in_specs=[tok(D), full(g),
                  pl.BlockSpec((NB, 1, N_MOD, D), lambda b, i: (b, jnp.where(i >= nct, 1, 0), 0, 0)),
                  full(ws), full(wr), full(wm)],
        out_specs=[pl.BlockSpec((tm, NB * ws.shape[1]), lambda b, i: (i, b)), tok(wr.shape[1]), tok(wm.shape[1])],
        out_shape=[jax.ShapeDtypeStruct((S, B * ws.shape[1]), F32),
                   jax.ShapeDtypeStruct((B, S, wr.shape[1]), F32),
                   jax.ShapeDtypeStruct((B, S, wm.shape[1]), F32)],
        compiler_params=_params("parallel", "parallel"),
        name="input_proj",
    )(z, g, modsel, ws, wr, wm)


def _cpow(ar, ai, lag, shape, nbits=5):
    pr = jnp.ones(shape, F32)
    pi = jnp.zeros(shape, F32)
    for b in range(nbits):
        bit = ((lag >> b) & 1) == 1
        fr = jnp.where(bit, ar, 1.0)
        fi = jnp.where(bit, ai, 0.0)
        pr, pi = pr * fr - pi * fi, pr * fi + pi * fr
        ar, ai = ar * ar - ai * ai, 2.0 * ar * ai
    return pr, pi


def _ssm_prep_kernel(lc_re_ref, lc_im_ref, lr_re_ref, lr_im_ref, ldt_ref, bt_re_ref, bt_im_ref,
                     ct_re_ref, ct_im_ref, wt_ref, wso_ref, wsi_ref, a_ref):
    C, GC, P = SSM_CHUNK, SSM_GROUP, SSM_STATE
    W = C * GC
    lane = lax.broadcasted_iota(jnp.int32, (1, W), 1)
    quarter = lane // P
    is_re = (quarter == 0) | (quarter == 3)
    jcol = lane // GC
    srow = lax.broadcasted_iota(jnp.int32, (W, 1), 0) // GC
    for d in (0, 1):
        dt = jnp.exp(ldt_ref[d, 0])
        lr, li = lc_re_ref[d, 0], lc_im_ref[d, 0]
        mag = jnp.exp(lr * dt)
        ar, ai = mag * jnp.cos(li * dt), mag * jnp.sin(li * dt)
        cr, ci = ct_re_ref[d, 0], ct_im_ref[d, 0]
        lag_z = jcol if d == 0 else (C - 1) - jcol
        lag_s = jcol + 1 if d == 0 else C - jcol

        def q_of(lag):
            pr, pi = _cpow(ar, ai, lag, (P, W))
            q_re = cr * pr - ci * pi
            q_im = -(cr * pi + ci * pr)
            return q_re, q_im

        qz_re, qz_im = q_of(lag_z)
        qs_re, qs_im = q_of(lag_s)
        lr4, li4 = lr_re_ref[d, 0], lr_im_ref[d, 0]
        mag4 = jnp.exp(lr4 * dt)
        ar4, ai4 = mag4 * jnp.cos(li4 * dt), mag4 * jnp.sin(li4 * dt)
        den = lr4 * lr4 + li4 * li4
        nr, ni = ar4 - 1.0, ai4
        coef_re = (nr * lr4 + ni * li4) / den
        coef_im = (ni * lr4 - nr * li4) / den
        br, bi = bt_re_ref[0], bt_im_ref[0]
        bb_re = coef_re * br - coef_im * bi
        bb_im = coef_re * bi + coef_im * br
        bcat = jnp.where(lane < P, bb_re, bb_im)[:, :2 * P]
        qz = jnp.concatenate([qz_re, qz_im], axis=0)
        z = jnp.dot(bcat, qz, preferred_element_type=F32,
                    precision=lax.Precision.HIGHEST)
        rows = []
        for s in range(C):
            if d == 0:
                sh = pltpu.roll(z, GC * s, axis=1) if s else z
                rows.append(jnp.where(lane >= GC * s, sh, 0.0))
            else:
                m = C - 1 - s
                sh = pltpu.roll(z, W - GC * m, axis=1) if m else z
                rows.append(jnp.where(lane < W - GC * m, sh, 0.0))
        wt_ref[d, 0] = jnp.concatenate(rows, axis=0).astype(BF16)
        wsi_ref[d, 0] = jnp.concatenate([qs_re, qs_im], axis=0).astype(BF16)
        e_row = (C - 1) - srow if d == 0 else srow
        er, ei = _cpow(ar4, ai4, e_row, (W, W), nbits=4)
        bbx = jnp.where(is_re, bb_re, bb_im)
        bby = jnp.where(is_re, -bb_im, bb_re)
        bbx = jnp.concatenate([bbx] * C, axis=0)
        bby = jnp.concatenate([bby] * C, axis=0)
        wso_ref[d, 0] = (er * bbx + ei * bby).astype(BF16)
        cr16, ci16 = _cpow(ar4, ai4, jnp.full((1, W), C, jnp.int32), (1, W))
        a_ref[d, 0, 0:1, :] = cr16
        a_ref[d, 0, 1:2, :] = jnp.where(is_re, -ci16, ci16)


def _ssm_prep(lam_re, lam_im, log_dt, b_re, b_im, c_re, c_im):
    G, P, GC, C = SSM_GROUPS, SSM_STATE, SSM_GROUP, SSM_CHUNK
    W = C * GC
    lc_re, lc_im = lam_re[..., None], lam_im[..., None]
    lr_re = jnp.tile(lam_re, (1, 1, 4))[:, :, None, :]
    lr_im = jnp.tile(lam_im, (1, 1, 4))[:, :, None, :]
    ldt = log_dt[..., None, None]
    bt_re = jnp.tile(jnp.swapaxes(b_re, 1, 2), (1, 1, 4))
    bt_im = jnp.tile(jnp.swapaxes(b_im, 1, 2), (1, 1, 4))
    ct_re = jnp.tile(jnp.swapaxes(c_re, 2, 3), (1, 1, 1, C))
    ct_im = jnp.tile(jnp.swapaxes(c_im, 2, 3), (1, 1, 1, C))
    d4 = lambda a, b: pl.BlockSpec((2, 1, a, b), lambda g: (0, g, 0, 0))
    mat = jax.ShapeDtypeStruct((2, G, W, W), BF16)
    return pl.pallas_call(
        _ssm_prep_kernel,
        grid=(G,),
        in_specs=[d4(P, 1), d4(P, 1), d4(1, W), d4(1, W), d4(1, 1),
                  pl.BlockSpec((1, GC, W), lambda g: (g, 0, 0)),
                  pl.BlockSpec((1, GC, W), lambda g: (g, 0, 0)),
                  d4(P, W), d4(P, W)],
        out_specs=[d4(W, W), d4(W, W), d4(2 * P, W), d4(2, W)],
        out_shape=[mat, mat, jax.ShapeDtypeStruct((2, G, 2 * P, W), BF16),
                   jax.ShapeDtypeStruct((2, G, 2, W), F32)],
        compiler_params=_params("parallel"),
        name="ssm_prep",
    )(lc_re, lc_im, lr_re, lr_im, ldt, bt_re, bt_im, ct_re, ct_im)


def _ssm_kernel(*refs, d, final, KT):
    if final:
        x_ref, ub_ref, yp_ref, wt_ref, wso_ref, wsi_ref, a_ref, dn_ref, o_ref, st_ref = refs
    else:
        x_ref, wt_ref, wso_ref, wsi_ref, a_ref, ub_ref, o_ref, st_ref = refs
    C, GC, P = SSM_CHUNK, SSM_GROUP, SSM_STATE
    BT = x_ref.shape[2]
    GB = x_ref.shape[3] // GC
    R = KT * BT

    @pl.when(pl.program_id(2) == 0)
    def _():
        st_ref[...] = jnp.zeros_like(st_ref)

    gs = range(GB)
    if final:
        ub = [ub_ref[g, 0, 0] for g in gs]
    else:
        xt = [x_ref[:, t].reshape(R, GB * GC) for t in range(C)]
        ub = [jnp.concatenate([xt[t][:, g * GC:(g + 1) * GC] for t in range(C)], axis=1).astype(BF16) for g in gs]
        for g in gs:
            ub_ref[g, 0, 0] = ub[g]
    loc = [_dot(ub[g], wso_ref[0, g]) for g in gs]
    a1 = [a_ref[0, g, 0:1, :] for g in gs]
    a2 = [a_ref[0, g, 1:2, :] for g in gs]
    x = [st_ref[g] for g in gs]
    xs = [[None] * KT for _ in gs]
    for k in (range(KT) if d == 0 else range(KT - 1, -1, -1)):
        for g in gs:
            xs[g][k] = x[g][:, :2 * P]
            x[g] = a1[g] * x[g] + a2[g] * pltpu.roll(x[g], 2 * P, axis=1) + loc[g][k * BT:(k + 1) * BT]
    ys = []
    for g in gs:
        st_ref[g] = x[g]
        xin = jnp.concatenate(xs[g], axis=0).astype(BF16)
        ys.append(_dot(ub[g], wt_ref[0, g]) + _dot(xin, wsi_ref[0, g]))
    if not final:
        for g in gs:
            o_ref[g, 0, 0] = ys[g]
        return
    ys = [ys[g] + yp_ref[g, 0, 0] for g in gs]
    for t in range(C):
        yt = jnp.concatenate([ys[g][:, t * GC:(t + 1) * GC] for g in gs], axis=1)
        o_ref[:, t] = (yt + x_ref[:, t].reshape(R, GB * GC) * dn_ref[...]).reshape(KT, BT, GB * GC)


def _ssm_dir(p_tm, prev, prep, d_skip, d, B, ctx_len):
    S = p_tm.shape[0]
    G, GC, C, P = SSM_GROUPS, SSM_GROUP, SSM_CHUNK, SSM_STATE
    W = C * GC
    LW = 128
    GB = LW // GC
    BT = 16 if B % 16 == 0 else 8
    KT = _token_tile(ctx_len, S - ctx_len) // C
    NT, NTc = S // (C * KT), ctx_len // (C * KT)
    wt, wso, wsi, a4 = prep
    final = prev is not None
    R = KT * BT

    def tile(ti):
        if d == 0:
            return ti
        return jnp.where(ti < NTc, NTc - 1 - ti, NT - 1 - (ti - NTc))

    x4 = p_tm.reshape(S // C, C, B, G * GC)
    xspec = pl.BlockSpec((KT, C, BT, LW), lambda gb, bt, ti: (tile(ti), 0, bt, gb))
    gspec = pl.BlockSpec((GB, 1, 1, R, W), lambda gb, bt, ti: (gb, bt, tile(ti), 0, 0))
    gshape = (G, B // BT, NT, R, W)
    wspec = lambda r: pl.BlockSpec((1, GB, r, W), lambda gb, bt, ti: (d, gb, 0, 0))
    in_specs, args = [xspec], [x4]
    if final:
        in_specs += [gspec, gspec]
        args += list(prev)
    in_specs += [wspec(W), wspec(W), wspec(2 * P), wspec(2)]
    args += [wt, wso, wsi, a4]
    if final:
        in_specs.append(pl.BlockSpec((1, LW), lambda gb, bt, ti: (0, gb)))
        args.append(d_skip.reshape(1, G * GC))
        out_specs, out_shape = xspec, jax.ShapeDtypeStruct(x4.shape, F32)
    else:
        out_specs = [gspec, gspec]
        out_shape = [jax.ShapeDtypeStruct(gshape, BF16), jax.ShapeDtypeStruct(gshape, F32)]
    y = pl.pallas_call(
        functools.partial(_ssm_kernel, d=d, final=final, KT=KT),
        grid=(G // GB, B // BT, NT),
        in_specs=in_specs,
        out_specs=out_specs,
        out_shape=out_shape,
        scratch_shapes=[pltpu.VMEM((GB, BT, W), F32)],
        compiler_params=_params("parallel", "parallel", "arbitrary"),
        name="ssm_rev" if d else "ssm_fwd",
    )(*args)
    return y.reshape(p_tm.shape) if final else y


def _ssm_scan(p_tm, prep, d_skip, B, ctx_len):
    fwd = _ssm_dir(p_tm, None, prep, d_skip, 0, B, ctx_len)
    return _ssm_dir(p_tm, fwd, prep, d_skip, 1, B, ctx_len)


def _rwkv_kernel(*refs, d, final, NC, NCc):
    if final:
        (p_ref, hp_ref, hn_ref, yp_ref, mu_ref, w0_ref, w2_ref, a0_ref, a2_ref, g2_ref, pv_ref,
         o_ref, s_ref, xb_ref, yb_ref) = refs
    else:
        (p_ref, hp_ref, hn_ref, mu_ref, w0_ref, w2_ref, a0_ref, a2_ref, g2_ref, pv_ref,
         o_ref, s_ref, xb_ref, yb_ref) = refs
    L, N, H, W = RWKV_CHUNK, RWKV_HEAD, RWKV_HEADS, RWKV_W
    NB = p_ref.shape[0]
    ci = pl.program_id(1)
    if d == 0:
        c = ci
    else:
        c = jnp.where(ci < NCc, NCc - 1 - ci, NC - 1 - (ci - NCc))

    @pl.when(ci == 0)
    def _():
        s_ref[...] = jnp.zeros_like(s_ref)

    keep_prev = jnp.where((c == 0) | (c == NCc), 0.0, 1.0)
    keep_next = jnp.where((c == NCc - 1) | (c == NC - 1), 0.0, 1.0)
    xs = []
    for bi in range(NB):
        p = p_ref[bi]
        xb_ref[bi, 0:8, :] = hp_ref[bi] * keep_prev
        xb_ref[bi, 8:8 + L, :] = p
        xb_ref[bi, 8 + L:16 + L, :] = hn_ref[bi] * keep_next
        prev = xb_ref[bi, 7:7 + L, :]
        nxt = xb_ref[bi, 9:9 + L, :]
        xs.append(p + mu_ref[...] * (0.5 * (prev + nxt) - p))
    x = jnp.concatenate(xs, axis=0)

    r, k, v = x[:, 0:W], x[:, W:2 * W], x[:, 2 * W:3 * W]
    pw = x[:, 3 * W:3 * W + LORA_W]
    pa = x[:, 3 * W + LORA_W:3 * W + 2 * LORA_W]
    pg = x[:, 3 * W + 2 * LORA_W:3 * W + 3 * LORA_W]
    k_k, k_a, r_k = pv_ref[0:1, :], pv_ref[1:2, :], pv_ref[2:3, :]
    ln_w, ln_b = pv_ref[3:4, :], pv_ref[4:5, :]
    pab = pa.astype(BF16)

    zw = w0_ref[d:d + 1, :] + _dot(jnp.tanh(pw).astype(BF16), w2_ref[d])
    nz = -zw
    softplus = jnp.maximum(nz, 0.0) + jnp.log(1.0 + jnp.exp(-jnp.abs(nz)))
    lw = -jnp.exp(-softplus - 0.5)
    a = _sigmoid(a0_ref[d:d + 1, :] + _dot(pab, a2_ref[d]))
    kd = k * (1.0 + (a - 1.0) * k_a)

    hrow = lax.broadcasted_iota(jnp.int32, (2 * N, 2 * N), 0) // N
    hcol = lax.broadcasted_iota(jnp.int32, (2 * N, 2 * N), 1) // N
    head_ones = jnp.where(hrow == hcol, 1.0, 0.0).astype(BF16)

    kk = k * k_k
    kk = kk * lax.rsqrt(_dot_exact_rhs([kk * kk], head_ones)[0] + 1e-12)
    b = a * kk

    PW2 = 2 * N
    trow = lax.broadcasted_iota(jnp.int32, (L, PW2), 0)
    lane2 = lax.broadcasted_iota(jnp.int32, (L, PW2), 1)
    left = lane2 < N
    tcol = jnp.where(left, lane2, lane2 - N)
    if d == 0:
        strict, incl = tcol < trow, tcol <= trow
    else:
        strict, incl = tcol > trow, tcol >= trow

    def bdiag(t):
        return jnp.concatenate([jnp.where(left, t, 0.0), jnp.where(left, 0.0, t)], axis=0).astype(BF16)

    brow = lax.broadcasted_iota(jnp.int32, (NB * L, NB * L), 0)
    bcol = lax.broadcasted_iota(jnp.int32, (NB * L, NB * L), 1)
    upto = (bcol <= brow) if d == 0 else (bcol >= brow)
    cum = jnp.where(upto, jnp.where((brow // L) == (bcol // L), 1.0, 0.0), 0.0).astype(BF16)
    cs = _dot_exact_lhs(cum, lw, 3)
    last = L - 1 if d == 0 else 0
    cls = [cs[bi * L + last:bi * L + last + 1, :] for bi in range(NB)]
    cl = jnp.concatenate([jnp.broadcast_to(t, (L, W)) for t in cls], axis=0)
    e_to_end = jnp.exp(cl - cs)
    e_neg = jnp.exp(-cs)
    rt = (r * jnp.exp(cs)).astype(BF16)
    at = (kk * jnp.exp(cs - lw)).astype(BF16)
    bt = b * e_neg
    kt = kd * e_neg
    kh = (kd * e_to_end).astype(BF16)
    bh = (b * e_to_end).astype(BF16)
    e_chunk = [jnp.exp(t) for t in cls]
    vb = v.astype(BF16)

    ids = [(bi, j) for bi in range(NB) for j in range(H // 2)]
    n = range(len(ids))
    rs = [slice(bi * L, (bi + 1) * L) for bi, _ in ids]
    ls = [slice(j * PW2, (j + 1) * PW2) for _, j in ids]
    ar = [jnp.concatenate([at[rs[i], ls[i]], rt[rs[i], ls[i]]], axis=0) for i in n]
    s0 = [s_ref[bi, j] for bi, j in ids]
    g_b = [_dot_nt(ar[i], bdiag(bt[rs[i], ls[i]])) for i in n]
    g_k = [_dot_nt(ar[i], bdiag(kt[rs[i], ls[i]])) for i in n]
    g_s = [_dot_nt(ar[i], bdiag(s0[i])) for i in n]
    nab = [jnp.where(strict, g_b[i][:L], 0.0) for i in n]
    mrb = [jnp.where(incl, g_b[i][L:], 0.0).astype(BF16) for i in n]
    nm = [jnp.concatenate([jnp.where(strict, g_k[i][:L], 0.0), jnp.where(incl, g_k[i][L:], 0.0)],
                          axis=0).astype(BF16) for i in n]
    nv = [g_s[i] + _dot(nm[i], bdiag(v[rs[i], ls[i]])) for i in n]
    pm = [-t for t in nab]
    q = [_dot(t.astype(BF16), bdiag(t)) for t in nab]
    steps = int(math.log2(L)) - 1
    for it in range(steps):
        qd = [bdiag(t) for t in q]
        if it + 1 < steps:
            pq = [_dot(jnp.concatenate([pm[i], q[i]], axis=0).astype(BF16), qd[i]) for i in n]
            pm = [pm[i] + q[i] + pq[i][:L] for i in n]
            q = [t[L:] for t in pq]
        else:
            pm = [pm[i] + q[i] + _dot(pm[i].astype(BF16), qd[i]) for i in n]
    u = [nv[i][:L] + _dot(pm[i].astype(BF16), bdiag(nv[i][:L])) for i in n]
    for i in n:
        yb_ref[rs[i], ls[i]] = nv[i][L:] - _dot(mrb[i], bdiag(u[i]))
    for i, (bi, j) in enumerate(ids):
        vu = jnp.concatenate([vb[rs[i], ls[i]], -u[i].astype(BF16)], axis=0)
        kb = jnp.concatenate([kh[rs[i], ls[i]], bh[rs[i], ls[i]]], axis=0)
        full = _dot_tn(vu, kb)
        s_ref[bi, j] = s0[i] * e_chunk[bi][:, ls[i]] + jnp.where(left, full[:L], full[L:])

    if not final:
        o_ref[...] = yb_ref[...].reshape(NB, L, W)
        return
    y = yp_ref[...].reshape(NB * L, W) + yb_ref[...]
    o = 1 - d
    a_o = _sigmoid(a0_ref[o:o + 1, :] + _dot(pab, a2_ref[o]))
    kd_sum = kd + k * (1.0 + (a_o - 1.0) * k_a)
    y_sum, rk_sum = _dot_exact_rhs([y, r * kd_sum * r_k], head_ones)
    dev = y - y_sum * (1.0 / N)
    var = _dot_exact_rhs([dev * dev], head_ones)[0] * (1.0 / N)
    yn = dev * lax.rsqrt(var + GN_EPS) * ln_w + ln_b
    bonus = rk_sum * v
    g = _dot(_sigmoid(pg).astype(BF16), g2_ref[...])
    o_ref[...] = ((yn + bonus) * g).reshape(NB, L, W)


def _rwkv_dir(p_rwkv, y_prev, params, d, ctx_len):
    B, S, PW = p_rwkv.shape
    L, W = RWKV_CHUNK, RWKV_W
    NC, NCc = S // L, ctx_len // L
    NB = 4
    final = y_prev is not None

    def chunk(ci):
        if d == 0:
            return ci
        return jnp.where(ci < NCc, NCc - 1 - ci, NC - 1 - (ci - NCc))

    hb = L // 8
    tok = lambda n: pl.BlockSpec((NB, L, n), lambda b, ci: (b, chunk(ci), 0))
    full = lambda a: pl.BlockSpec(a.shape, lambda b, ci: (0,) * a.ndim)
    in_specs = [tok(PW),
                pl.BlockSpec((NB, 8, PW), lambda b, ci: (b, jnp.maximum(chunk(ci) * hb - 1, 0), 0)),
                pl.BlockSpec((NB, 8, PW), lambda b, ci: (b, jnp.minimum((chunk(ci) + 1) * hb, S // 8 - 1), 0))]
    args = [p_rwkv, p_rwkv, p_rwkv]
    if final:
        in_specs.append(tok(W))
        args.append(y_prev)
    in_specs += [full(a) for a in params]
    args += list(params)
    return pl.pallas_call(
        functools.partial(_rwkv_kernel, d=d, final=final, NC=NC, NCc=NCc),
        grid=(B // NB, NC),
        in_specs=in_specs,
        out_specs=tok(W),
        out_shape=jax.ShapeDtypeStruct((B, S, W), F32),
        scratch_shapes=[pltpu.VMEM((NB, RWKV_HEADS // 2, RWKV_HEAD, 2 * RWKV_HEAD), F32),
                        pltpu.VMEM((NB, L + 16, PW), F32),
                        pltpu.VMEM((NB * L, W), F32)],
        compiler_params=_params("parallel", "arbitrary"),
        name="rwkv_rev" if d else "rwkv_fwd",
    )(*args)


def _rwkv_params(mu, w0, w2, a0, a2, g2, k_k, k_a, r_k, ln_w, ln_b):
    W = RWKV_W
    half = LORA_W // 2

    def pad_dir(w):
        out = jnp.zeros((2, LORA_W, W), F32)
        out = out.at[0, :half].set(w[0]).at[1, half:].set(w[1])
        return out.astype(BF16)

    pv = jnp.zeros((8, W), F32)
    pv = pv.at[0].set(k_k).at[1].set(k_a).at[2].set(r_k.reshape(W)).at[3].set(ln_w).at[4].set(ln_b)
    return (mu.reshape(1, RWKV_IN), w0, pad_dir(w2), a0, pad_dir(a2), g2.astype(BF16), pv)


def _mla_prep_kernel(p_ref, cq_ref, sq_ref, ck_ref, sk_ref, qn_ref, kvn_ref, wq_ref, wk_ref, wv_ref,
                     gn_ref, q_ref, k_ref, v_ref):
    HP = HEAD_PAD
    p = p_ref[0]
    lane = lax.broadcasted_iota(jnp.int32, (1, HP), 1)
    m_nope = jnp.where(lane < MLA_NOPE, 1.0, 0.0)
    m_rope = jnp.where(lane < MLA_NOPE, 0.0, jnp.where(lane < MLA_NOPE + MLA_ROPE, 1.0, 0.0))
    g_q, g_kn, g_kr = gn_ref[0:1, :], gn_ref[1:2, :], gn_ref[2:3, :]

    def rms(x, n):
        return lax.rsqrt(jnp.sum(x * x, axis=-1, keepdims=True) * (1.0 / n) + NORM_EPS)

    xq = p[:, :Q_LORA]
    cq = (xq * rms(xq, Q_LORA) * qn_ref[...]).astype(BF16)
    xkv = p[:, Q_LORA:Q_LORA + KV_LORA]
    ckv = (xkv * rms(xkv, KV_LORA) * kvn_ref[...]).astype(BF16)
    q = _dot(cq, wq_ref[...])
    kn = _dot(ckv, wk_ref[...])
    v_ref[0] = _dot_nt(wv_ref[...], ckv).astype(BF16)

    kr = p[:, Q_LORA + KV_LORA:Q_LORA + KV_LORA + HP]
    krn = kr * rms(kr, MLA_ROPE) * g_kr
    to_rope = pltpu.roll(krn, MLA_NOPE, axis=1)
    sw_a = jnp.where(lane >= 112, 0.0, jnp.where(lane >= 96, pltpu.roll(krn, 80, axis=1), 0.0))
    sw_b = jnp.where(lane >= 112, pltpu.roll(krn, 112, axis=1), 0.0)
    kext = to_rope + sw_a + sw_b
    krot = kext * ck_ref[...] + pltpu.roll(kext, HP - MLA_ROPE, axis=1) * sk_ref[...]

    cq_t, sq_t = cq_ref[...], sq_ref[...]
    for h in range(MLA_HEADS):
        sl = slice(h * HP, (h + 1) * HP)
        qh = q[:, sl]
        scale = m_nope * rms(qh * m_nope, MLA_NOPE) + (1.0 - m_nope) * rms(qh * m_rope, MLA_ROPE)
        qn = qh * scale * g_q
        q_ref[0, :, sl] = (qn * cq_t + pltpu.roll(qn, HP - MLA_ROPE, axis=1) * sq_t).astype(BF16)
        kh = kn[:, sl]
        k_ref[0, :, sl] = (kh * rms(kh, MLA_NOPE) * g_kn + krot).astype(BF16)


def _mla_prep(p_mla, tables, params, ctx_len):
    B, S, PW = p_mla.shape
    tm = _token_tile(ctx_len, S - ctx_len)
    cq_t, sq_t, ck_t, sk_t = tables
    HW = MLA_HEADS * HEAD_PAD
    tok = lambda n: pl.BlockSpec((1, tm, n), lambda b, i: (b, i, 0))
    tab = pl.BlockSpec((tm, HEAD_PAD), lambda b, i: (i, 0))
    full = lambda a: pl.BlockSpec(a.shape, lambda b, i: (0,) * a.ndim)
    return pl.pallas_call(
        _mla_prep_kernel,
        grid=(B, S // tm),
        in_specs=[tok(PW), tab, tab, tab, tab] + [full(a) for a in params],
        out_specs=[tok(HW), tok(HW), pl.BlockSpec((1, MLA_HEADS * MLA_V, tm), lambda b, i: (b, 0, i))],
        out_shape=[jax.ShapeDtypeStruct((B, S, HW), BF16), jax.ShapeDtypeStruct((B, S, HW), BF16),
                   jax.ShapeDtypeStruct((B, MLA_HEADS * MLA_V, S), BF16)],
        compiler_params=_params("parallel", "parallel"),
        name="mla_prep",
    )(p_mla, cq_t, sq_t, ck_t, sk_t, *params)


def _mla_params(q_norm, kv_norm, w_uq, w_ukv, qn_nope, kn_nope, qn_rope, kn_rope):
    H, NP, RP, HP = MLA_HEADS, MLA_NOPE, MLA_ROPE, HEAD_PAD
    half = RP // 2
    swap = jnp.concatenate([jnp.arange(half, RP), jnp.arange(0, half)])
    wq = w_uq.reshape(Q_LORA, H, NP + RP)
    wq = jnp.concatenate([wq, wq[:, :, NP + swap]], axis=-1).reshape(Q_LORA, H * HP)
    wkv = w_ukv.reshape(KV_LORA, H, NP + MLA_V)
    wk = jnp.concatenate([wkv[:, :, :NP], jnp.zeros((KV_LORA, H, HP - NP), F32)], axis=-1)
    wk = wk.reshape(KV_LORA, H * HP)
    wv = wkv[:, :, NP:].reshape(KV_LORA, H * MLA_V)
    gn = jnp.zeros((8, HP), F32)
    gn = gn.at[0].set(jnp.concatenate([qn_nope, qn_rope, qn_rope[swap]]))
    gn = gn.at[1, :NP].set(kn_nope).at[2, :RP].set(kn_rope)
    return (q_norm.reshape(1, Q_LORA), kv_norm.reshape(1, KV_LORA),
            wq.astype(BF16), wk.astype(BF16), wv.T.astype(BF16), gn)


def _rope_tables(ctx_len, seq):
    rows = seq // GRID_W
    axis_dims = MLA_ROPE // 2
    row = jnp.repeat(jnp.arange(rows), GRID_W).astype(F32)
    col = jnp.tile(jnp.arange(GRID_W), rows).astype(F32)
    inv = ROPE_BASE ** (-jnp.arange(0, axis_dims, 2, dtype=F32) / axis_dims)
    ang = jnp.concatenate([row[:, None] * inv, col[:, None] * inv], axis=-1)
    cos = jnp.concatenate([jnp.ones((ctx_len, axis_dims), F32), jnp.cos(ang)], axis=0)
    sin = jnp.concatenate([jnp.zeros((ctx_len, axis_dims), F32), jnp.sin(ang)], axis=0)
    S = ctx_len + seq
    pad = jnp.zeros((S, HEAD_PAD - MLA_NOPE - MLA_ROPE), F32)
    cos_t = jnp.concatenate([jnp.ones((S, MLA_NOPE), F32), cos, cos, pad], axis=-1)
    sin_t = jnp.concatenate([jnp.zeros((S, MLA_NOPE), F32), -sin, sin, pad], axis=-1)
    qs = MLA_SCALE * math.log2(math.e)
    return cos_t * qs, sin_t * qs, cos_t, sin_t


def _attn_kernel(q_ref, k_ref, vt_ref, o_ref, *, nct, ctx_len):
    S = k_ref.shape[1]
    i = pl.program_id(2)

    def attend(nk):
        hs = range(q_ref.shape[2] // HEAD_PAD)
        s = [_dot_nt(k_ref[0, :nk, h * HEAD_PAD:(h + 1) * HEAD_PAD],
                     q_ref[0, :, h * HEAD_PAD:(h + 1) * HEAD_PAD]) for h in hs]
        m = [jnp.max(t, axis=0, keepdims=True) for t in s]
        e = [jnp.exp2(s[h] - m[h]) for h in hs]
        l = [jnp.sum(t, axis=0, keepdims=True) for t in e]
        o = [_dot(vt_ref[0, h * MLA_V:(h + 1) * MLA_V, :nk], e[h].astype(BF16)) / l[h] for h in hs]
        for j in range(len(hs) // 2):
            pair = jnp.concatenate([o[2 * j], o[2 * j + 1]], axis=0)
            o_ref[0, :, j * 2 * MLA_V:(j + 1) * 2 * MLA_V] = pair.T.astype(BF16)

    @pl.when(i < nct)
    def _():
        attend(ctx_len)

    @pl.when(i >= nct)
    def _():
        attend(S)


def _attention(q, k, v, ctx_len):
    B, S, _ = q.shape
    tq = _token_tile(ctx_len, S - ctx_len)
    HS = 8
    return pl.pallas_call(
        functools.partial(_attn_kernel, nct=ctx_len // tq, ctx_len=ctx_len),
        grid=(B, MLA_HEADS // HS, S // tq),
        in_specs=[pl.BlockSpec((1, tq, HS * HEAD_PAD), lambda b, h, i: (b, i, h)),
                  pl.BlockSpec((1, S, HS * HEAD_PAD), lambda b, h, i: (b, 0, h)),
                  pl.BlockSpec((1, HS * MLA_V, S), lambda b, h, i: (b, h, 0))],
        out_specs=pl.BlockSpec((1, tq, HS * MLA_V), lambda b, h, i: (b, i, h)),
        out_shape=jax.ShapeDtypeStruct((B, S, MLA_HEADS * MLA_V), BF16),
        compiler_params=_params("parallel", "parallel", "parallel"),
        name="mla_attention",
    )(q, k, v)


def _merge_kernel(z_ref, g1_ref, g2_ref, mod_ref, ys_ref, yr_ref, ym_ref, wg_ref, wb_ref, wo_ref,
                  glw_ref, glb_ref, rw_ref, zo_ref, h2_ref, lg_ref):
    D, W = D_MODEL, BRANCH_WIDTH
    NB, tm = z_ref.shape[0], z_ref.shape[1]
    ns = range(NB)
    rows = lambda t: jnp.concatenate(t, axis=0)
    z = [z_ref[i] for i in ns]
    m = [mod_ref[i, 0] for i in ns]
    h = rows([_norm_mod(z[i], g1_ref[...], m[i][0:1], m[i][1:2]) for i in ns]).astype(BF16)
    ys = rows([ys_ref[:, i * W:(i + 1) * W] for i in ns])
    ys = 0.5 * ys * (1.0 + jnp.tanh(math.sqrt(2.0 / math.pi) * (ys + 0.044715 * ys * ys * ys)))
    ys = ys * _sigmoid(_dot(ys.astype(BF16), glw_ref[...]) + glb_ref[...])
    branches = (ys.astype(BF16), rows([yr_ref[i] for i in ns]).astype(BF16), rows([ym_ref[i] for i in ns]))
    acc = jnp.zeros((NB * tm, D), F32)
    for j, yj in enumerate(branches):
        gate = _sigmoid(_dot(h, wg_ref[:, j * D:(j + 1) * D]))
        acc = acc + gate * _dot(yj, wb_ref[j])
    mix = _dot(acc.astype(BF16), wo_ref[...])
    zn = [z[i] + m[i][2:3] * mix[i * tm:(i + 1) * tm] for i in ns]
    h2 = rows([_norm_mod(zn[i], g2_ref[...], m[i][3:4], m[i][4:5]) for i in ns])
    E = rw_ref.shape[0]
    rw_rows = jnp.concatenate(_split3(rw_ref[...]), axis=0)
    hp = _split3(h2)
    t0 = _dot_nt(rw_rows, hp[0])
    t1 = _dot_nt(rw_rows[:2 * E], hp[1])
    t2 = _dot_nt(rw_rows[:E], hp[2])
    lg = t0[:E] + t0[E:2 * E] + t0[2 * E:] + t1[:E] + t1[E:] + t2
    for i in ns:
        zo_ref[i] = zn[i]
        h2_ref[i] = hp[0][i * tm:(i + 1) * tm]
        lg_ref[i] = lg[:, i * tm:(i + 1) * tm]


def _merge(z, g1, g2, modsel, ys, yr, ym, wg, wb, wo, glw, glb, rwt, ctx_len):
    B, S, D = z.shape
    tm = _token_tile(ctx_len, S - ctx_len)
    nct = ctx_len // tm
    E = rwt.shape[0]
    NB = 2
    tok = lambda n: pl.BlockSpec((NB, tm, n), lambda b, i: (b, i, 0))
    full = lambda a: pl.BlockSpec(a.shape, lambda b, i: (0,) * a.ndim)
    W = BRANCH_WIDTH
    return pl.pallas_call(
        _merge_kernel,
        grid=(B // NB, S // tm),
        in_specs=[tok(D), full(g1), full(g2),
                  pl.BlockSpec((NB, 1, N_MOD, D), lambda b, i: (b, jnp.where(i >= nct, 1, 0), 0, 0)),
                  pl.BlockSpec((tm, NB * W), lambda b, i: (i, b)),
                  tok(W), tok(W), full(wg), full(wb), full(wo), full(glw), full(glb), full(rwt)],
        out_specs=[tok(D), tok(D), pl.BlockSpec((NB, E, tm), lambda b, i: (b, 0, i))],
        out_shape=[jax.ShapeDtypeStruct((B, S, D), F32), jax.ShapeDtypeStruct((B, S, D), BF16),
                   jax.ShapeDtypeStruct((B, E, S), F32)],
        compiler_params=_params("parallel", "parallel"),
        name="merge",
    )(z, g1, g2, modsel, ys, yr, ym, wg, wb, wo, glw, glb, rwt)


def _route_kernel(lg_ref, slot_ref, gate_ref, *, ctx_len, cap_c, cap_l):
    lg = lg_ref[0]
    E, S = lg.shape
    e = jnp.exp(lg - jnp.max(lg, axis=0, keepdims=True))
    aff = e / jnp.sum(e, axis=0, keepdims=True)
    bits = pltpu.bitcast(aff, jnp.int32)
    lane = lax.broadcasted_iota(jnp.int32, (1, S), 1)
    in_c = lane < ctx_len
    ms_c = jnp.where(in_c, 1.0, 0.0)
    LT = 128
    tr = lax.broadcasted_iota(jnp.int32, (LT, LT), 0)
    tc = lax.broadcasted_iota(jnp.int32, (LT, LT), 1)
    tri = jnp.where(tr <= tc, 1.0, 0.0).astype(BF16)

    def counts(x):
        n_c = jnp.sum(x * ms_c, axis=1, keepdims=True)
        return n_c, jnp.sum(x, axis=1, keepdims=True) - n_c

    def prefix(x):
        tiles = [x[:, t * LT:(t + 1) * LT] for t in range(S // LT)]
        inc = _dot(jnp.concatenate(tiles, axis=0).astype(BF16), tri)
        off = jnp.zeros((E, 1), F32)
        out = []
        for t, xt in enumerate(tiles):
            it = inc[t * E:(t + 1) * E]
            out.append(it - xt + off)
            off = off + it[:, LT - 1:LT]
        n_c, _ = counts(x)
        return jnp.concatenate(out, axis=1) - jnp.where(in_c, 0.0, n_c)

    def body(_, carry):
        lo_c, hi_c, lo_l, hi_l = carry
        mid_c = lo_c + ((hi_c - lo_c + 1) >> 1)
        mid_l = lo_l + ((hi_l - lo_l + 1) >> 1)
        n_c, n_l = counts(jnp.where(bits >= jnp.where(in_c, mid_c, mid_l), 1.0, 0.0))
        ok_c, ok_l = n_c >= cap_c, n_l >= cap_l
        return (jnp.where(ok_c, mid_c, lo_c), jnp.where(ok_c, hi_c, mid_c - 1),
                jnp.where(ok_l, mid_l, lo_l), jnp.where(ok_l, hi_l, mid_l - 1))

    zero = jnp.zeros((E, 1), jnp.int32)
    top = jnp.full((E, 1), 0x7F800000, jnp.int32)
    lo_c, _, lo_l, _ = lax.fori_loop(0, 31, body, (zero, top, zero, top))
    thr = jnp.where(in_c, lo_c, lo_l)
    gt = jnp.where(bits > thr, 1.0, 0.0)
    eq = jnp.where(bits == thr, 1.0, 0.0)
    g_c, g_l = counts(gt)
    need = jnp.where(in_c, cap_c - g_c, cap_l - g_l)
    sel = gt + eq * jnp.where(prefix(eq) < need, 1.0, 0.0)
    rank = prefix(sel) + jnp.where(in_c, 0.0, float(cap_c))
    slot_ref[0] = jnp.where(sel > 0.0, rank, -1.0)
    gate_ref[0] = aff * sel


def _route(logits_t, ctx_len, cap_c, cap_l):
    B, E, S = logits_t.shape
    assert S % 128 == 0
    spec = pl.BlockSpec((1, E, S), lambda b: (b, 0, 0))
    return pl.pallas_call(
        functools.partial(_route_kernel, ctx_len=ctx_len, cap_c=cap_c, cap_l=cap_l),
        grid=(B,),
        in_specs=[spec],
        out_specs=[spec, spec],
        out_shape=[jax.ShapeDtypeStruct((B, E, S), F32)] * 2,
        compiler_params=_params("parallel"),
        name="route",
    )(logits_t)


def _expert_kernel(h_ref, slot_ref, gate_ref, w1_ref, w3_ref, w2_ref, o_ref, *, cap, cap_c, ctx_len):
    NB, S = h_ref.shape[0], h_ref.shape[1]
    xs, gc = [], []
    for i in range(NB):
        for (c0, c1), (t0, t1) in (((0, cap_c), (0, ctx_len)), ((cap_c, cap), (ctx_len, S))):
            slot = slot_ref[i, 0, :, t0:t1]
            gate = gate_ref[i, 0, :, t0:t1]
            cidx = lax.broadcasted_iota(jnp.int32, (c1 - c0, t1 - t0), 0).astype(F32) + float(c0)
            hit = slot == cidx
            onehot = jnp.where(hit, 1.0, 0.0).astype(BF16)
            xs.append(_dot(onehot, h_ref[i, t0:t1, :]))
            gc.append(jnp.sum(jnp.where(hit, gate, 0.0), axis=1, keepdims=True))
    xs = jnp.concatenate(xs, axis=0).astype(BF16)
    gc = jnp.concatenate(gc, axis=0)
    a1 = _dot(xs, w1_ref[0])
    a3 = _dot(xs, w3_ref[0])
    hid = (a1 * _sigmoid(a1) * a3).astype(BF16)
    y = (_dot(hid, w2_ref[0]) * gc).astype(BF16)
    for i in range(NB):
        o_ref[i, 0] = y[i * cap:(i + 1) * cap]


def _experts(h2, slot, gate, w1, w3, w2, cap, cap_c, ctx_len):
    B, S, D = h2.shape
    E, _, F = w1.shape
    NB = 2
    row = pl.BlockSpec((NB, 1, 1, S), lambda e, b: (b, e, 0, 0))
    return pl.pallas_call(
        functools.partial(_expert_kernel, cap=cap, cap_c=cap_c, ctx_len=ctx_len),
        grid=(E, B // NB),
        in_specs=[pl.BlockSpec((NB, S, D), lambda e, b: (b, 0, 0)), row, row,
                  pl.BlockSpec((1, D, F), lambda e, b: (e, 0, 0)),
                  pl.BlockSpec((1, D, F), lambda e, b: (e, 0, 0)),
                  pl.BlockSpec((1, F, D), lambda e, b: (e, 0, 0))],
        out_specs=pl.BlockSpec((NB, 1, cap, D), lambda e, b: (b, e, 0, 0)),
        out_shape=jax.ShapeDtypeStruct((B, E, cap, D), BF16),
        compiler_params=_params("parallel", "parallel"),
        name="experts",
    )(h2, slot.reshape(B, E, 1, S), gate.reshape(B, E, 1, S), w1, w3, w2)


def _scatter_kernel(z_ref, mod_ref, st_ref, y_ref, o_ref, *, cap_c, cap, nct):
    NB, tm, E = st_ref.shape

    def add(lo, hi):
        cidx = lax.broadcasted_iota(jnp.int32, (tm, hi - lo), 1).astype(F32) + float(lo)
        st = [st_ref[i] for i in range(NB)]
        acc = [jnp.zeros((tm, z_ref.shape[2]), F32) for _ in range(NB)]
        for e in range(E):
            onehot = [jnp.where(st[i][:, e:e + 1] == cidx, 1.0, 0.0).astype(BF16) for i in range(NB)]
            acc = [acc[i] + _dot(onehot[i], y_ref[i, e, lo:hi, :]) for i in range(NB)]
        for i in range(NB):
            o_ref[i] = z_ref[i] + mod_ref[i, 0][5:6] * acc[i]

    @pl.when(pl.program_id(1) < nct)
    def _():
        add(0, cap_c)

    @pl.when(pl.program_id(1) >= nct)
    def _():
        add(cap_c, cap)


def _scatter(z, modsel, slot_t, yc, ctx_len, cap_c):
    B, S, D = z.shape
    _, E, cap, _ = yc.shape
    tm = _token_tile(ctx_len, S - ctx_len)
    nct = ctx_len // tm
    NB = 1
    tok = lambda n: pl.BlockSpec((NB, tm, n), lambda b, i: (b, i, 0))
    return pl.pallas_call(
        functools.partial(_scatter_kernel, cap_c=cap_c, cap=cap, nct=nct),
        grid=(B // NB, S // tm),
        in_specs=[tok(D),
                  pl.BlockSpec((NB, 1, N_MOD, D), lambda b, i: (b, jnp.where(i >= nct, 1, 0), 0, 0)),
                  tok(E),
                  pl.BlockSpec((NB, E, cap, D), lambda b, i: (b, 0, 0, 0))],
        out_specs=tok(D),
        out_shape=jax.ShapeDtypeStruct((B, S, D), F32),
        compiler_params=_params("parallel", "parallel"),
        name="moe_scatter",
    )(z, modsel, slot_t, yc)


def kernel(x, c, ctx, c_ctx, ada_w, ada_b, norm1_g, norm2_g, w_in, ssm_lambda_re, ssm_lambda_im, ssm_log_dt, ssm_b_re, ssm_b_im, ssm_c_re, ssm_c_im, ssm_d, ssm_glu_w, ssm_glu_b, rwkv_mu, rwkv_w0, rwkv_w2, rwkv_a0, rwkv_a2, rwkv_g2, rwkv_k_k, rwkv_k_a, rwkv_r_k, rwkv_ln_w, rwkv_ln_b, mla_q_norm, mla_kv_norm, mla_w_uq, mla_w_ukv, mla_qn_nope, mla_kn_nope, mla_qn_rope, mla_kn_rope, w_branch, w_out, router_w, moe_w1, moe_w3, moe_w2):
    B, T, D = x.shape
    CTX = ctx.shape[1]
    S = CTX + T
    depth = ada_w.shape[0]
    cap_c = EC_CAPACITY * CTX // N_EXPERTS
    cap_l = EC_CAPACITY * T // N_EXPERTS
    assert D == D_MODEL and T % GRID_W == 0 and cap_c % 8 == 0 and cap_l % 8 == 0 and B % 8 == 0

    rows = -(-(B + 1) // 8) * 8
    cc = jnp.concatenate([c, c_ctx[None, :], jnp.zeros((rows - B - 1, D), F32)], axis=0)
    mods = _mods(cc, ada_w, ada_b).reshape(depth, rows, N_MOD, D)
    mod_lat = mods[:, :B]
    mod_ctx = jnp.broadcast_to(mods[:, B:B + 1], mod_lat.shape)
    modsel = jnp.stack([mod_ctx, mod_lat], axis=2)

    tables = _rope_tables(CTX, T)

    z = jnp.concatenate([ctx, x], axis=1)
    for l in range(depth):
        g1, g2 = norm1_g[l].reshape(1, D), norm2_g[l].reshape(1, D)
        wl = w_in[l]
        w_ssm = wl[:, :RWKV_OFF].astype(BF16)
        w_rwkv = wl[:, RWKV_OFF:MLA_OFF].astype(BF16)
        w_mla = jnp.pad(wl[:, MLA_OFF:GATE_OFF], ((0, 0), (0, MLA_IN_PAD - MLA_IN))).astype(BF16)
        w_gate = wl[:, GATE_OFF:].astype(BF16)
        p_ssm, p_rwkv, p_mla = _input_proj(z, g1, modsel[l], w_ssm, w_rwkv, w_mla, CTX)

        prep = _ssm_prep(ssm_lambda_re[l], ssm_lambda_im[l], ssm_log_dt[l], ssm_b_re[l], ssm_b_im[l],
                         ssm_c_re[l], ssm_c_im[l])
        y_ssm = _ssm_scan(p_ssm, prep, ssm_d[l], B, CTX)

        rp = _rwkv_params(rwkv_mu[l], rwkv_w0[l], rwkv_w2[l], rwkv_a0[l], rwkv_a2[l], rwkv_g2[l],
                          rwkv_k_k[l], rwkv_k_a[l], rwkv_r_k[l], rwkv_ln_w[l], rwkv_ln_b[l])
        y_fwd = _rwkv_dir(p_rwkv, None, rp, 0, CTX)
        y_rwkv = _rwkv_dir(p_rwkv, y_fwd, rp, 1, CTX)

        mp = _mla_params(mla_q_norm[l], mla_kv_norm[l], mla_w_uq[l], mla_w_ukv[l], mla_qn_nope[l],
                         mla_kn_nope[l], mla_qn_rope[l], mla_kn_rope[l])
        q, k, v = _mla_prep(p_mla, tables, mp, CTX)
        y_mla = _attention(q, k, v, CTX)

        z, h2, logits_t = _merge(z, g1, g2, modsel[l], y_ssm, y_rwkv, y_mla, w_gate,
                                 w_branch[l].astype(BF16), w_out[l].astype(BF16),
                                 ssm_glu_w[l].astype(BF16), ssm_glu_b[l].reshape(1, -1),
                                 router_w[l].T, CTX)
        slot, gate = _route(logits_t, CTX, cap_c, cap_l)
        yc = _experts(h2, slot, gate, moe_w1[l].astype(BF16), moe_w3[l].astype(BF16),
                      moe_w2[l].astype(BF16), cap_c + cap_l, cap_c, CTX)
        z = _scatter(z, modsel[l], jnp.swapaxes(slot, 1, 2), yc, CTX, cap_c)
    return z[:, CTX:]
```

```python
import functools
import math

import jax
import jax.numpy as jnp
from jax import lax
from jax.experimental import pallas as pl
from jax.experimental.pallas import tpu as pltpu

F32 = jnp.float32
BF16 = jnp.bfloat16

D_MODEL = 1024
GRID_W = 64
N_MOD = 6
NORM_EPS = 1e-6
GN_EPS = 64e-5
BRANCH_WIDTH = 512
SSM_GROUP = 16
SSM_GROUPS = BRANCH_WIDTH // SSM_GROUP
SSM_STATE = 64
SSM_CHUNK = 16
RWKV_W = BRANCH_WIDTH
RWKV_HEAD = 64
RWKV_HEADS = RWKV_W // RWKV_HEAD
RWKV_CHUNK = 64
LORA_W = 128
RWKV_IN = 3 * RWKV_W + 3 * LORA_W
MLA_HEADS = 8
MLA_NOPE = 64
MLA_ROPE = 32
MLA_V = 64
Q_LORA = 384
KV_LORA = 256
MLA_IN = Q_LORA + KV_LORA + MLA_ROPE
MLA_IN_PAD = 768
MLA_SCALE = 1.0 / math.sqrt(MLA_NOPE + MLA_ROPE)
ROPE_BASE = 10000.0
HEAD_PAD = 128
RWKV_OFF = BRANCH_WIDTH
MLA_OFF = RWKV_OFF + RWKV_IN
GATE_OFF = MLA_OFF + MLA_IN
N_EXPERTS = 16
EXPERT_FF = 1536
EC_CAPACITY = 2
VMEM_LIMIT = 56 * 1024 * 1024


def _params(*sem):
    return pltpu.CompilerParams(dimension_semantics=sem, vmem_limit_bytes=VMEM_LIMIT)


def _dot(a, b):
    return jnp.dot(a, b, preferred_element_type=F32)


def _dot_nt(a, b):
    return lax.dot_general(a, b, (((1,), (1,)), ((), ())), preferred_element_type=F32)


def _dot_tn(a, b):
    return lax.dot_general(a, b, (((0,), (0,)), ((), ())), preferred_element_type=F32)


def _split2(x):
    hi = x.astype(BF16)
    lo = (x - hi.astype(F32)).astype(BF16)
    return hi, lo


def _split3(x):
    hi = x.astype(BF16)
    r1 = x - hi.astype(F32)
    mid = r1.astype(BF16)
    lo = (r1 - mid.astype(F32)).astype(BF16)
    return hi, mid, lo


def _dot_exact_rhs(xs, m):
    rows, width = xs[0].shape
    k = m.shape[0]
    nt = width // k
    parts = [p[:, j * k:(j + 1) * k] for x in xs for p in _split2(x) for j in range(nt)]
    res = _dot(jnp.concatenate(parts, axis=0), m)
    blk = lambda i: res[i * rows:(i + 1) * rows]
    return [jnp.concatenate([blk((2 * i) * nt + j) + blk((2 * i + 1) * nt + j) for j in range(nt)], axis=1)
            for i in range(len(xs))]


def _dot_exact_lhs(m, x, parts):
    pieces = _split2(x) if parts == 2 else _split3(x)
    acc = _dot(m, pieces[0])
    for p in pieces[1:]:
        acc = acc + _dot(m, p)
    return acc


def _sigmoid(x):
    return 0.5 * jnp.tanh(0.5 * x) + 0.5


def _norm_mod(x, g, shift, scale):
    y = x * lax.rsqrt(jnp.mean(x * x, axis=-1, keepdims=True) + NORM_EPS)
    return (y * g) * (1.0 + scale) + shift


def _token_tile(ctx_len, seq):
    for tm in (256, 128, 64):
        if ctx_len % tm == 0 and seq % tm == 0:
            return tm
    raise ValueError("context and latent lengths must be multiples of 64")


def _mods_kernel(cc_ref, w_ref, b_ref, o_ref):
    c = cc_ref[...]
    s = (c * _sigmoid(c)).astype(BF16)
    o_ref[0] = _dot(s, w_ref[0].astype(BF16)) + b_ref[0]


def _mods(cc, ada_w, ada_b):
    L, D, N = ada_w.shape
    R = cc.shape[0]
    tn = 1536
    return pl.pallas_call(
        _mods_kernel,
        grid=(L, N // tn),
        in_specs=[pl.BlockSpec((R, D), lambda l, j: (0, 0)),
                  pl.BlockSpec((1, D, tn), lambda l, j: (l, 0, j)),
                  pl.BlockSpec((1, 1, tn), lambda l, j: (l, 0, j))],
        out_specs=pl.BlockSpec((1, R, tn), lambda l, j: (l, 0, j)),
        out_shape=jax.ShapeDtypeStruct((L, R, N), F32),
        compiler_params=_params("parallel", "parallel"),
        name="adaln_mods",
    )(cc, ada_w, ada_b.reshape(L, 1, N))


def _kin_kernel(z_ref, g_ref, mod_ref, ws_ref, wr_ref, wm_ref, os_ref, or_ref, om_ref):
    NB, tm = z_ref.shape[0], z_ref.shape[1]
    W = ws_ref.shape[1]
    h = jnp.concatenate([_norm_mod(z_ref[i], g_ref[...], mod_ref[i, 0][0:1], mod_ref[i, 0][1:2])
                         for i in range(NB)], axis=0).astype(BF16)
    ps, pr, pm = _dot(h, ws_ref[...]), _dot(h, wr_ref[...]), _dot(h, wm_ref[...])
    for i in range(NB):
        os_ref[:, i * W:(i + 1) * W] = ps[i * tm:(i + 1) * tm]
        or_ref[i] = pr[i * tm:(i + 1) * tm]
        om_ref[i] = pm[i * tm:(i + 1) * tm]


def _input_proj(z, g, modsel, ws, wr, wm, ctx_len):
    B, S, D = z.shape
    tm = _token_tile(ctx_len, S - ctx_len)
    nct = ctx_len // tm
    NB = 4
    tok = lambda n: pl.BlockSpec((NB, tm, n), lambda b, i: (b, i, 0))
    full = lambda a: pl.BlockSpec(a.shape, lambda b, i: (0,) * a.ndim)
    return pl.pallas_call(
        _kin_kernel,
        grid=(B // NB, S // tm),
        in_specs=[tok(D), full(g),
                  pl.BlockSpec((NB, 1, N_MOD, D), lambda b, i: (b, jnp.where(i >= nct, 1, 0), 0, 0)),
                  full(ws), full(wr), full(wm)],
        out_specs=[pl.BlockSpec((tm, NB * ws.shape[1]), lambda b, i: (i, b)), tok(wr.shape[1]), tok(wm.shape[1])],
        out_shape=[jax.ShapeDtypeStruct((S, B * ws.shape[1]), F32),
                   jax.ShapeDtypeStruct((B, S, wr.shape[1]), F32),
                   jax.ShapeDtypeStruct((B, S, wm.shape[1]), F32)],
        compiler_params=_params("parallel", "parallel"),
        name="input_proj",
    )(z, g, modsel, ws, wr, wm)


def _cpow(ar, ai, lag, shape, nbits=5):
    pr = jnp.ones(shape, F32)
    pi = jnp.zeros(shape, F32)
    for b in range(nbits):
        bit = ((lag >> b) & 1) == 1
        fr = jnp.where(bit, ar, 1.0)
        fi = jnp.where(bit, ai, 0.0)
        pr, pi = pr * fr - pi * fi, pr * fi + pi * fr
        ar, ai = ar * ar - ai * ai, 2.0 * ar * ai
    return pr, pi


def _ssm_prep_kernel(lc_re_ref, lc_im_ref, lr_re_ref, lr_im_ref, ldt_ref, bt_re_ref, bt_im_ref,
                     ct_re_ref, ct_im_ref, wt_ref, wso_ref, wsi_ref, a_ref):
    C, GC, P = SSM_CHUNK, SSM_GROUP, SSM_STATE
    W = C * GC
    lane = lax.broadcasted_iota(jnp.int32, (1, W), 1)
    quarter = lane // P
    is_re = (quarter == 0) | (quarter == 3)
    jcol = lane // GC
    srow = lax.broadcasted_iota(jnp.int32, (W, 1), 0) // GC
    for d in (0, 1):
        dt = jnp.exp(ldt_ref[d, 0])
        lr, li = lc_re_ref[d, 0], lc_im_ref[d, 0]
        mag = jnp.exp(lr * dt)
        ar, ai = mag * jnp.cos(li * dt), mag * jnp.sin(li * dt)
        cr, ci = ct_re_ref[d, 0], ct_im_ref[d, 0]
        lag_z = jcol if d == 0 else (C - 1) - jcol
        lag_s = jcol + 1 if d == 0 else C - jcol

        def q_of(lag):
            pr, pi = _cpow(ar, ai, lag, (P, W))
            q_re = cr * pr - ci * pi
            q_im = -(cr * pi + ci * pr)
            return q_re, q_im

        qz_re, qz_im = q_of(lag_z)
        qs_re, qs_im = q_of(lag_s)
        lr4, li4 = lr_re_ref[d, 0], lr_im_ref[d, 0]
        mag4 = jnp.exp(lr4 * dt)
        ar4, ai4 = mag4 * jnp.cos(li4 * dt), mag4 * jnp.sin(li4 * dt)
        den = lr4 * lr4 + li4 * li4
        nr, ni = ar4 - 1.0, ai4
        coef_re = (nr * lr4 + ni * li4) / den
        coef_im = (ni * lr4 - nr * li4) / den
        br, bi = bt_re_ref[0], bt_im_ref[0]
        bb_re = coef_re * br - coef_im * bi
        bb_im = coef_re * bi + coef_im * br
        bcat = jnp.where(lane < P, bb_re, bb_im)[:, :2 * P]
        qz = jnp.concatenate([qz_re, qz_im], axis=0)
        z = jnp.dot(bcat, qz, preferred_element_type=F32,
                    precision=lax.Precision.HIGHEST)
        rows = []
        for s in range(C):
            if d == 0:
                sh = pltpu.roll(z, GC * s, axis=1) if s else z
                rows.append(jnp.where(lane >= GC * s, sh, 0.0))
            else:
                m = C - 1 - s
                sh = pltpu.roll(z, W - GC * m, axis=1) if m else z
                rows.append(jnp.where(lane < W - GC * m, sh, 0.0))
        wt_ref[d, 0] = jnp.concatenate(rows, axis=0).astype(BF16)
        wsi_ref[d, 0] = jnp.concatenate([qs_re, qs_im], axis=0).astype(BF16)
        e_row = (C - 1) - srow if d == 0 else srow
        er, ei = _cpow(ar4, ai4, e_row, (W, W), nbits=4)
        bbx = jnp.where(is_re, bb_re, bb_im)
        bby = jnp.where(is_re, -bb_im, bb_re)
        bbx = jnp.concatenate([bbx] * C, axis=0)
        bby = jnp.concatenate([bby] * C, axis=0)
        wso_ref[d, 0] = (er * bbx + ei * bby).astype(BF16)
        cr16, ci16 = _cpow(ar4, ai4, jnp.full((1, W), C, jnp.int32), (1, W))
        a_ref[d, 0, 0:1, :] = cr16
        a_ref[d, 0, 1:2, :] = jnp.where(is_re, -ci16, ci16)


def _ssm_prep(lam_re, lam_im, log_dt, b_re, b_im, c_re, c_im):
    G, P, GC, C = SSM_GROUPS, SSM_STATE, SSM_GROUP, SSM_CHUNK
    W = C * GC
    lc_re, lc_im = lam_re[..., None], lam_im[..., None]
    lr_re = jnp.tile(lam_re, (1, 1, 4))[:, :, None, :]
    lr_im = jnp.tile(lam_im, (1, 1, 4))[:, :, None, :]
    ldt = log_dt[..., None, None]
    bt_re = jnp.tile(jnp.swapaxes(b_re, 1, 2), (1, 1, 4))
    bt_im = jnp.tile(jnp.swapaxes(b_im, 1, 2), (1, 1, 4))
    ct_re = jnp.tile(jnp.swapaxes(c_re, 2, 3), (1, 1, 1, C))
    ct_im = jnp.tile(jnp.swapaxes(c_im, 2, 3), (1, 1, 1, C))
    d4 = lambda a, b: pl.BlockSpec((2, 1, a, b), lambda g: (0, g, 0, 0))
    mat = jax.ShapeDtypeStruct((2, G, W, W), BF16)
    return pl.pallas_call(
        _ssm_prep_kernel,
        grid=(G,),
        in_specs=[d4(P, 1), d4(P, 1), d4(1, W), d4(1, W), d4(1, 1),
                  pl.BlockSpec((1, GC, W), lambda g: (g, 0, 0)),
                  pl.BlockSpec((1, GC, W), lambda g: (g, 0, 0)),
                  d4(P, W), d4(P, W)],
        out_specs=[d4(W, W), d4(W, W), d4(2 * P, W), d4(2, W)],
        out_shape=[mat, mat, jax.ShapeDtypeStruct((2, G, 2 * P, W), BF16),
                   jax.ShapeDtypeStruct((2, G, 2, W), F32)],
        compiler_params=_params("parallel"),
        name="ssm_prep",
    )(lc_re, lc_im, lr_re, lr_im, ldt, bt_re, bt_im, ct_re, ct_im)


def _ssm_kernel(*refs, d, final, KT):
    if final:
        x_ref, ub_ref, yp_ref, wt_ref, wso_ref, wsi_ref, a_ref, dn_ref, o_ref, st_ref = refs
    else:
        x_ref, wt_ref, wso_ref, wsi_ref, a_ref, ub_ref, o_ref, st_ref = refs
    C, GC, P = SSM_CHUNK, SSM_GROUP, SSM_STATE
    BT = x_ref.shape[2]
    GB = x_ref.shape[3] // GC
    R = KT * BT

    @pl.when(pl.program_id(2) == 0)
    def _():
        st_ref[...] = jnp.zeros_like(st_ref)

    gs = range(GB)
    if final:
        ub = [ub_ref[g, 0, 0] for g in gs]
    else:
        xt = [x_ref[:, t].reshape(R, GB * GC) for t in range(C)]
        ub = [jnp.concatenate([xt[t][:, g * GC:(g + 1) * GC] for t in range(C)], axis=1).astype(BF16) for g in gs]
        for g in gs:
            ub_ref[g, 0, 0] = ub[g]
    loc = [_dot(ub[g], wso_ref[0, g]) for g in gs]
    a1 = [a_ref[0, g, 0:1, :] for g in gs]
    a2 = [a_ref[0, g, 1:2, :] for g in gs]
    x = [st_ref[g] for g in gs]
    xs = [[None] * KT for _ in gs]
    for k in (range(KT) if d == 0 else range(KT - 1, -1, -1)):
        for g in gs:
            xs[g][k] = x[g][:, :2 * P]
            x[g] = a1[g] * x[g] + a2[g] * pltpu.roll(x[g], 2 * P, axis=1) + loc[g][k * BT:(k + 1) * BT]
    ys = []
    for g in gs:
        st_ref[g] = x[g]
        xin = jnp.concatenate(xs[g], axis=0).astype(BF16)
        ys.append(_dot(ub[g], wt_ref[0, g]) + _dot(xin, wsi_ref[0, g]))
    if not final:
        for g in gs:
            o_ref[g, 0, 0] = ys[g]
        return
    ys = [ys[g] + yp_ref[g, 0, 0] for g in gs]
    for t in range(C):
        yt = jnp.concatenate([ys[g][:, t * GC:(t + 1) * GC] for g in gs], axis=1)
        o_ref[:, t] = (yt + x_ref[:, t].reshape(R, GB * GC) * dn_ref[...]).reshape(KT, BT, GB * GC)


def _ssm_dir(p_tm, prev, prep, d_skip, d, B, ctx_len):
    S = p_tm.shape[0]
    G, GC, C, P = SSM_GROUPS, SSM_GROUP, SSM_CHUNK, SSM_STATE
    W = C * GC
    LW = 128
    GB = LW // GC
    BT = 16 if B % 16 == 0 else 8
    KT = _token_tile(ctx_len, S - ctx_len) // C
    NT, NTc = S // (C * KT), ctx_len // (C * KT)
    wt, wso, wsi, a4 = prep
    final = prev is not None
    R = KT * BT

    def tile(ti):
        if d == 0:
            return ti
        return jnp.where(ti < NTc, NTc - 1 - ti, NT - 1 - (ti - NTc))

    x4 = p_tm.reshape(S // C, C, B, G * GC)
    xspec = pl.BlockSpec((KT, C, BT, LW), lambda gb, bt, ti: (tile(ti), 0, bt, gb))
    gspec = pl.BlockSpec((GB, 1, 1, R, W), lambda gb, bt, ti: (gb, bt, tile(ti), 0, 0))
    gshape = (G, B // BT, NT, R, W)
    wspec = lambda r: pl.BlockSpec((1, GB, r, W), lambda gb, bt, ti: (d, gb, 0, 0))
    in_specs, args = [xspec], [x4]
    if final:
        in_specs += [gspec, gspec]
        args += list(prev)
    in_specs += [wspec(W), wspec(W), wspec(2 * P), wspec(2)]
    args += [wt, wso, wsi, a4]
    if final:
        in_specs.append(pl.BlockSpec((1, LW), lambda gb, bt, ti: (0, gb)))
        args.append(d_skip.reshape(1, G * GC))
        out_specs, out_shape = xspec, jax.ShapeDtypeStruct(x4.shape, F32)
    else:
        out_specs = [gspec, gspec]
        out_shape = [jax.ShapeDtypeStruct(gshape, BF16), jax.ShapeDtypeStruct(gshape, F32)]
    y = pl.pallas_call(
        functools.partial(_ssm_kernel, d=d, final=final, KT=KT),
        grid=(G // GB, B // BT, NT),
        in_specs=in_specs,
        out_specs=out_specs,
        out_shape=out_shape,
        scratch_shapes=[pltpu.VMEM((GB, BT, W), F32)],
        compiler_params=_params("parallel", "parallel", "arbitrary"),
        name="ssm_rev" if d else "ssm_fwd",
    )(*args)
    return y.reshape(p_tm.shape) if final else y


def _ssm_scan(p_tm, prep, d_skip, B, ctx_len):
    fwd = _ssm_dir(p_tm, None, prep, d_skip, 0, B, ctx_len)
    return _ssm_dir(p_tm, fwd, prep, d_skip, 1, B, ctx_len)


def _rwkv_kernel(*refs, d, final, NC, NCc):
    if final:
        (p_ref, hp_ref, hn_ref, yp_ref, mu_ref, w0_ref, w2_ref, a0_ref, a2_ref, g2_ref, pv_ref,
         o_ref, s_ref, xb_ref, yb_ref) = refs
    else:
        (p_ref, hp_ref, hn_ref, mu_ref, w0_ref, w2_ref, a0_ref, a2_ref, g2_ref, pv_ref,
         o_ref, s_ref, xb_ref, yb_ref) = refs
    L, N, H, W = RWKV_CHUNK, RWKV_HEAD, RWKV_HEADS, RWKV_W
    NB = p_ref.shape[0]
    ci = pl.program_id(1)
    if d == 0:
        c = ci
    else:
        c = jnp.where(ci < NCc, NCc - 1 - ci, NC - 1 - (ci - NCc))

    @pl.when(ci == 0)
    def _():
        s_ref[...] = jnp.zeros_like(s_ref)

    keep_prev = jnp.where((c == 0) | (c == NCc), 0.0, 1.0)
    keep_next = jnp.where((c == NCc - 1) | (c == NC - 1), 0.0, 1.0)
    xs = []
    for bi in range(NB):
        p = p_ref[bi]
        xb_ref[bi, 0:8, :] = hp_ref[bi] * keep_prev
        xb_ref[bi, 8:8 + L, :] = p
        xb_ref[bi, 8 + L:16 + L, :] = hn_ref[bi] * keep_next
        prev = xb_ref[bi, 7:7 + L, :]
        nxt = xb_ref[bi, 9:9 + L, :]
        xs.append(p + mu_ref[...] * (0.5 * (prev + nxt) - p))
    x = jnp.concatenate(xs, axis=0)

    r, k, v = x[:, 0:W], x[:, W:2 * W], x[:, 2 * W:3 * W]
    pw = x[:, 3 * W:3 * W + LORA_W]
    pa = x[:, 3 * W + LORA_W:3 * W + 2 * LORA_W]
    pg = x[:, 3 * W + 2 * LORA_W:3 * W + 3 * LORA_W]
    k_k, k_a, r_k = pv_ref[0:1, :], pv_ref[1:2, :], pv_ref[2:3, :]
    ln_w, ln_b = pv_ref[3:4, :], pv_ref[4:5, :]
    pab = pa.astype(BF16)

    zw = w0_ref[d:d + 1, :] + _dot(jnp.tanh(pw).astype(BF16), w2_ref[d])
    nz = -zw
    softplus = jnp.maximum(nz, 0.0) + jnp.log(1.0 + jnp.exp(-jnp.abs(nz)))
    lw = -jnp.exp(-softplus - 0.5)
    a = _sigmoid(a0_ref[d:d + 1, :] + _dot(pab, a2_ref[d]))
    kd = k * (1.0 + (a - 1.0) * k_a)

    hrow = lax.broadcasted_iota(jnp.int32, (2 * N, 2 * N), 0) // N
    hcol = lax.broadcasted_iota(jnp.int32, (2 * N, 2 * N), 1) // N
    head_ones = jnp.where(hrow == hcol, 1.0, 0.0).astype(BF16)

    kk = k * k_k
    kk = kk * lax.rsqrt(_dot_exact_rhs([kk * kk], head_ones)[0] + 1e-12)
    b = a * kk

    PW2 = 2 * N
    trow = lax.broadcasted_iota(jnp.int32, (L, PW2), 0)
    lane2 = lax.broadcasted_iota(jnp.int32, (L, PW2), 1)
    left = lane2 < N
    tcol = jnp.where(left, lane2, lane2 - N)
    if d == 0:
        strict, incl = tcol < trow, tcol <= trow
    else:
        strict, incl = tcol > trow, tcol >= trow

    def bdiag(t):
        return jnp.concatenate([jnp.where(left, t, 0.0), jnp.where(left, 0.0, t)], axis=0).astype(BF16)

    brow = lax.broadcasted_iota(jnp.int32, (NB * L, NB * L), 0)
    bcol = lax.broadcasted_iota(jnp.int32, (NB * L, NB * L), 1)
    upto = (bcol <= brow) if d == 0 else (bcol >= brow)
    cum = jnp.where(upto, jnp.where((brow // L) == (bcol // L), 1.0, 0.0), 0.0).astype(BF16)
    cs = _dot_exact_lhs(cum, lw, 3)
    last = L - 1 if d == 0 else 0
    cls = [cs[bi * L + last:bi * L + last + 1, :] for bi in range(NB)]
    cl = jnp.concatenate([jnp.broadcast_to(t, (L, W)) for t in cls], axis=0)
    e_to_end = jnp.exp(cl - cs)
    e_neg = jnp.exp(-cs)
    rt = (r * jnp.exp(cs)).astype(BF16)
    at = (kk * jnp.exp(cs - lw)).astype(BF16)
    bt = b * e_neg
    kt = kd * e_neg
    kh = (kd * e_to_end).astype(BF16)
    bh = (b * e_to_end).astype(BF16)
    e_chunk = [jnp.exp(t) for t in cls]
    vb = v.astype(BF16)

    ids = [(bi, j) for bi in range(NB) for j in range(H // 2)]
    n = range(len(ids))
    rs = [slice(bi * L, (bi + 1) * L) for bi, _ in ids]
    ls = [slice(j * PW2, (j + 1) * PW2) for _, j in ids]
    ar = [jnp.concatenate([at[rs[i], ls[i]], rt[rs[i], ls[i]]], axis=0) for i in n]
    s0 = [s_ref[bi, j] for bi, j in ids]
    g_b = [_dot_nt(ar[i], bdiag(bt[rs[i], ls[i]])) for i in n]
    g_k = [_dot_nt(ar[i], bdiag(kt[rs[i], ls[i]])) for i in n]
    g_s = [_dot_nt(ar[i], bdiag(s0[i])) for i in n]
    nab = [jnp.where(strict, g_b[i][:L], 0.0) for i in n]
    mrb = [jnp.where(incl, g_b[i][L:], 0.0).astype(BF16) for i in n]
    nm = [jnp.concatenate([jnp.where(strict, g_k[i][:L], 0.0), jnp.where(incl, g_k[i][L:], 0.0)],
                          axis=0).astype(BF16) for i in n]
    nv = [g_s[i] + _dot(nm[i], bdiag(v[rs[i], ls[i]])) for i in n]
    pm = [-t for t in nab]
    q = [_dot(t.astype(BF16), bdiag(t)) for t in nab]
    steps = int(math.log2(L)) - 1
    for it in range(steps):
        qd = [bdiag(t) for t in q]
        if it + 1 < steps:
            pq = [_dot(jnp.concatenate([pm[i], q[i]], axis=0).astype(BF16), qd[i]) for i in n]
            pm = [pm[i] + q[i] + pq[i][:L] for i in n]
            q = [t[L:] for t in pq]
        else:
            pm = [pm[i] + q[i] + _dot(pm[i].astype(BF16), qd[i]) for i in n]
    u = [nv[i][:L] + _dot(pm[i].astype(BF16), bdiag(nv[i][:L])) for i in n]
    for i in n:
        yb_ref[rs[i], ls[i]] = nv[i][L:] - _dot(mrb[i], bdiag(u[i]))
    for i, (bi, j) in enumerate(ids):
        vu = jnp.concatenate([vb[rs[i], ls[i]], -u[i].astype(BF16)], axis=0)
        kb = jnp.concatenate([kh[rs[i], ls[i]], bh[rs[i], ls[i]]], axis=0)
        full = _dot_tn(vu, kb)
        s_ref[bi, j] = s0[i] * e_chunk[bi][:, ls[i]] + jnp.where(left, full[:L], full[L:])

    if not final:
        o_ref[...] = yb_ref[...].reshape(NB, L, W)
        return
    y = yp_ref[...].reshape(NB * L, W) + yb_ref[...]
    o = 1 - d
    a_o = _sigmoid(a0_ref[o:o + 1, :] + _dot(pab, a2_ref[o]))
    kd_sum = kd + k * (1.0 + (a_o - 1.0) * k_a)
    y_sum, rk_sum = _dot_exact_rhs([y, r * kd_sum * r_k], head_ones)
    dev = y - y_sum * (1.0 / N)
    var = _dot_exact_rhs([dev * dev], head_ones)[0] * (1.0 / N)
    yn = dev * lax.rsqrt(var + GN_EPS) * ln_w + ln_b
    bonus = rk_sum * v
    g = _dot(_sigmoid(pg).astype(BF16), g2_ref[...])
    o_ref[...] = ((yn + bonus) * g).reshape(NB, L, W)


def _rwkv_dir(p_rwkv, y_prev, params, d, ctx_len):
    B, S, PW = p_rwkv.shape
    L, W = RWKV_CHUNK, RWKV_W
    NC, NCc = S // L, ctx_len // L
    NB = 4
    final = y_prev is not None

    def chunk(ci):
        if d == 0:
            return ci
        return jnp.where(ci < NCc, NCc - 1 - ci, NC - 1 - (ci - NCc))

    hb = L // 8
    tok = lambda n: pl.BlockSpec((NB, L, n), lambda b, ci: (b, chunk(ci), 0))
    full = lambda a: pl.BlockSpec(a.shape, lambda b, ci: (0,) * a.ndim)
    in_specs = [tok(PW),
                pl.BlockSpec((NB, 8, PW), lambda b, ci: (b, jnp.maximum(chunk(ci) * hb - 1, 0), 0)),
                pl.BlockSpec((NB, 8, PW), lambda b, ci: (b, jnp.minimum((chunk(ci) + 1) * hb, S // 8 - 1), 0))]
    args = [p_rwkv, p_rwkv, p_rwkv]
    if final:
        in_specs.append(tok(W))
        args.append(y_prev)
    in_specs += [full(a) for a in params]
    args += list(params)
    return pl.pallas_call(
        functools.partial(_rwkv_kernel, d=d, final=final, NC=NC, NCc=NCc),
        grid=(B // NB, NC),
        in_specs=in_specs,
        out_specs=tok(W),
        out_shape=jax.ShapeDtypeStruct((B, S, W), F32),
        scratch_shapes=[pltpu.VMEM((NB, RWKV_HEADS // 2, RWKV_HEAD, 2 * RWKV_HEAD), F32),
                        pltpu.VMEM((NB, L + 16, PW), F32),
                        pltpu.VMEM((NB * L, W), F32)],
        compiler_params=_params("parallel", "arbitrary"),
        name="rwkv_rev" if d else "rwkv_fwd",
    )(*args)


def _rwkv_params(mu, w0, w2, a0, a2, g2, k_k, k_a, r_k, ln_w, ln_b):
    W = RWKV_W
    half = LORA_W // 2

    def pad_dir(w):
        out = jnp.zeros((2, LORA_W, W), F32)
        out = out.at[0, :half].set(w[0]).at[1, half:].set(w[1])
        return out.astype(BF16)

    pv = jnp.zeros((8, W), F32)
    pv = pv.at[0].set(k_k).at[1].set(k_a).at[2].set(r_k.reshape(W)).at[3].set(ln_w).at[4].set(ln_b)
    return (mu.reshape(1, RWKV_IN), w0, pad_dir(w2), a0, pad_dir(a2), g2.astype(BF16), pv)


def _mla_prep_kernel(p_ref, cq_ref, sq_ref, ck_ref, sk_ref, qn_ref, kvn_ref, wq_ref, wk_ref, wv_ref,
                     gn_ref, q_ref, k_ref, v_ref):
    HP = HEAD_PAD
    p = p_ref[0]
    lane = lax.broadcasted_iota(jnp.int32, (1, HP), 1)
    m_nope = jnp.where(lane < MLA_NOPE, 1.0, 0.0)
    m_rope = jnp.where(lane < MLA_NOPE, 0.0, jnp.where(lane < MLA_NOPE + MLA_ROPE, 1.0, 0.0))
    g_q, g_kn, g_kr = gn_ref[0:1, :], gn_ref[1:2, :], gn_ref[2:3, :]

    def rms(x, n):
        return lax.rsqrt(jnp.sum(x * x, axis=-1, keepdims=True) * (1.0 / n) + NORM_EPS)

    xq = p[:, :Q_LORA]
    cq = (xq * rms(xq, Q_LORA) * qn_ref[...]).astype(BF16)
    xkv = p[:, Q_LORA:Q_LORA + KV_LORA]
    ckv = (xkv * rms(xkv, KV_LORA) * kvn_ref[...]).astype(BF16)
    q = _dot(cq, wq_ref[...])
    kn = _dot(ckv, wk_ref[...])
    v_ref[0] = _dot_nt(wv_ref[...], ckv).astype(BF16)

    kr = p[:, Q_LORA + KV_LORA:Q_LORA + KV_LORA + HP]
    krn = kr * rms(kr, MLA_ROPE) * g_kr
    to_rope = pltpu.roll(krn, MLA_NOPE, axis=1)
    sw_a = jnp.where(lane >= 112, 0.0, jnp.where(lane >= 96, pltpu.roll(krn, 80, axis=1), 0.0))
    sw_b = jnp.where(lane >= 112, pltpu.roll(krn, 112, axis=1), 0.0)
    kext = to_rope + sw_a + sw_b
    krot = kext * ck_ref[...] + pltpu.roll(kext, HP - MLA_ROPE, axis=1) * sk_ref[...]

    cq_t, sq_t = cq_ref[...], sq_ref[...]
    for h in range(MLA_HEADS):
        sl = slice(h * HP, (h + 1) * HP)
        qh = q[:, sl]
        scale = m_nope * rms(qh * m_nope, MLA_NOPE) + (1.0 - m_nope) * rms(qh * m_rope, MLA_ROPE)
        qn = qh * scale * g_q
        q_ref[0, :, sl] = (qn * cq_t + pltpu.roll(qn, HP - MLA_ROPE, axis=1) * sq_t).astype(BF16)
        kh = kn[:, sl]
        k_ref[0, :, sl] = (kh * rms(kh, MLA_NOPE) * g_kn + krot).astype(BF16)


def _mla_prep(p_mla, tables, params, ctx_len):
    B, S, PW = p_mla.shape
    tm = _token_tile(ctx_len, S - ctx_len)
    cq_t, sq_t, ck_t, sk_t = tables
    HW = MLA_HEADS * HEAD_PAD
    tok = lambda n: pl.BlockSpec((1, tm, n), lambda b, i: (b, i, 0))
    tab = pl.BlockSpec((tm, HEAD_PAD), lambda b, i: (i, 0))
    full = lambda a: pl.BlockSpec(a.shape, lambda b, i: (0,) * a.ndim)
    return pl.pallas_call(
        _mla_prep_kernel,
        grid=(B, S // tm),
        in_specs=[tok(PW), tab, tab, tab, tab] + [full(a) for a in params],
        out_specs=[tok(HW), tok(HW), pl.BlockSpec((1, MLA_HEADS * MLA_V, tm), lambda b, i: (b, 0, i))],
        out_shape=[jax.ShapeDtypeStruct((B, S, HW), BF16), jax.ShapeDtypeStruct((B, S, HW), BF16),
                   jax.ShapeDtypeStruct((B, MLA_HEADS * MLA_V, S), BF16)],
        compiler_params=_params("parallel", "parallel"),
        name="mla_prep",
    )(p_mla, cq_t, sq_t, ck_t, sk_t, *params)


def _mla_params(q_norm, kv_norm, w_uq, w_ukv, qn_nope, kn_nope, qn_rope, kn_rope):
    H, NP, RP, HP = MLA_HEADS, MLA_NOPE, MLA_ROPE, HEAD_PAD
    half = RP // 2
    swap = jnp.concatenate([jnp.arange(half, RP), jnp.arange(0, half)])
    wq = w_uq.reshape(Q_LORA, H, NP + RP)
    wq = jnp.concatenate([wq, wq[:, :, NP + swap]], axis=-1).reshape(Q_LORA, H * HP)
    wkv = w_ukv.reshape(KV_LORA, H, NP + MLA_V)
    wk = jnp.concatenate([wkv[:, :, :NP], jnp.zeros((KV_LORA, H, HP - NP), F32)], axis=-1)
    wk = wk.reshape(KV_LORA, H * HP)
    wv = wkv[:, :, NP:].reshape(KV_LORA, H * MLA_V)
    gn = jnp.zeros((8, HP), F32)
    gn = gn.at[0].set(jnp.concatenate([qn_nope, qn_rope, qn_rope[swap]]))
    gn = gn.at[1, :NP].set(kn_nope).at[2, :RP].set(kn_rope)
    return (q_norm.reshape(1, Q_LORA), kv_norm.reshape(1, KV_LORA),
            wq.astype(BF16), wk.astype(BF16), wv.T.astype(BF16), gn)


def _rope_tables(ctx_len, seq):
    rows = seq // GRID_W
    axis_dims = MLA_ROPE // 2
    row = jnp.repeat(jnp.arange(rows), GRID_W).astype(F32)
    col = jnp.tile(jnp.arange(GRID_W), rows).astype(F32)
    inv = ROPE_BASE ** (-jnp.arange(0, axis_dims, 2, dtype=F32) / axis_dims)
    ang = jnp.concatenate([row[:, None] * inv, col[:, None] * inv], axis=-1)
    cos = jnp.concatenate([jnp.ones((ctx_len, axis_dims), F32), jnp.cos(ang)], axis=0)
    sin = jnp.concatenate([jnp.zeros((ctx_len, axis_dims), F32), jnp.sin(ang)], axis=0)
    S = ctx_len + seq
    pad = jnp.zeros((S, HEAD_PAD - MLA_NOPE - MLA_ROPE), F32)
    cos_t = jnp.concatenate([jnp.ones((S, MLA_NOPE), F32), cos, cos, pad], axis=-1)
    sin_t = jnp.concatenate([jnp.zeros((S, MLA_NOPE), F32), -sin, sin, pad], axis=-1)
    qs = MLA_SCALE * math.log2(math.e)
    return cos_t * qs, sin_t * qs, cos_t, sin_t


def _attn_kernel(q_ref, k_ref, vt_ref, o_ref, *, nct, ctx_len):
    S = k_ref.shape[1]
    i = pl.program_id(2)

    def attend(nk):
        hs = range(q_ref.shape[2] // HEAD_PAD)
        s = [_dot_nt(k_ref[0, :nk, h * HEAD_PAD:(h + 1) * HEAD_PAD],
                     q_ref[0, :, h * HEAD_PAD:(h + 1) * HEAD_PAD]) for h in hs]
        m = [jnp.max(t, axis=0, keepdims=True) for t in s]
        e = [jnp.exp2(s[h] - m[h]) for h in hs]
        l = [jnp.sum(t, axis=0, keepdims=True) for t in e]
        o = [_dot(vt_ref[0, h * MLA_V:(h + 1) * MLA_V, :nk], e[h].astype(BF16)) / l[h] for h in hs]
        for j in range(len(hs) // 2):
            pair = jnp.concatenate([o[2 * j], o[2 * j + 1]], axis=0)
            o_ref[0, :, j * 2 * MLA_V:(j + 1) * 2 * MLA_V] = pair.T.astype(BF16)

    @pl.when(i < nct)
    def _():
        attend(ctx_len)

    @pl.when(i >= nct)
    def _():
        attend(S)


def _attention(q, k, v, ctx_len):
    B, S, _ = q.shape
    tq = _token_tile(ctx_len, S - ctx_len)
    HS = 8
    return pl.pallas_call(
        functools.partial(_attn_kernel, nct=ctx_len // tq, ctx_len=ctx_len),
        grid=(B, MLA_HEADS // HS, S // tq),
        in_specs=[pl.BlockSpec((1, tq, HS * HEAD_PAD), lambda b, h, i: (b, i, h)),
                  pl.BlockSpec((1, S, HS * HEAD_PAD), lambda b, h, i: (b, 0, h)),
                  pl.BlockSpec((1, HS * MLA_V, S), lambda b, h, i: (b, h, 0))],
        out_specs=pl.BlockSpec((1, tq, HS * MLA_V), lambda b, h, i: (b, i, h)),
        out_shape=jax.ShapeDtypeStruct((B, S, MLA_HEADS * MLA_V), BF16),
        compiler_params=_params("parallel", "parallel", "parallel"),
        name="mla_attention",
    )(q, k, v)


def _merge_kernel(z_ref, g1_ref, g2_ref, mod_ref, ys_ref, yr_ref, ym_ref, wg_ref, wb_ref, wo_ref,
                  glw_ref, glb_ref, rw_ref, zo_ref, h2_ref, lg_ref):
    D, W = D_MODEL, BRANCH_WIDTH
    NB, tm = z_ref.shape[0], z_ref.shape[1]
    ns = range(NB)
    rows = lambda t: jnp.concatenate(t, axis=0)
    z = [z_ref[i] for i in ns]
    m = [mod_ref[i, 0] for i in ns]
    h = rows([_norm_mod(z[i], g1_ref[...], m[i][0:1], m[i][1:2]) for i in ns]).astype(BF16)
    ys = rows([ys_ref[:, i * W:(i + 1) * W] for i in ns])
    ys = 0.5 * ys * (1.0 + jnp.tanh(math.sqrt(2.0 / math.pi) * (ys + 0.044715 * ys * ys * ys)))
    ys = ys * _sigmoid(_dot(ys.astype(BF16), glw_ref[...]) + glb_ref[...])
    branches = (ys.astype(BF16), rows([yr_ref[i] for i in ns]).astype(BF16), rows([ym_ref[i] for i in ns]))
    acc = jnp.zeros((NB * tm, D), F32)
    for j, yj in enumerate(branches):
        gate = _sigmoid(_dot(h, wg_ref[:, j * D:(j + 1) * D]))
        acc = acc + gate * _dot(yj, wb_ref[j])
    mix = _dot(acc.astype(BF16), wo_ref[...])
    zn = [z[i] + m[i][2:3] * mix[i * tm:(i + 1) * tm] for i in ns]
    h2 = rows([_norm_mod(zn[i], g2_ref[...], m[i][3:4], m[i][4:5]) for i in ns])
    E = rw_ref.shape[0]
    rw_rows = jnp.concatenate(_split3(rw_ref[...]), axis=0)
    hp = _split3(h2)
    t0 = _dot_nt(rw_rows, hp[0])
    t1 = _dot_nt(rw_rows[:2 * E], hp[1])
    t2 = _dot_nt(rw_rows[:E], hp[2])
    lg = t0[:E] + t0[E:2 * E] + t0[2 * E:] + t1[:E] + t1[E:] + t2
    for i in ns:
        zo_ref[i] = zn[i]
        h2_ref[i] = hp[0][i * tm:(i + 1) * tm]
        lg_ref[i] = lg[:, i * tm:(i + 1) * tm]


def _merge(z, g1, g2, modsel, ys, yr, ym, wg, wb, wo, glw, glb, rwt, ctx_len):
    B, S, D = z.shape
    tm = _token_tile(ctx_len, S - ctx_len)
    nct = ctx_len // tm
    E = rwt.shape[0]
    NB = 2
    tok = lambda n: pl.BlockSpec((NB, tm, n), lambda b, i: (b, i, 0))
    full = lambda a: pl.BlockSpec(a.shape, lambda b, i: (0,) * a.ndim)
    W = BRANCH_WIDTH
    return pl.pallas_call(
        _merge_kernel,
        grid=(B // NB, S // tm),
        in_specs=[tok(D), full(g1), full(g2),
                  pl.BlockSpec((NB, 1, N_MOD, D), lambda b, i: (b, jnp.where(i >= nct, 1, 0), 0, 0)),
                  pl.BlockSpec((tm, NB * W), lambda b, i: (i, b)),
                  tok(W), tok(W), full(wg), full(wb), full(wo), full(glw), full(glb), full(rwt)],
        out_specs=[tok(D), tok(D), pl.BlockSpec((NB, E, tm), lambda b, i: (b, 0, i))],
        out_shape=[jax.ShapeDtypeStruct((B, S, D), F32), jax.ShapeDtypeStruct((B, S, D), BF16),
                   jax.ShapeDtypeStruct((B, E, S), F32)],
        compiler_params=_params("parallel", "parallel"),
        name="merge",
    )(z, g1, g2, modsel, ys, yr, ym, wg, wb, wo, glw, glb, rwt)


def _route_kernel(lg_ref, slot_ref, gate_ref, *, ctx_len, cap_c, cap_l):
    lg = lg_ref[0]
    E, S = lg.shape
    e = jnp.exp(lg - jnp.max(lg, axis=0, keepdims=True))
    aff = e / jnp.sum(e, axis=0, keepdims=True)
    bits = pltpu.bitcast(aff, jnp.int32)
    lane = lax.broadcasted_iota(jnp.int32, (1, S), 1)
    in_c = lane < ctx_len
    ms_c = jnp.where(in_c, 1.0, 0.0)
    LT = 128
    tr = lax.broadcasted_iota(jnp.int32, (LT, LT), 0)
    tc = lax.broadcasted_iota(jnp.int32, (LT, LT), 1)
    tri = jnp.where(tr <= tc, 1.0, 0.0).astype(BF16)

    def counts(x):
        n_c = jnp.sum(x * ms_c, axis=1, keepdims=True)
        return n_c, jnp.sum(x, axis=1, keepdims=True) - n_c

    def prefix(x):
        tiles = [x[:, t * LT:(t + 1) * LT] for t in range(S // LT)]
        inc = _dot(jnp.concatenate(tiles, axis=0).astype(BF16), tri)
        off = jnp.zeros((E, 1), F32)
        out = []
        for t, xt in enumerate(tiles):
            it = inc[t * E:(t + 1) * E]
            out.append(it - xt + off)
            off = off + it[:, LT - 1:LT]
        n_c, _ = counts(x)
        return jnp.concatenate(out, axis=1) - jnp.where(in_c, 0.0, n_c)

    def body(_, carry):
        lo_c, hi_c, lo_l, hi_l = carry
        mid_c = lo_c + ((hi_c - lo_c + 1) >> 1)
        mid_l = lo_l + ((hi_l - lo_l + 1) >> 1)
        n_c, n_l = counts(jnp.where(bits >= jnp.where(in_c, mid_c, mid_l), 1.0, 0.0))
        ok_c, ok_l = n_c >= cap_c, n_l >= cap_l
        return (jnp.where(ok_c, mid_c, lo_c), jnp.where(ok_c, hi_c, mid_c - 1),
                jnp.where(ok_l, mid_l, lo_l), jnp.where(ok_l, hi_l, mid_l - 1))

    zero = jnp.zeros((E, 1), jnp.int32)
    top = jnp.full((E, 1), 0x7F800000, jnp.int32)
    lo_c, _, lo_l, _ = lax.fori_loop(0, 31, body, (zero, top, zero, top))
    thr = jnp.where(in_c, lo_c, lo_l)
    gt = jnp.where(bits > thr, 1.0, 0.0)
    eq = jnp.where(bits == thr, 1.0, 0.0)
    g_c, g_l = counts(gt)
    need = jnp.where(in_c, cap_c - g_c, cap_l - g_l)
    sel = gt + eq * jnp.where(prefix(eq) < need, 1.0, 0.0)
    rank = prefix(sel) + jnp.where(in_c, 0.0, float(cap_c))
    slot_ref[0] = jnp.where(sel > 0.0, rank, -1.0)
    gate_ref[0] = aff * sel


def _route(logits_t, ctx_len, cap_c, cap_l):
    B, E, S = logits_t.shape
    assert S % 128 == 0
    spec = pl.BlockSpec((1, E, S), lambda b: (b, 0, 0))
    return pl.pallas_call(
        functools.partial(_route_kernel, ctx_len=ctx_len, cap_c=cap_c, cap_l=cap_l),
        grid=(B,),
        in_specs=[spec],
        out_specs=[spec, spec],
        out_shape=[jax.ShapeDtypeStruct((B, E, S), F32)] * 2,
        compiler_params=_params("parallel"),
        name="route",
    )(logits_t)


def _expert_kernel(h_ref, slot_ref, gate_ref, w1_ref, w3_ref, w2_ref, oc_ref, ol_ref, *, cap, cap_c, ctx_len):
    NB, S = h_ref.shape[0], h_ref.shape[1]
    xs, gc = [], []
    for i in range(NB):
        for (c0, c1), (t0, t1) in (((0, cap_c), (0, ctx_len)), ((cap_c, cap), (ctx_len, S))):
            slot = slot_ref[i, 0, :, t0:t1]
            gate = gate_ref[i, 0, :, t0:t1]
            cidx = lax.broadcasted_iota(jnp.int32, (c1 - c0, t1 - t0), 0).astype(F32) + float(c0)
            hit = slot == cidx
            onehot = jnp.where(hit, 1.0, 0.0).astype(BF16)
            xs.append(_dot(onehot, h_ref[i, t0:t1, :]))
            gc.append(jnp.sum(jnp.where(hit, gate, 0.0), axis=1, keepdims=True))
    xs = jnp.concatenate(xs, axis=0).astype(BF16)
    gc = jnp.concatenate(gc, axis=0)
    a1 = _dot(xs, w1_ref[0])
    a3 = _dot(xs, w3_ref[0])
    hid = (a1 * _sigmoid(a1) * a3).astype(BF16)
    y = (_dot(hid, w2_ref[0]) * gc).astype(BF16)
    for i in range(NB):
        oc_ref[i] = y[i * cap:i * cap + cap_c]
        ol_ref[i] = y[i * cap + cap_c:(i + 1) * cap]


def _experts(h2, slot, gate, w1, w3, w2, cap, cap_c, ctx_len):
    B, S, D = h2.shape
    E, _, F = w1.shape
    NB = 2
    row = pl.BlockSpec((NB, 1, 1, S), lambda e, b: (b, e, 0, 0))
    return pl.pallas_call(
        functools.partial(_expert_kernel, cap=cap, cap_c=cap_c, ctx_len=ctx_len),
        grid=(E, B // NB),
        in_specs=[pl.BlockSpec((NB, S, D), lambda e, b: (b, 0, 0)), row, row,
                  pl.BlockSpec((1, D, F), lambda e, b: (e, 0, 0)),
                  pl.BlockSpec((1, D, F), lambda e, b: (e, 0, 0)),
                  pl.BlockSpec((1, F, D), lambda e, b: (e, 0, 0))],
        out_specs=[pl.BlockSpec((NB, cap_c, D), lambda e, b: (b, e, 0)),
                   pl.BlockSpec((NB, cap - cap_c, D), lambda e, b: (b, e, 0))],
        out_shape=[jax.ShapeDtypeStruct((B, E * cap_c, D), BF16),
                   jax.ShapeDtypeStruct((B, E * (cap - cap_c), D), BF16)],
        compiler_params=_params("parallel", "parallel"),
        name="experts",
    )(h2, slot.reshape(B, E, 1, S), gate.reshape(B, E, 1, S), w1, w3, w2)


def _scatter_kernel(win_ref, z_ref, mod_ref, sl_ref, yc_ref, yl_ref, o_ref, ycat_ref, *, cap_c, cap_l, nct, ntl, win):
    sl = sl_ref[0]
    E, tm = sl.shape
    b, i = pl.program_id(0), pl.program_id(1)

    def onehot_t(lo, n, e):
        cidx = lax.broadcasted_iota(jnp.int32, (n, tm), 0).astype(F32) + lo
        return jnp.where(sl[e:e + 1, :] == cidx, 1.0, 0.0).astype(BF16)

    def finish(onehots, rows):
        acc = _dot_tn(jnp.concatenate(onehots, axis=0), rows)
        o_ref[0] = z_ref[0] + mod_ref[0, 0][5:6] * acc

    @pl.when(i < nct)
    def _():
        finish([onehot_t(0.0, cap_c, e) for e in range(E)], yc_ref[0])

    base = (b * ntl + jnp.maximum(i - nct, 0)) * (E + 1)
    fits = win_ref[base + E] == 1

    @pl.when((i >= nct) & fits)
    def _():
        onehots = []
        for e in range(E):
            w = win_ref[base + e]
            r0 = pl.multiple_of(e * cap_l + w, 16)
            ycat_ref[e * win:(e + 1) * win, :] = yl_ref[0, pl.ds(r0, win), :]
            onehots.append(onehot_t((w + cap_c).astype(F32), win, e))
        finish(onehots, ycat_ref[...])

    @pl.when((i >= nct) & jnp.logical_not(fits))
    def _():
        finish([onehot_t(float(cap_c), cap_l, e) for e in range(E)], yl_ref[0])


def _scatter_windows(slot, ctx_len, tm, cap_l, win):
    B, E, S = slot.shape
    ntl = (S - ctx_len) // tm
    cnt = (slot[:, :, ctx_len:] >= 0).reshape(B, E, ntl, tm).sum(-1).astype(jnp.int32)
    lo = jnp.cumsum(cnt, axis=-1) - cnt
    w = jnp.minimum((lo // 16) * 16, cap_l - win)
    fits = jnp.all(lo - w + cnt <= win, axis=1)
    table = jnp.concatenate([jnp.swapaxes(w, 1, 2), fits[..., None].astype(jnp.int32)], axis=-1)
    return table.reshape(-1)


def _scatter(z, modsel, slot, yc, yl, ctx_len, cap_c, win=64):
    B, S, D = z.shape
    E = slot.shape[1]
    cap_l = yl.shape[1] // E
    tm = _token_tile(ctx_len, S - ctx_len)
    nct = ctx_len // tm
    ntl = (S - ctx_len) // tm
    win = min(win, cap_l)
    tok = lambda n: pl.BlockSpec((1, tm, n), lambda b, i, w: (b, i, 0))
    whole = lambda a: pl.BlockSpec((1,) + a.shape[1:], lambda b, i, w: (b, 0, 0))
    grid_spec = pltpu.PrefetchScalarGridSpec(
        num_scalar_prefetch=1,
        grid=(B, S // tm),
        in_specs=[tok(D),
                  pl.BlockSpec((1, 1, N_MOD, D), lambda b, i, w: (b, jnp.where(i >= nct, 1, 0), 0, 0)),
                  pl.BlockSpec((1, E, tm), lambda b, i, w: (b, 0, i)), whole(yc), whole(yl)],
        out_specs=tok(D),
        scratch_shapes=[pltpu.VMEM((E * win, D), BF16)])
    return pl.pallas_call(
        functools.partial(_scatter_kernel, cap_c=cap_c, cap_l=cap_l, nct=nct, ntl=ntl, win=win),
        grid_spec=grid_spec,
        out_shape=jax.ShapeDtypeStruct((B, S, D), F32),
        compiler_params=_params("parallel", "parallel"),
        name="moe_scatter",
    )(_scatter_windows(slot, ctx_len, tm, cap_l, win), z, modsel, slot, yc, yl)


def kernel(x, c, ctx, c_ctx, ada_w, ada_b, norm1_g, norm2_g, w_in, ssm_lambda_re, ssm_lambda_im, ssm_log_dt, ssm_b_re, ssm_b_im, ssm_c_re, ssm_c_im, ssm_d, ssm_glu_w, ssm_glu_b, rwkv_mu, rwkv_w0, rwkv_w2, rwkv_a0, rwkv_a2, rwkv_g2, rwkv_k_k, rwkv_k_a, rwkv_r_k, rwkv_ln_w, rwkv_ln_b, mla_q_norm, mla_kv_norm, mla_w_uq, mla_w_ukv, mla_qn_nope, mla_kn_nope, mla_qn_rope, mla_kn_rope, w_branch, w_out, router_w, moe_w1, moe_w3, moe_w2):
    B, T, D = x.shape
    CTX = ctx.shape[1]
    S = CTX + T
    depth = ada_w.shape[0]
    cap_c = EC_CAPACITY * CTX // N_EXPERTS
    cap_l = EC_CAPACITY * T // N_EXPERTS
    assert D == D_MODEL and T % GRID_W == 0 and cap_c % 8 == 0 and cap_l % 8 == 0 and B % 8 == 0

    rows = -(-(B + 1) // 8) * 8
    cc = jnp.concatenate([c, c_ctx[None, :], jnp.zeros((rows - B - 1, D), F32)], axis=0)
    mods = _mods(cc, ada_w, ada_b).reshape(depth, rows, N_MOD, D)
    mod_lat = mods[:, :B]
    mod_ctx = jnp.broadcast_to(mods[:, B:B + 1], mod_lat.shape)
    modsel = jnp.stack([mod_ctx, mod_lat], axis=2)

    tables = _rope_tables(CTX, T)

    z = jnp.concatenate([ctx, x], axis=1)
    w_in_b = w_in.astype(BF16)
    for l in range(depth):
        g1, g2 = norm1_g[l].reshape(1, D), norm2_g[l].reshape(1, D)
        wl = w_in_b[l]
        w_ssm = wl[:, :RWKV_OFF]
        w_rwkv = wl[:, RWKV_OFF:MLA_OFF]
        w_mla = jnp.pad(wl[:, MLA_OFF:GATE_OFF], ((0, 0), (0, MLA_IN_PAD - MLA_IN)))
        w_gate = wl[:, GATE_OFF:]
        p_ssm, p_rwkv, p_mla = _input_proj(z, g1, modsel[l], w_ssm, w_rwkv, w_mla, CTX)

        prep = _ssm_prep(ssm_lambda_re[l], ssm_lambda_im[l], ssm_log_dt[l], ssm_b_re[l], ssm_b_im[l],
                         ssm_c_re[l], ssm_c_im[l])
        y_ssm = _ssm_scan(p_ssm, prep, ssm_d[l], B, CTX)

        rp = _rwkv_params(rwkv_mu[l], rwkv_w0[l], rwkv_w2[l], rwkv_a0[l], rwkv_a2[l], rwkv_g2[l],
                          rwkv_k_k[l], rwkv_k_a[l], rwkv_r_k[l], rwkv_ln_w[l], rwkv_ln_b[l])
        y_fwd = _rwkv_dir(p_rwkv, None, rp, 0, CTX)
        y_rwkv = _rwkv_dir(p_rwkv, y_fwd, rp, 1, CTX)

        mp = _mla_params(mla_q_norm[l], mla_kv_norm[l], mla_w_uq[l], mla_w_ukv[l], mla_qn_nope[l],
                         mla_kn_nope[l], mla_qn_rope[l], mla_kn_rope[l])
        q, k, v = _mla_prep(p_mla, tables, mp, CTX)
        y_mla = _attention(q, k, v, CTX)

        z, h2, logits_t = _merge(z, g1, g2, modsel[l], y_ssm, y_rwkv, y_mla, w_gate,
                                 w_branch[l].astype(BF16), w_out[l].astype(BF16),
                                 ssm_glu_w[l].astype(BF16), ssm_glu_b[l].reshape(1, -1),
                                 router_w[l].T, CTX)
        slot, gate = _route(logits_t, CTX, cap_c, cap_l)
        yc, yl = _experts(h2, slot, gate, moe_w1[l].astype(BF16), moe_w3[l].astype(BF16),
                          moe_w2[l].astype(BF16), cap_c + cap_l, cap_c, CTX)
        z = _scatter(z, modsel[l], slot, yc, yl, CTX, cap_c)
    return z[:, CTX:]
```

```python
import functools
import math

import jax
import jax.numpy as jnp
from jax import lax
from jax.experimental import pallas as pl
from jax.experimental.pallas import tpu as pltpu

F32 = jnp.float32
BF16 = jnp.bfloat16

D_MODEL = 1024
GRID_W = 64
N_MOD = 6
NORM_EPS = 1e-6
GN_EPS = 64e-5
BRANCH_WIDTH = 512
SSM_GROUP = 16
SSM_GROUPS = BRANCH_WIDTH // SSM_GROUP
SSM_STATE = 64
SSM_CHUNK = 16
RWKV_W = BRANCH_WIDTH
RWKV_HEAD = 64
RWKV_HEADS = RWKV_W // RWKV_HEAD
RWKV_CHUNK = 64
LORA_W = 128
RWKV_IN = 3 * RWKV_W + 3 * LORA_W
MLA_HEADS = 8
MLA_NOPE = 64
MLA_ROPE = 32
MLA_V = 64
Q_LORA = 384
KV_LORA = 256
MLA_IN = Q_LORA + KV_LORA + MLA_ROPE
MLA_IN_PAD = 768
MLA_SCALE = 1.0 / math.sqrt(MLA_NOPE + MLA_ROPE)
ROPE_BASE = 10000.0
HEAD_PAD = 128
RWKV_OFF = BRANCH_WIDTH
MLA_OFF = RWKV_OFF + RWKV_IN
GATE_OFF = MLA_OFF + MLA_IN
N_EXPERTS = 16
EXPERT_FF = 1536
EC_CAPACITY = 2
VMEM_LIMIT = 56 * 1024 * 1024


def _params(*sem):
    return pltpu.CompilerParams(dimension_semantics=sem, vmem_limit_bytes=VMEM_LIMIT)


def _dot(a, b):
    return jnp.dot(a, b, preferred_element_type=F32)


def _dot_nt(a, b):
    return lax.dot_general(a, b, (((1,), (1,)), ((), ())), preferred_element_type=F32)


def _dot_tn(a, b):
    return lax.dot_general(a, b, (((0,), (0,)), ((), ())), preferred_element_type=F32)


def _split2(x):
    hi = x.astype(BF16)
    lo = (x - hi.astype(F32)).astype(BF16)
    return hi, lo


def _split3(x):
    hi = x.astype(BF16)
    r1 = x - hi.astype(F32)
    mid = r1.astype(BF16)
    lo = (r1 - mid.astype(F32)).astype(BF16)
    return hi, mid, lo


def _dot_exact_rhs(xs, m):
    rows, width = xs[0].shape
    k = m.shape[0]
    nt = width // k
    parts = [p[:, j * k:(j + 1) * k] for x in xs for p in _split2(x) for j in range(nt)]
    res = _dot(jnp.concatenate(parts, axis=0), m)
    blk = lambda i: res[i * rows:(i + 1) * rows]
    return [jnp.concatenate([blk((2 * i) * nt + j) + blk((2 * i + 1) * nt + j) for j in range(nt)], axis=1)
            for i in range(len(xs))]


def _dot_exact_lhs(m, x, parts):
    pieces = _split2(x) if parts == 2 else _split3(x)
    acc = _dot(m, pieces[0])
    for p in pieces[1:]:
        acc = acc + _dot(m, p)
    return acc


def _sigmoid(x):
    return 0.5 * jnp.tanh(0.5 * x) + 0.5


def _norm_mod(x, g, shift, scale):
    y = x * lax.rsqrt(jnp.mean(x * x, axis=-1, keepdims=True) + NORM_EPS)
    return (y * g) * (1.0 + scale) + shift


def _token_tile(ctx_len, seq):
    for tm in (256, 128, 64):
        if ctx_len % tm == 0 and seq % tm == 0:
            return tm
    raise ValueError("context and latent lengths must be multiples of 64")


def _mods_kernel(cc_ref, w_ref, b_ref, o_ref):
    c = cc_ref[...]
    s = (c * _sigmoid(c)).astype(BF16)
    o_ref[0] = _dot(s, w_ref[0].astype(BF16)) + b_ref[0]


def _mods(cc, ada_w, ada_b):
    L, D, N = ada_w.shape
    R = cc.shape[0]
    tn = 1536
    return pl.pallas_call(
        _mods_kernel,
        grid=(L, N // tn),
        in_specs=[pl.BlockSpec((R, D), lambda l, j: (0, 0)),
                  pl.BlockSpec((1, D, tn), lambda l, j: (l, 0, j)),
                  pl.BlockSpec((1, 1, tn), lambda l, j: (l, 0, j))],
        out_specs=pl.BlockSpec((1, R, tn), lambda l, j: (l, 0, j)),
        out_shape=jax.ShapeDtypeStruct((L, R, N), F32),
        compiler_params=_params("parallel", "parallel"),
        name="adaln_mods",
    )(cc, ada_w, ada_b.reshape(L, 1, N))


def _kin_kernel(z_ref, g_ref, mod_ref, ws_ref, wr_ref, wm_ref, os_ref, or_ref, om_ref):
    NB, tm = z_ref.shape[0], z_ref.shape[1]
    W = ws_ref.shape[1]
    h = jnp.concatenate([_norm_mod(z_ref[i], g_ref[...], mod_ref[i, 0][0:1], mod_ref[i, 0][1:2])
                         for i in range(NB)], axis=0).astype(BF16)
    ps, pr, pm = _dot(h, ws_ref[...]), _dot(h, wr_ref[...]), _dot(h, wm_ref[...])
    for i in range(NB):
        os_ref[:, i * W:(i + 1) * W] = ps[i * tm:(i + 1) * tm]
        or_ref[i] = pr[i * tm:(i + 1) * tm]
        om_ref[i] = pm[i * tm:(i + 1) * tm]


def _input_proj(z, g, modsel, ws, wr, wm, ctx_len):
    B, S, D = z.shape
    tm = _token_tile(ctx_len, S - ctx_len)
    nct = ctx_len // tm
    NB = 4
    tok = lambda n: pl.BlockSpec((NB, tm, n), lambda b, i: (b, i, 0))
    full = lambda a: pl.BlockSpec(a.shape, lambda b, i: (0,) * a.ndim)
    return pl.pallas_call(
        _kin_kernel,
        grid=(B // NB, S // tm),
        in_specs=[tok(D), full(g),
                  pl.BlockSpec((NB, 1, N_MOD, D), lambda b, i: (b, jnp.where(i >= nct, 1, 0), 0, 0)),
                  full(ws), full(wr), full(wm)],
        out_specs=[pl.BlockSpec((tm, NB * ws.shape[1]), lambda b, i: (i, b)), tok(wr.shape[1]), tok(wm.shape[1])],
        out_shape=[jax.ShapeDtypeStruct((S, B * ws.shape[1]), F32),
                   jax.ShapeDtypeStruct((B, S, wr.shape[1]), F32),
                   jax.ShapeDtypeStruct((B, S, wm.shape[1]), F32)],
        compiler_params=_params("parallel", "parallel"),
        name="input_proj",
    )(z, g, modsel, ws, wr, wm)


def _cpow(ar, ai, lag, shape, nbits=5):
    pr = jnp.ones(shape, F32)
    pi = jnp.zeros(shape, F32)
    for b in range(nbits):
        bit = ((lag >> b) & 1) == 1
        fr = jnp.where(bit, ar, 1.0)
        fi = jnp.where(bit, ai, 0.0)
        pr, pi = pr * fr - pi * fi, pr * fi + pi * fr
        ar, ai = ar * ar - ai * ai, 2.0 * ar * ai
    return pr, pi


def _ssm_prep_kernel(lc_re_ref, lc_im_ref, lr_re_ref, lr_im_ref, ldt_ref, bt_re_ref, bt_im_ref,
                     ct_re_ref, ct_im_ref, wt_ref, wso_ref, wsi_ref, a_ref):
    C, GC, P = SSM_CHUNK, SSM_GROUP, SSM_STATE
    W = C * GC
    lane = lax.broadcasted_iota(jnp.int32, (1, W), 1)
    quarter = lane // P
    is_re = (quarter == 0) | (quarter == 3)
    jcol = lane // GC
    srow = lax.broadcasted_iota(jnp.int32, (W, 1), 0) // GC
    for d in (0, 1):
        dt = jnp.exp(ldt_ref[d, 0])
        lr, li = lc_re_ref[d, 0], lc_im_ref[d, 0]
        mag = jnp.exp(lr * dt)
        ar, ai = mag * jnp.cos(li * dt), mag * jnp.sin(li * dt)
        cr, ci = ct_re_ref[d, 0], ct_im_ref[d, 0]
        lag_z = jcol if d == 0 else (C - 1) - jcol
        lag_s = jcol + 1 if d == 0 else C - jcol

        def q_of(lag):
            pr, pi = _cpow(ar, ai, lag, (P, W))
            q_re = cr * pr - ci * pi
            q_im = -(cr * pi + ci * pr)
            return q_re, q_im

        qz_re, qz_im = q_of(lag_z)
        qs_re, qs_im = q_of(lag_s)
        lr4, li4 = lr_re_ref[d, 0], lr_im_ref[d, 0]
        mag4 = jnp.exp(lr4 * dt)
        ar4, ai4 = mag4 * jnp.cos(li4 * dt), mag4 * jnp.sin(li4 * dt)
        den = lr4 * lr4 + li4 * li4
        nr, ni = ar4 - 1.0, ai4
        coef_re = (nr * lr4 + ni * li4) / den
        coef_im = (ni * lr4 - nr * li4) / den
        br, bi = bt_re_ref[0], bt_im_ref[0]
        bb_re = coef_re * br - coef_im * bi
        bb_im = coef_re * bi + coef_im * br
        bcat = jnp.where(lane < P, bb_re, bb_im)[:, :2 * P]
        qz = jnp.concatenate([qz_re, qz_im], axis=0)
        z = jnp.dot(bcat, qz, preferred_element_type=F32,
                    precision=lax.Precision.HIGHEST)
        rows = []
        for s in range(C):
            if d == 0:
                sh = pltpu.roll(z, GC * s, axis=1) if s else z
                rows.append(jnp.where(lane >= GC * s, sh, 0.0))
            else:
                m = C - 1 - s
                sh = pltpu.roll(z, W - GC * m, axis=1) if m else z
                rows.append(jnp.where(lane < W - GC * m, sh, 0.0))
        wt_ref[d, 0] = jnp.concatenate(rows, axis=0).astype(BF16)
        wsi_ref[d, 0] = jnp.concatenate([qs_re, qs_im], axis=0).astype(BF16)
        e_row = (C - 1) - srow if d == 0 else srow
        er, ei = _cpow(ar4, ai4, e_row, (W, W), nbits=4)
        bbx = jnp.where(is_re, bb_re, bb_im)
        bby = jnp.where(is_re, -bb_im, bb_re)
        bbx = jnp.concatenate([bbx] * C, axis=0)
        bby = jnp.concatenate([bby] * C, axis=0)
        wso_ref[d, 0] = (er * bbx + ei * bby).astype(BF16)
        cr16, ci16 = _cpow(ar4, ai4, jnp.full((1, W), C, jnp.int32), (1, W))
        a_ref[d, 0, 0:1, :] = cr16
        a_ref[d, 0, 1:2, :] = jnp.where(is_re, -ci16, ci16)


def _ssm_prep(lam_re, lam_im, log_dt, b_re, b_im, c_re, c_im):
    G, P, GC, C = SSM_GROUPS, SSM_STATE, SSM_GROUP, SSM_CHUNK
    W = C * GC
    lc_re, lc_im = lam_re[..., None], lam_im[..., None]
    lr_re = jnp.tile(lam_re, (1, 1, 4))[:, :, None, :]
    lr_im = jnp.tile(lam_im, (1, 1, 4))[:, :, None, :]
    ldt = log_dt[..., None, None]
    bt_re = jnp.tile(jnp.swapaxes(b_re, 1, 2), (1, 1, 4))
    bt_im = jnp.tile(jnp.swapaxes(b_im, 1, 2), (1, 1, 4))
    ct_re = jnp.tile(jnp.swapaxes(c_re, 2, 3), (1, 1, 1, C))
    ct_im = jnp.tile(jnp.swapaxes(c_im, 2, 3), (1, 1, 1, C))
    d4 = lambda a, b: pl.BlockSpec((2, 1, a, b), lambda g: (0, g, 0, 0))
    mat = jax.ShapeDtypeStruct((2, G, W, W), BF16)
    return pl.pallas_call(
        _ssm_prep_kernel,
        grid=(G,),
        in_specs=[d4(P, 1), d4(P, 1), d4(1, W), d4(1, W), d4(1, 1),
                  pl.BlockSpec((1, GC, W), lambda g: (g, 0, 0)),
                  pl.BlockSpec((1, GC, W), lambda g: (g, 0, 0)),
                  d4(P, W), d4(P, W)],
        out_specs=[d4(W, W), d4(W, W), d4(2 * P, W), d4(2, W)],
        out_shape=[mat, mat, jax.ShapeDtypeStruct((2, G, 2 * P, W), BF16),
                   jax.ShapeDtypeStruct((2, G, 2, W), F32)],
        compiler_params=_params("parallel"),
        name="ssm_prep",
    )(lc_re, lc_im, lr_re, lr_im, ldt, bt_re, bt_im, ct_re, ct_im)


def _ssm_kernel(*refs, d, final, KT):
    if final:
        x_ref, ub_ref, yp_ref, wt_ref, wso_ref, wsi_ref, a_ref, dn_ref, o_ref, st_ref = refs
    else:
        x_ref, wt_ref, wso_ref, wsi_ref, a_ref, ub_ref, o_ref, st_ref = refs
    C, GC, P = SSM_CHUNK, SSM_GROUP, SSM_STATE
    BT = x_ref.shape[2]
    GB = x_ref.shape[3] // GC
    R = KT * BT

    @pl.when(pl.program_id(2) == 0)
    def _():
        st_ref[...] = jnp.zeros_like(st_ref)

    gs = range(GB)
    if final:
        ub = [ub_ref[g, 0, 0] for g in gs]
    else:
        xt = [x_ref[:, t].reshape(R, GB * GC) for t in range(C)]
        ub = [jnp.concatenate([xt[t][:, g * GC:(g + 1) * GC] for t in range(C)], axis=1).astype(BF16) for g in gs]
        for g in gs:
            ub_ref[g, 0, 0] = ub[g]
    loc = [_dot(ub[g], wso_ref[0, g]) for g in gs]
    a1 = [a_ref[0, g, 0:1, :] for g in gs]
    a2 = [a_ref[0, g, 1:2, :] for g in gs]
    x = [st_ref[g] for g in gs]
    xs = [[None] * KT for _ in gs]
    for k in (range(KT) if d == 0 else range(KT - 1, -1, -1)):
        for g in gs:
            xs[g][k] = x[g][:, :2 * P]
            x[g] = a1[g] * x[g] + a2[g] * pltpu.roll(x[g], 2 * P, axis=1) + loc[g][k * BT:(k + 1) * BT]
    ys = []
    for g in gs:
        st_ref[g] = x[g]
        xin = jnp.concatenate(xs[g], axis=0).astype(BF16)
        ys.append(_dot(ub[g], wt_ref[0, g]) + _dot(xin, wsi_ref[0, g]))
    if not final:
        for g in gs:
            o_ref[g, 0, 0] = ys[g]
        return
    ys = [ys[g] + yp_ref[g, 0, 0] for g in gs]
    for t in range(C):
        yt = jnp.concatenate([ys[g][:, t * GC:(t + 1) * GC] for g in gs], axis=1)
        o_ref[:, t] = (yt + x_ref[:, t].reshape(R, GB * GC) * dn_ref[...]).reshape(KT, BT, GB * GC)


def _ssm_dir(p_tm, prev, prep, d_skip, d, B, ctx_len):
    S = p_tm.shape[0]
    G, GC, C, P = SSM_GROUPS, SSM_GROUP, SSM_CHUNK, SSM_STATE
    W = C * GC
    LW = 128
    GB = LW // GC
    BT = 16 if B % 16 == 0 else 8
    KT = _token_tile(ctx_len, S - ctx_len) // C
    NT, NTc = S // (C * KT), ctx_len // (C * KT)
    wt, wso, wsi, a4 = prep
    final = prev is not None
    R = KT * BT

    def tile(ti):
        if d == 0:
            return ti
        return jnp.where(ti < NTc, NTc - 1 - ti, NT - 1 - (ti - NTc))

    x4 = p_tm.reshape(S // C, C, B, G * GC)
    xspec = pl.BlockSpec((KT, C, BT, LW), lambda gb, bt, ti: (tile(ti), 0, bt, gb))
    gspec = pl.BlockSpec((GB, 1, 1, R, W), lambda gb, bt, ti: (gb, bt, tile(ti), 0, 0))
    gshape = (G, B // BT, NT, R, W)
    wspec = lambda r: pl.BlockSpec((1, GB, r, W), lambda gb, bt, ti: (d, gb, 0, 0))
    in_specs, args = [xspec], [x4]
    if final:
        in_specs += [gspec, gspec]
        args += list(prev)
    in_specs += [wspec(W), wspec(W), wspec(2 * P), wspec(2)]
    args += [wt, wso, wsi, a4]
    if final:
        in_specs.append(pl.BlockSpec((1, LW), lambda gb, bt, ti: (0, gb)))
        args.append(d_skip.reshape(1, G * GC))
        out_specs, out_shape = xspec, jax.ShapeDtypeStruct(x4.shape, F32)
    else:
        out_specs = [gspec, gspec]
        out_shape = [jax.ShapeDtypeStruct(gshape, BF16), jax.ShapeDtypeStruct(gshape, F32)]
    y = pl.pallas_call(
        functools.partial(_ssm_kernel, d=d, final=final, KT=KT),
        grid=(G // GB, B // BT, NT),
        in_specs=in_specs,
        out_specs=out_specs,
        out_shape=out_shape,
        scratch_shapes=[pltpu.VMEM((GB, BT, W), F32)],
        compiler_params=_params("parallel", "parallel", "arbitrary"),
        name="ssm_rev" if d else "ssm_fwd",
    )(*args)
    return y.reshape(p_tm.shape) if final else y


def _ssm_scan(p_tm, prep, d_skip, B, ctx_len):
    fwd = _ssm_dir(p_tm, None, prep, d_skip, 0, B, ctx_len)
    return _ssm_dir(p_tm, fwd, prep, d_skip, 1, B, ctx_len)


def _rwkv_kernel(*refs, d, final, NC, NCc):
    if final:
        (p_ref, hp_ref, hn_ref, yp_ref, mu_ref, w0_ref, w2_ref, a0_ref, a2_ref, g2_ref, pv_ref,
         o_ref, s_ref, xb_ref, yb_ref) = refs
    else:
        (p_ref, hp_ref, hn_ref, mu_ref, w0_ref, w2_ref, a0_ref, a2_ref, g2_ref, pv_ref,
         o_ref, s_ref, xb_ref, yb_ref) = refs
    L, N, H, W = RWKV_CHUNK, RWKV_HEAD, RWKV_HEADS, RWKV_W
    NB = p_ref.shape[0]
    ci = pl.program_id(1)
    if d == 0:
        c = ci
    else:
        c = jnp.where(ci < NCc, NCc - 1 - ci, NC - 1 - (ci - NCc))

    @pl.when(ci == 0)
    def _():
        s_ref[...] = jnp.zeros_like(s_ref)

    keep_prev = jnp.where((c == 0) | (c == NCc), 0.0, 1.0)
    keep_next = jnp.where((c == NCc - 1) | (c == NC - 1), 0.0, 1.0)
    xs = []
    for bi in range(NB):
        p = p_ref[bi]
        xb_ref[bi, 0:8, :] = hp_ref[bi] * keep_prev
        xb_ref[bi, 8:8 + L, :] = p
        xb_ref[bi, 8 + L:16 + L, :] = hn_ref[bi] * keep_next
        prev = xb_ref[bi, 7:7 + L, :]
        nxt = xb_ref[bi, 9:9 + L, :]
        xs.append(p + mu_ref[...] * (0.5 * (prev + nxt) - p))
    x = jnp.concatenate(xs, axis=0)

    r, k, v = x[:, 0:W], x[:, W:2 * W], x[:, 2 * W:3 * W]
    pw = x[:, 3 * W:3 * W + LORA_W]
    pa = x[:, 3 * W + LORA_W:3 * W + 2 * LORA_W]
    pg = x[:, 3 * W + 2 * LORA_W:3 * W + 3 * LORA_W]
    k_k, k_a, r_k = pv_ref[0:1, :], pv_ref[1:2, :], pv_ref[2:3, :]
    ln_w, ln_b = pv_ref[3:4, :], pv_ref[4:5, :]
    pab = pa.astype(BF16)

    zw = w0_ref[d:d + 1, :] + _dot(jnp.tanh(pw).astype(BF16), w2_ref[d])
    nz = -zw
    softplus = jnp.maximum(nz, 0.0) + jnp.log(1.0 + jnp.exp(-jnp.abs(nz)))
    lw = -jnp.exp(-softplus - 0.5)
    a = _sigmoid(a0_ref[d:d + 1, :] + _dot(pab, a2_ref[d]))
    kd = k * (1.0 + (a - 1.0) * k_a)

    hrow = lax.broadcasted_iota(jnp.int32, (2 * N, 2 * N), 0) // N
    hcol = lax.broadcasted_iota(jnp.int32, (2 * N, 2 * N), 1) // N
    head_ones = jnp.where(hrow == hcol, 1.0, 0.0).astype(BF16)

    kk = k * k_k
    kk = kk * lax.rsqrt(_dot_exact_rhs([kk * kk], head_ones)[0] + 1e-12)
    b = a * kk

    PW2 = 2 * N
    trow = lax.broadcasted_iota(jnp.int32, (L, PW2), 0)
    lane2 = lax.broadcasted_iota(jnp.int32, (L, PW2), 1)
    left = lane2 < N
    tcol = jnp.where(left, lane2, lane2 - N)
    if d == 0:
        strict, incl = tcol < trow, tcol <= trow
    else:
        strict, incl = tcol > trow, tcol >= trow

    def bdiag(t):
        return jnp.concatenate([jnp.where(left, t, 0.0), jnp.where(left, 0.0, t)], axis=0).astype(BF16)

    brow = lax.broadcasted_iota(jnp.int32, (NB * L, NB * L), 0)
    bcol = lax.broadcasted_iota(jnp.int32, (NB * L, NB * L), 1)
    upto = (bcol <= brow) if d == 0 else (bcol >= brow)
    cum = jnp.where(upto, jnp.where((brow // L) == (bcol // L), 1.0, 0.0), 0.0).astype(BF16)
    cs = _dot_exact_lhs(cum, lw, 3)
    last = L - 1 if d == 0 else 0
    cls = [cs[bi * L + last:bi * L + last + 1, :] for bi in range(NB)]
    cl = jnp.concatenate([jnp.broadcast_to(t, (L, W)) for t in cls], axis=0)
    e_to_end = jnp.exp(cl - cs)
    e_neg = jnp.exp(-cs)
    rt = (r * jnp.exp(cs)).astype(BF16)
    at = (kk * jnp.exp(cs - lw)).astype(BF16)
    bt = b * e_neg
    kt = kd * e_neg
    kh = (kd * e_to_end).astype(BF16)
    bh = (b * e_to_end).astype(BF16)
    e_chunk = [jnp.exp(t) for t in cls]
    vb = v.astype(BF16)

    ids = [(bi, j) for bi in range(NB) for j in range(H // 2)]
    n = range(len(ids))
    rs = [slice(bi * L, (bi + 1) * L) for bi, _ in ids]
    ls = [slice(j * PW2, (j + 1) * PW2) for _, j in ids]
    ar = [jnp.concatenate([at[rs[i], ls[i]], rt[rs[i], ls[i]]], axis=0) for i in n]
    s0 = [s_ref[bi, j] for bi, j in ids]
    g_b = [_dot_nt(ar[i], bdiag(bt[rs[i], ls[i]])) for i in n]
    g_k = [_dot_nt(ar[i], bdiag(kt[rs[i], ls[i]])) for i in n]
    g_s = [_dot_nt(ar[i], bdiag(s0[i])) for i in n]
    nab = [jnp.where(strict, g_b[i][:L], 0.0) for i in n]
    mrb = [jnp.where(incl, g_b[i][L:], 0.0).astype(BF16) for i in n]
    nm = [jnp.concatenate([jnp.where(strict, g_k[i][:L], 0.0), jnp.where(incl, g_k[i][L:], 0.0)],
                          axis=0).astype(BF16) for i in n]
    nv = [g_s[i] + _dot(nm[i], bdiag(v[rs[i], ls[i]])) for i in n]
    pm = [-t for t in nab]
    q = [_dot(t.astype(BF16), bdiag(t)) for t in nab]
    steps = int(math.log2(L)) - 1
    for it in range(steps):
        qd = [bdiag(t) for t in q]
        if it + 1 < steps:
            pq = [_dot(jnp.concatenate([pm[i], q[i]], axis=0).astype(BF16), qd[i]) for i in n]
            pm = [pm[i] + q[i] + pq[i][:L] for i in n]
            q = [t[L:] for t in pq]
        else:
            pm = [pm[i] + q[i] + _dot(pm[i].astype(BF16), qd[i]) for i in n]
    u = [nv[i][:L] + _dot(pm[i].astype(BF16), bdiag(nv[i][:L])) for i in n]
    for i in n:
        yb_ref[rs[i], ls[i]] = nv[i][L:] - _dot(mrb[i], bdiag(u[i]))
    for i, (bi, j) in enumerate(ids):
        vu = jnp.concatenate([vb[rs[i], ls[i]], -u[i].astype(BF16)], axis=0)
        kb = jnp.concatenate([kh[rs[i], ls[i]], bh[rs[i], ls[i]]], axis=0)
        full = _dot_tn(vu, kb)
        s_ref[bi, j] = s0[i] * e_chunk[bi][:, ls[i]] + jnp.where(left, full[:L], full[L:])

    if not final:
        o_ref[...] = yb_ref[...].reshape(NB, L, W)
        return
    y = yp_ref[...].reshape(NB * L, W) + yb_ref[...]
    o = 1 - d
    a_o = _sigmoid(a0_ref[o:o + 1, :] + _dot(pab, a2_ref[o]))
    kd_sum = kd + k * (1.0 + (a_o - 1.0) * k_a)
    y_sum, rk_sum = _dot_exact_rhs([y, r * kd_sum * r_k], head_ones)
    dev = y - y_sum * (1.0 / N)
    var = _dot_exact_rhs([dev * dev], head_ones)[0] * (1.0 / N)
    yn = dev * lax.rsqrt(var + GN_EPS) * ln_w + ln_b
    bonus = rk_sum * v
    g = _dot(_sigmoid(pg).astype(BF16), g2_ref[...])
    o_ref[...] = ((yn + bonus) * g).reshape(NB, L, W)


def _rwkv_dir(p_rwkv, y_prev, params, d, ctx_len):
    B, S, PW = p_rwkv.shape
    L, W = RWKV_CHUNK, RWKV_W
    NC, NCc = S // L, ctx_len // L
    NB = 4
    final = y_prev is not None

    def chunk(ci):
        if d == 0:
            return ci
        return jnp.where(ci < NCc, NCc - 1 - ci, NC - 1 - (ci - NCc))

    hb = L // 8
    tok = lambda n: pl.BlockSpec((NB, L, n), lambda b, ci: (b, chunk(ci), 0))
    full = lambda a: pl.BlockSpec(a.shape, lambda b, ci: (0,) * a.ndim)
    in_specs = [tok(PW),
                pl.BlockSpec((NB, 8, PW), lambda b, ci: (b, jnp.maximum(chunk(ci) * hb - 1, 0), 0)),
                pl.BlockSpec((NB, 8, PW), lambda b, ci: (b, jnp.minimum((chunk(ci) + 1) * hb, S // 8 - 1), 0))]
    args = [p_rwkv, p_rwkv, p_rwkv]
    if final:
        in_specs.append(tok(W))
        args.append(y_prev)
    in_specs += [full(a) for a in params]
    args += list(params)
    return pl.pallas_call(
        functools.partial(_rwkv_kernel, d=d, final=final, NC=NC, NCc=NCc),
        grid=(B // NB, NC),
        in_specs=in_specs,
        out_specs=tok(W),
        out_shape=jax.ShapeDtypeStruct((B, S, W), F32),
        scratch_shapes=[pltpu.VMEM((NB, RWKV_HEADS // 2, RWKV_HEAD, 2 * RWKV_HEAD), F32),
                        pltpu.VMEM((NB, L + 16, PW), F32),
                        pltpu.VMEM((NB * L, W), F32)],
        compiler_params=_params("parallel", "arbitrary"),
        name="rwkv_rev" if d else "rwkv_fwd",
    )(*args)


def _rwkv_params(mu, w0, w2, a0, a2, g2, k_k, k_a, r_k, ln_w, ln_b):
    W = RWKV_W
    half = LORA_W // 2

    def pad_dir(w):
        out = jnp.zeros((2, LORA_W, W), F32)
        out = out.at[0, :half].set(w[0]).at[1, half:].set(w[1])
        return out.astype(BF16)

    pv = jnp.zeros((8, W), F32)
    pv = pv.at[0].set(k_k).at[1].set(k_a).at[2].set(r_k.reshape(W)).at[3].set(ln_w).at[4].set(ln_b)
    return (mu.reshape(1, RWKV_IN), w0, pad_dir(w2), a0, pad_dir(a2), g2.astype(BF16), pv)


def _mla_prep_kernel(p_ref, cq_ref, sq_ref, ck_ref, sk_ref, qn_ref, kvn_ref, wq_ref, wk_ref, wv_ref,
                     gn_ref, q_ref, k_ref, v_ref):
    HP = HEAD_PAD
    p = p_ref[0]
    lane = lax.broadcasted_iota(jnp.int32, (1, HP), 1)
    m_nope = jnp.where(lane < MLA_NOPE, 1.0, 0.0)
    m_rope = jnp.where(lane < MLA_NOPE, 0.0, jnp.where(lane < MLA_NOPE + MLA_ROPE, 1.0, 0.0))
    g_q, g_kn, g_kr = gn_ref[0:1, :], gn_ref[1:2, :], gn_ref[2:3, :]

    def rms(x, n):
        return lax.rsqrt(jnp.sum(x * x, axis=-1, keepdims=True) * (1.0 / n) + NORM_EPS)

    xq = p[:, :Q_LORA]
    cq = (xq * rms(xq, Q_LORA) * qn_ref[...]).astype(BF16)
    xkv = p[:, Q_LORA:Q_LORA + KV_LORA]
    ckv = (xkv * rms(xkv, KV_LORA) * kvn_ref[...]).astype(BF16)
    q = _dot(cq, wq_ref[...])
    kn = _dot(ckv, wk_ref[...])
    v_ref[0] = _dot_nt(wv_ref[...], ckv).astype(BF16)

    kr = p[:, Q_LORA + KV_LORA:Q_LORA + KV_LORA + HP]
    krn = kr * rms(kr, MLA_ROPE) * g_kr
    to_rope = pltpu.roll(krn, MLA_NOPE, axis=1)
    sw_a = jnp.where(lane >= 112, 0.0, jnp.where(lane >= 96, pltpu.roll(krn, 80, axis=1), 0.0))
    sw_b = jnp.where(lane >= 112, pltpu.roll(krn, 112, axis=1), 0.0)
    kext = to_rope + sw_a + sw_b
    krot = kext * ck_ref[...] + pltpu.roll(kext, HP - MLA_ROPE, axis=1) * sk_ref[...]

    cq_t, sq_t = cq_ref[...], sq_ref[...]
    for h in range(MLA_HEADS):
        sl = slice(h * HP, (h + 1) * HP)
        qh = q[:, sl]
        scale = m_nope * rms(qh * m_nope, MLA_NOPE) + (1.0 - m_nope) * rms(qh * m_rope, MLA_ROPE)
        qn = qh * scale * g_q
        q_ref[0, :, sl] = (qn * cq_t + pltpu.roll(qn, HP - MLA_ROPE, axis=1) * sq_t).astype(BF16)
        kh = kn[:, sl]
        k_ref[0, :, sl] = (kh * rms(kh, MLA_NOPE) * g_kn + krot).astype(BF16)


def _mla_prep(p_mla, tables, params, ctx_len):
    B, S, PW = p_mla.shape
    tm = _token_tile(ctx_len, S - ctx_len)
    cq_t, sq_t, ck_t, sk_t = tables
    HW = MLA_HEADS * HEAD_PAD
    tok = lambda n: pl.BlockSpec((1, tm, n), lambda b, i: (b, i, 0))
    tab = pl.BlockSpec((tm, HEAD_PAD), lambda b, i: (i, 0))
    full = lambda a: pl.BlockSpec(a.shape, lambda b, i: (0,) * a.ndim)
    return pl.pallas_call(
        _mla_prep_kernel,
        grid=(B, S // tm),
        in_specs=[tok(PW), tab, tab, tab, tab] + [full(a) for a in params],
        out_specs=[tok(HW), tok(HW), pl.BlockSpec((1, MLA_HEADS * MLA_V, tm), lambda b, i: (b, 0, i))],
        out_shape=[jax.ShapeDtypeStruct((B, S, HW), BF16), jax.ShapeDtypeStruct((B, S, HW), BF16),
                   jax.ShapeDtypeStruct((B, MLA_HEADS * MLA_V, S), BF16)],
        compiler_params=_params("parallel", "parallel"),
        name="mla_prep",
    )(p_mla, cq_t, sq_t, ck_t, sk_t, *params)


def _mla_params(q_norm, kv_norm, w_uq, w_ukv, qn_nope, kn_nope, qn_rope, kn_rope):
    H, NP, RP, HP = MLA_HEADS, MLA_NOPE, MLA_ROPE, HEAD_PAD
    half = RP // 2
    swap = jnp.concatenate([jnp.arange(half, RP), jnp.arange(0, half)])
    wq = w_uq.reshape(Q_LORA, H, NP + RP)
    wq = jnp.concatenate([wq, wq[:, :, NP + swap]], axis=-1).reshape(Q_LORA, H * HP)
    wkv = w_ukv.reshape(KV_LORA, H, NP + MLA_V)
    wk = jnp.concatenate([wkv[:, :, :NP], jnp.zeros((KV_LORA, H, HP - NP), F32)], axis=-1)
    wk = wk.reshape(KV_LORA, H * HP)
    wv = wkv[:, :, NP:].reshape(KV_LORA, H * MLA_V)
    gn = jnp.zeros((8, HP), F32)
    gn = gn.at[0].set(jnp.concatenate([qn_nope, qn_rope, qn_rope[swap]]))
    gn = gn.at[1, :NP].set(kn_nope).at[2, :RP].set(kn_rope)
    return (q_norm.reshape(1, Q_LORA), kv_norm.reshape(1, KV_LORA),
            wq.astype(BF16), wk.astype(BF16), wv.T.astype(BF16), gn)


def _rope_tables(ctx_len, seq):
    rows = seq // GRID_W
    axis_dims = MLA_ROPE // 2
    row = jnp.repeat(jnp.arange(rows), GRID_W).astype(F32)
    col = jnp.tile(jnp.arange(GRID_W), rows).astype(F32)
    inv = ROPE_BASE ** (-jnp.arange(0, axis_dims, 2, dtype=F32) / axis_dims)
    ang = jnp.concatenate([row[:, None] * inv, col[:, None] * inv], axis=-1)
    cos = jnp.concatenate([jnp.ones((ctx_len, axis_dims), F32), jnp.cos(ang)], axis=0)
    sin = jnp.concatenate([jnp.zeros((ctx_len, axis_dims), F32), jnp.sin(ang)], axis=0)
    S = ctx_len + seq
    pad = jnp.zeros((S, HEAD_PAD - MLA_NOPE - MLA_ROPE), F32)
    cos_t = jnp.concatenate([jnp.ones((S, MLA_NOPE), F32), cos, cos, pad], axis=-1)
    sin_t = jnp.concatenate([jnp.zeros((S, MLA_NOPE), F32), -sin, sin, pad], axis=-1)
    qs = MLA_SCALE * math.log2(math.e)
    return cos_t * qs, sin_t * qs, cos_t, sin_t


def _attn_kernel(q_ref, k_ref, vt_ref, o_ref, *, nct, ctx_len):
    S = k_ref.shape[1]
    i = pl.program_id(2)

    def attend(nk):
        hs = range(q_ref.shape[2] // HEAD_PAD)
        s = [_dot_nt(k_ref[0, :nk, h * HEAD_PAD:(h + 1) * HEAD_PAD],
                     q_ref[0, :, h * HEAD_PAD:(h + 1) * HEAD_PAD]) for h in hs]
        m = [jnp.max(t, axis=0, keepdims=True) for t in s]
        e = [jnp.exp2(s[h] - m[h]) for h in hs]
        l = [jnp.sum(t, axis=0, keepdims=True) for t in e]
        o = [_dot(vt_ref[0, h * MLA_V:(h + 1) * MLA_V, :nk], e[h].astype(BF16)) / l[h] for h in hs]
        for j in range(len(hs) // 2):
            pair = jnp.concatenate([o[2 * j], o[2 * j + 1]], axis=0)
            o_ref[0, :, j * 2 * MLA_V:(j + 1) * 2 * MLA_V] = pair.T.astype(BF16)

    @pl.when(i < nct)
    def _():
        attend(ctx_len)

    @pl.when(i >= nct)
    def _():
        attend(S)


def _attention(q, k, v, ctx_len):
    B, S, _ = q.shape
    tq = _token_tile(ctx_len, S - ctx_len)
    HS = 8
    return pl.pallas_call(
        functools.partial(_attn_kernel, nct=ctx_len // tq, ctx_len=ctx_len),
        grid=(B, MLA_HEADS // HS, S // tq),
        in_specs=[pl.BlockSpec((1, tq, HS * HEAD_PAD), lambda b, h, i: (b, i, h)),
                  pl.BlockSpec((1, S, HS * HEAD_PAD), lambda b, h, i: (b, 0, h)),
                  pl.BlockSpec((1, HS * MLA_V, S), lambda b, h, i: (b, h, 0))],
        out_specs=pl.BlockSpec((1, tq, HS * MLA_V), lambda b, h, i: (b, i, h)),
        out_shape=jax.ShapeDtypeStruct((B, S, MLA_HEADS * MLA_V), BF16),
        compiler_params=_params("parallel", "parallel", "parallel"),
        name="mla_attention",
    )(q, k, v)


def _merge_kernel(z_ref, g1_ref, g2_ref, mod_ref, ys_ref, yr_ref, ym_ref, wg_ref, wb_ref, wo_ref,
                  glw_ref, glb_ref, rw_ref, zo_ref, h2_ref, lg_ref):
    D, W = D_MODEL, BRANCH_WIDTH
    NB, tm = z_ref.shape[0], z_ref.shape[1]
    ns = range(NB)
    rows = lambda t: jnp.concatenate(t, axis=0)
    z = [z_ref[i] for i in ns]
    m = [mod_ref[i, 0] for i in ns]
    h = rows([_norm_mod(z[i], g1_ref[...], m[i][0:1], m[i][1:2]) for i in ns]).astype(BF16)
    ys = rows([ys_ref[:, i * W:(i + 1) * W] for i in ns])
    ys = 0.5 * ys * (1.0 + jnp.tanh(math.sqrt(2.0 / math.pi) * (ys + 0.044715 * ys * ys * ys)))
    ys = ys * _sigmoid(_dot(ys.astype(BF16), glw_ref[...]) + glb_ref[...])
    branches = (ys.astype(BF16), rows([yr_ref[i] for i in ns]).astype(BF16), rows([ym_ref[i] for i in ns]))
    acc = jnp.zeros((NB * tm, D), F32)
    for j, yj in enumerate(branches):
        gate = _sigmoid(_dot(h, wg_ref[:, j * D:(j + 1) * D]))
        acc = acc + gate * _dot(yj, wb_ref[j])
    mix = _dot(acc.astype(BF16), wo_ref[...])
    zn = [z[i] + m[i][2:3] * mix[i * tm:(i + 1) * tm] for i in ns]
    h2 = rows([_norm_mod(zn[i], g2_ref[...], m[i][3:4], m[i][4:5]) for i in ns])
    E = rw_ref.shape[0]
    rw_rows = jnp.concatenate(_split3(rw_ref[...]), axis=0)
    hp = _split3(h2)
    t0 = _dot_nt(rw_rows, hp[0])
    t1 = _dot_nt(rw_rows[:2 * E], hp[1])
    t2 = _dot_nt(rw_rows[:E], hp[2])
    lg = t0[:E] + t0[E:2 * E] + t0[2 * E:] + t1[:E] + t1[E:] + t2
    for i in ns:
        zo_ref[i] = zn[i]
        h2_ref[i] = hp[0][i * tm:(i + 1) * tm]
        lg_ref[i] = lg[:, i * tm:(i + 1) * tm]


def _merge(z, g1, g2, modsel, ys, yr, ym, wg, wb, wo, glw, glb, rwt, ctx_len):
    B, S, D = z.shape
    tm = _token_tile(ctx_len, S - ctx_len)
    nct = ctx_len // tm
    E = rwt.shape[0]
    NB = 2
    tok = lambda n: pl.BlockSpec((NB, tm, n), lambda b, i: (b, i, 0))
    full = lambda a: pl.BlockSpec(a.shape, lambda b, i: (0,) * a.ndim)
    W = BRANCH_WIDTH
    return pl.pallas_call(
        _merge_kernel,
        grid=(B // NB, S // tm),
        in_specs=[tok(D), full(g1), full(g2),
                  pl.BlockSpec((NB, 1, N_MOD, D), lambda b, i: (b, jnp.where(i >= nct, 1, 0), 0, 0)),
                  pl.BlockSpec((tm, NB * W), lambda b, i: (i, b)),
                  tok(W), tok(W), full(wg), full(wb), full(wo), full(glw), full(glb), full(rwt)],
        out_specs=[tok(D), tok(D), pl.BlockSpec((NB, E, tm), lambda b, i: (b, 0, i))],
        out_shape=[jax.ShapeDtypeStruct((B, S, D), F32), jax.ShapeDtypeStruct((B, S, D), BF16),
                   jax.ShapeDtypeStruct((B, E, S), F32)],
        compiler_params=_params("parallel", "parallel"),
        name="merge",
    )(z, g1, g2, modsel, ys, yr, ym, wg, wb, wo, glw, glb, rwt)


def _route_kernel(lg_ref, slot_ref, gate_ref, *, ctx_len, cap_c, cap_l):
    lg = lg_ref[0]
    E, S = lg.shape
    e = jnp.exp(lg - jnp.max(lg, axis=0, keepdims=True))
    aff = e / jnp.sum(e, axis=0, keepdims=True)
    bits = pltpu.bitcast(aff, jnp.int32)
    lane = lax.broadcasted_iota(jnp.int32, (1, S), 1)
    in_c = lane < ctx_len
    ms_c = jnp.where(in_c, 1.0, 0.0)
    LT = 128
    tr = lax.broadcasted_iota(jnp.int32, (LT, LT), 0)
    tc = lax.broadcasted_iota(jnp.int32, (LT, LT), 1)
    tri = jnp.where(tr <= tc, 1.0, 0.0).astype(BF16)

    def counts(x):
        n_c = jnp.sum(x * ms_c, axis=1, keepdims=True)
        return n_c, jnp.sum(x, axis=1, keepdims=True) - n_c

    def prefix(x):
        tiles = [x[:, t * LT:(t + 1) * LT] for t in range(S // LT)]
        inc = _dot(jnp.concatenate(tiles, axis=0).astype(BF16), tri)
        off = jnp.zeros((E, 1), F32)
        out = []
        for t, xt in enumerate(tiles):
            it = inc[t * E:(t + 1) * E]
            out.append(it - xt + off)
            off = off + it[:, LT - 1:LT]
        n_c, _ = counts(x)
        return jnp.concatenate(out, axis=1) - jnp.where(in_c, 0.0, n_c)

    def body(_, carry):
        lo_c, hi_c, lo_l, hi_l = carry
        mid_c = lo_c + ((hi_c - lo_c + 1) >> 1)
        mid_l = lo_l + ((hi_l - lo_l + 1) >> 1)
        n_c, n_l = counts(jnp.where(bits >= jnp.where(in_c, mid_c, mid_l), 1.0, 0.0))
        ok_c, ok_l = n_c >= cap_c, n_l >= cap_l
        return (jnp.where(ok_c, mid_c, lo_c), jnp.where(ok_c, hi_c, mid_c - 1),
                jnp.where(ok_l, mid_l, lo_l), jnp.where(ok_l, hi_l, mid_l - 1))

    zero = jnp.zeros((E, 1), jnp.int32)
    top = jnp.full((E, 1), 0x7F800000, jnp.int32)
    lo_c, _, lo_l, _ = lax.fori_loop(0, 31, body, (zero, top, zero, top))
    thr = jnp.where(in_c, lo_c, lo_l)
    gt = jnp.where(bits > thr, 1.0, 0.0)
    eq = jnp.where(bits == thr, 1.0, 0.0)
    g_c, g_l = counts(gt)
    need = jnp.where(in_c, cap_c - g_c, cap_l - g_l)
    sel = gt + eq * jnp.where(prefix(eq) < need, 1.0, 0.0)
    rank = prefix(sel) + jnp.where(in_c, 0.0, float(cap_c))
    slot_ref[0] = jnp.where(sel > 0.0, rank, -1.0)
    gate_ref[0] = aff * sel


def _route(logits_t, ctx_len, cap_c, cap_l):
    B, E, S = logits_t.shape
    assert S % 128 == 0
    spec = pl.BlockSpec((1, E, S), lambda b: (b, 0, 0))
    return pl.pallas_call(
        functools.partial(_route_kernel, ctx_len=ctx_len, cap_c=cap_c, cap_l=cap_l),
        grid=(B,),
        in_specs=[spec],
        out_specs=[spec, spec],
        out_shape=[jax.ShapeDtypeStruct((B, E, S), F32)] * 2,
        compiler_params=_params("parallel"),
        name="route",
    )(logits_t)


def _expert_kernel(h_ref, slot_ref, gate_ref, w1_ref, w3_ref, w2_ref, oc_ref, ol_ref, *, cap, cap_c, ctx_len):
    NB, S = h_ref.shape[0], h_ref.shape[1]
    xs, gc = [], []
    for i in range(NB):
        for (c0, c1), (t0, t1) in (((0, cap_c), (0, ctx_len)), ((cap_c, cap), (ctx_len, S))):
            slot = slot_ref[i, 0, :, t0:t1]
            gate = gate_ref[i, 0, :, t0:t1]
            cidx = lax.broadcasted_iota(jnp.int32, (c1 - c0, t1 - t0), 0).astype(F32) + float(c0)
            hit = slot == cidx
            onehot = jnp.where(hit, 1.0, 0.0).astype(BF16)
            xs.append(_dot(onehot, h_ref[i, t0:t1, :]))
            gc.append(jnp.sum(jnp.where(hit, gate, 0.0), axis=1, keepdims=True))
    xs = jnp.concatenate(xs, axis=0).astype(BF16)
    gc = jnp.concatenate(gc, axis=0)
    a1 = _dot(xs, w1_ref[0])
    a3 = _dot(xs, w3_ref[0])
    hid = (a1 * _sigmoid(a1) * a3).astype(BF16)
    y = (_dot(hid, w2_ref[0]) * gc).astype(BF16)
    for i in range(NB):
        oc_ref[i] = y[i * cap:i * cap + cap_c]
        ol_ref[i] = y[i * cap + cap_c:(i + 1) * cap]


def _experts(h2, slot, gate, w1, w3, w2, layer, cap, cap_c, ctx_len):
    B, S, D = h2.shape
    _, E, _, F = w1.shape
    NB = 2
    row = pl.BlockSpec((NB, 1, 1, S), lambda e, b: (b, e, 0, 0))
    return pl.pallas_call(
        functools.partial(_expert_kernel, cap=cap, cap_c=cap_c, ctx_len=ctx_len),
        grid=(E, B // NB),
        in_specs=[pl.BlockSpec((NB, S, D), lambda e, b: (b, 0, 0)), row, row,
                  pl.BlockSpec((None, 1, D, F), lambda e, b: (layer, e, 0, 0)),
                  pl.BlockSpec((None, 1, D, F), lambda e, b: (layer, e, 0, 0)),
                  pl.BlockSpec((None, 1, F, D), lambda e, b: (layer, e, 0, 0))],
        out_specs=[pl.BlockSpec((NB, cap_c, D), lambda e, b: (b, e, 0)),
                   pl.BlockSpec((NB, cap - cap_c, D), lambda e, b: (b, e, 0))],
        out_shape=[jax.ShapeDtypeStruct((B, E * cap_c, D), BF16),
                   jax.ShapeDtypeStruct((B, E * (cap - cap_c), D), BF16)],
        compiler_params=_params("parallel", "parallel"),
        name="experts",
    )(h2, slot.reshape(B, E, 1, S), gate.reshape(B, E, 1, S), w1, w3, w2)


def _scatter_kernel(win_ref, z_ref, mod_ref, sl_ref, yc_ref, yl_ref, o_ref, ycat_ref, *, cap_c, cap_l, nct, ntl, win,
                    tile0):
    sl = sl_ref[0]
    E, tm = sl.shape
    b, i = pl.program_id(0), pl.program_id(1) + tile0

    def onehot_t(lo, n, e):
        cidx = lax.broadcasted_iota(jnp.int32, (n, tm), 0).astype(F32) + lo
        return jnp.where(sl[e:e + 1, :] == cidx, 1.0, 0.0).astype(BF16)

    def finish(onehots, rows):
        acc = _dot_tn(jnp.concatenate(onehots, axis=0), rows)
        o_ref[0] = z_ref[0] + mod_ref[0, 0][5:6] * acc

    if tile0 < nct:
        @pl.when(i < nct)
        def _():
            finish([onehot_t(0.0, cap_c, e) for e in range(E)], yc_ref[0])

    base =(b * ntl + jnp.maximum(i - nct, 0)) * (E + 1)
    fits = win_ref[base + E] == 1

    @pl.when((i >= nct) & fits)
    def _():
        onehots = []
        for e in range(E):
            w = win_ref[base + e]
            r0 = pl.multiple_of(e * cap_l + w, 16)
            ycat_ref[e * win:(e + 1) * win, :] = yl_ref[0, pl.ds(r0, win), :]
            onehots.append(onehot_t((w + cap_c).astype(F32), win, e))
        finish(onehots, ycat_ref[...])

    @pl.when((i >= nct) & jnp.logical_not(fits))
    def _():
        finish([onehot_t(float(cap_c), cap_l, e) for e in range(E)], yl_ref[0])


def _scatter_windows(slot, ctx_len, tm, cap_l, win):
    B, E, S = slot.shape
    ntl = (S - ctx_len) // tm
    cnt = (slot[:, :, ctx_len:] >= 0).reshape(B, E, ntl, tm).sum(-1).astype(jnp.int32)
    lo = jnp.cumsum(cnt, axis=-1) - cnt
    w = jnp.minimum((lo // 16) * 16, cap_l - win)
    fits = jnp.all(lo - w + cnt <= win, axis=1)
    table = jnp.concatenate([jnp.swapaxes(w, 1, 2), fits[..., None].astype(jnp.int32)], axis=-1)
    return table.reshape(-1)


def _scatter(z, modsel, slot, yc, yl, ctx_len, cap_c, win=64, latent_only=False):
    B, S, D = z.shape
    E = slot.shape[1]
    cap_l = yl.shape[1] // E
    tm = _token_tile(ctx_len, S - ctx_len)
    nct = ctx_len // tm
    ntl = (S - ctx_len) // tm
    t0 = nct if latent_only else 0
    win = min(win, cap_l)
    tok = lambda n: pl.BlockSpec((1, tm, n), lambda b, i, w: (b, i + t0, 0))
    whole = lambda a: pl.BlockSpec((1,) + a.shape[1:], lambda b, i, w: (b, 0, 0))
    grid_spec = pltpu.PrefetchScalarGridSpec(
        num_scalar_prefetch=1,
        grid=(B, S // tm - t0),
        in_specs=[tok(D),
                  pl.BlockSpec((1, 1, N_MOD, D), lambda b, i, w: (b, jnp.where(i + t0 >= nct, 1, 0), 0, 0)),
                  pl.BlockSpec((1, E, tm), lambda b, i, w: (b, 0, i + t0)), whole(yc), whole(yl)],
        out_specs=pl.BlockSpec((1, tm, D), lambda b, i, w: (b, i, 0)),
        scratch_shapes=[pltpu.VMEM((E * win, D), BF16)])
    return pl.pallas_call(
        functools.partial(_scatter_kernel, cap_c=cap_c, cap_l=cap_l, nct=nct, ntl=ntl, win=win, tile0=t0),
        grid_spec=grid_spec,
        out_shape=jax.ShapeDtypeStruct((B, S - t0 * tm, D), F32),
        compiler_params=_params("parallel", "parallel"),
        name="moe_scatter",
    )(_scatter_windows(slot, ctx_len, tm, cap_l, win), z, modsel, slot, yc, yl)


def kernel(x, c, ctx, c_ctx, ada_w, ada_b, norm1_g, norm2_g, w_in, ssm_lambda_re, ssm_lambda_im, ssm_log_dt, ssm_b_re, ssm_b_im, ssm_c_re, ssm_c_im, ssm_d, ssm_glu_w, ssm_glu_b, rwkv_mu, rwkv_w0, rwkv_w2, rwkv_a0, rwkv_a2, rwkv_g2, rwkv_k_k, rwkv_k_a, rwkv_r_k, rwkv_ln_w, rwkv_ln_b, mla_q_norm, mla_kv_norm, mla_w_uq, mla_w_ukv, mla_qn_nope, mla_kn_nope, mla_qn_rope, mla_kn_rope, w_branch, w_out, router_w, moe_w1, moe_w3, moe_w2):
    B, T, D = x.shape
    CTX = ctx.shape[1]
    S = CTX + T
    depth = ada_w.shape[0]
    cap_c = EC_CAPACITY * CTX // N_EXPERTS
    cap_l = EC_CAPACITY * T // N_EXPERTS
    assert D == D_MODEL and T % GRID_W == 0 and cap_c % 8 == 0 and cap_l % 8 == 0 and B % 8 == 0

    rows = -(-(B + 1) // 8) * 8
    cc = jnp.concatenate([c, c_ctx[None, :], jnp.zeros((rows - B - 1, D), F32)], axis=0)
    mods = _mods(cc, ada_w, ada_b).reshape(depth, rows, N_MOD, D)
    mod_lat = mods[:, :B]
    mod_ctx = jnp.broadcast_to(mods[:, B:B + 1], mod_lat.shape)
    modsel = jnp.stack([mod_ctx, mod_lat], axis=2)

    tables = _rope_tables(CTX, T)

    z = jnp.concatenate([ctx, x], axis=1)
    w_in_b = w_in.astype(BF16)
    w1_b, w3_b, w2_b = moe_w1.astype(BF16), moe_w3.astype(BF16), moe_w2.astype(BF16)
    for l in range(depth):
        g1, g2 = norm1_g[l].reshape(1, D), norm2_g[l].reshape(1, D)
        wl = w_in_b[l]
        w_ssm = wl[:, :RWKV_OFF]
        w_rwkv = wl[:, RWKV_OFF:MLA_OFF]
        w_mla = jnp.pad(wl[:, MLA_OFF:GATE_OFF], ((0, 0), (0, MLA_IN_PAD - MLA_IN)))
        w_gate = wl[:, GATE_OFF:]
        p_ssm, p_rwkv, p_mla = _input_proj(z, g1, modsel[l], w_ssm, w_rwkv, w_mla, CTX)

        prep = _ssm_prep(ssm_lambda_re[l], ssm_lambda_im[l], ssm_log_dt[l], ssm_b_re[l], ssm_b_im[l],
                         ssm_c_re[l], ssm_c_im[l])
        y_ssm = _ssm_scan(p_ssm, prep, ssm_d[l], B, CTX)

        rp = _rwkv_params(rwkv_mu[l], rwkv_w0[l], rwkv_w2[l], rwkv_a0[l], rwkv_a2[l], rwkv_g2[l],
                          rwkv_k_k[l], rwkv_k_a[l], rwkv_r_k[l], rwkv_ln_w[l], rwkv_ln_b[l])
        y_fwd = _rwkv_dir(p_rwkv, None, rp, 0, CTX)
        y_rwkv = _rwkv_dir(p_rwkv, y_fwd, rp, 1, CTX)

        mp = _mla_params(mla_q_norm[l], mla_kv_norm[l], mla_w_uq[l], mla_w_ukv[l], mla_qn_nope[l],
                         mla_kn_nope[l], mla_qn_rope[l], mla_kn_rope[l])
        q, k, v = _mla_prep(p_mla, tables, mp, CTX)
        y_mla = _attention(q, k, v, CTX)

        z, h2, logits_t = _merge(z, g1, g2, modsel[l], y_ssm, y_rwkv, y_mla, w_gate,
                                 w_branch[l].astype(BF16), w_out[l].astype(BF16),
                                 ssm_glu_w[l].astype(BF16), ssm_glu_b[l].reshape(1, -1),
                                 router_w[l].T, CTX)
        slot, gate = _route(logits_t, CTX, cap_c, cap_l)
        yc, yl = _experts(h2, slot, gate, w1_b, w3_b, w2_b, l, cap_c + cap_l, cap_c, CTX)
        z = _scatter(z, modsel[l], slot, yc, yl, CTX, cap_c, latent_only=(l == depth - 1))
    return z
```

```python
import functools
import math

import jax
import jax.numpy as jnp
from jax import lax
from jax.experimental import pallas as pl
from jax.experimental.pallas import tpu as pltpu

F32 = jnp.float32
BF16 = jnp.bfloat16

D_MODEL = 1024
GRID_W = 64
N_MOD = 6
NORM_EPS = 1e-6
GN_EPS = 64e-5
BRANCH_WIDTH = 512
SSM_GROUP = 16
SSM_GROUPS = BRANCH_WIDTH // SSM_GROUP
SSM_STATE = 64
SSM_CHUNK = 16
RWKV_W = BRANCH_WIDTH
RWKV_HEAD = 64
RWKV_HEADS = RWKV_W // RWKV_HEAD
RWKV_CHUNK = 64
LORA_W = 128
RWKV_IN = 3 * RWKV_W + 3 * LORA_W
MLA_HEADS = 8
MLA_NOPE = 64
MLA_ROPE = 32
MLA_V = 64
Q_LORA = 384
KV_LORA = 256
MLA_IN = Q_LORA + KV_LORA + MLA_ROPE
MLA_IN_PAD = 768
MLA_SCALE = 1.0 / math.sqrt(MLA_NOPE + MLA_ROPE)
ROPE_BASE = 10000.0
HEAD_PAD = 128
RWKV_OFF = BRANCH_WIDTH
MLA_OFF = RWKV_OFF + RWKV_IN
GATE_OFF = MLA_OFF + MLA_IN
N_EXPERTS = 16
EXPERT_FF = 1536
EC_CAPACITY = 2
VMEM_LIMIT = 56 * 1024 * 1024


def _params(*sem):
    return pltpu.CompilerParams(dimension_semantics=sem, vmem_limit_bytes=VMEM_LIMIT)


def _dot(a, b):
    return jnp.dot(a, b, preferred_element_type=F32)


def _dot_nt(a, b):
    return lax.dot_general(a, b, (((1,), (1,)), ((), ())), preferred_element_type=F32)


def _dot_tn(a, b):
    return lax.dot_general(a, b, (((0,), (0,)), ((), ())), preferred_element_type=F32)


def _split2(x):
    hi = x.astype(BF16)
    lo = (x - hi.astype(F32)).astype(BF16)
    return hi, lo


def _split3(x):
    hi = x.astype(BF16)
    r1 = x - hi.astype(F32)
    mid = r1.astype(BF16)
    lo = (r1 - mid.astype(F32)).astype(BF16)
    return hi, mid, lo


def _dot_exact_rhs(xs, m):
    rows, width = xs[0].shape
    k = m.shape[0]
    nt = width // k
    parts = [p[:, j * k:(j + 1) * k] for x in xs for p in _split2(x) for j in range(nt)]
    res = _dot(jnp.concatenate(parts, axis=0), m)
    blk = lambda i: res[i * rows:(i + 1) * rows]
    return [jnp.concatenate([blk((2 * i) * nt + j) + blk((2 * i + 1) * nt + j) for j in range(nt)], axis=1)
            for i in range(len(xs))]


def _dot_exact_lhs(m, x, parts):
    pieces = _split2(x) if parts == 2 else _split3(x)
    acc = _dot(m, pieces[0])
    for p in pieces[1:]:
        acc = acc + _dot(m, p)
    return acc


def _sigmoid(x):
    return 0.5 * jnp.tanh(0.5 * x) + 0.5


def _norm_mod(x, g, shift, scale):
    y = x * lax.rsqrt(jnp.mean(x * x, axis=-1, keepdims=True) + NORM_EPS)
    return (y * g) * (1.0 + scale) + shift


def _token_tile(ctx_len, seq):
    for tm in (256, 128, 64):
        if ctx_len % tm == 0 and seq % tm == 0:
            return tm
    raise ValueError("context and latent lengths must be multiples of 64")


def _mods_kernel(cc_ref, w_ref, b_ref, o_ref):
    c = cc_ref[...]
    s = (c * _sigmoid(c)).astype(BF16)
    o_ref[0] = _dot(s, w_ref[0].astype(BF16)) + b_ref[0]


def _mods(cc, ada_w, ada_b):
    L, D, N = ada_w.shape
    R = cc.shape[0]
    tn = 1536
    return pl.pallas_call(
        _mods_kernel,
        grid=(L, N // tn),
        in_specs=[pl.BlockSpec((R, D), lambda l, j: (0, 0)),
                  pl.BlockSpec((1, D, tn), lambda l, j: (l, 0, j)),
                  pl.BlockSpec((1, 1, tn), lambda l, j: (l, 0, j))],
        out_specs=pl.BlockSpec((1, R, tn), lambda l, j: (l, 0, j)),
        out_shape=jax.ShapeDtypeStruct((L, R, N), F32),
        compiler_params=_params("parallel", "parallel"),
        name="adaln_mods",
    )(cc, ada_w, ada_b.reshape(L, 1, N))


def _kin_kernel(z_ref, g_ref, mod_ref, ws_ref, wr_ref, wm_ref, cq_ref, sq_ref, ck_ref, sk_ref,
                qn_ref, kvn_ref, wq_ref, wk_ref, wv_ref, gn_ref, os_ref, or_ref, q_ref, k_ref, v_ref):
    NB, tm = z_ref.shape[0], z_ref.shape[1]
    W = ws_ref.shape[1]
    h = jnp.concatenate([_norm_mod(z_ref[i], g_ref[...], mod_ref[i, 0][0:1], mod_ref[i, 0][1:2])
                         for i in range(NB)], axis=0).astype(BF16)
    pm = _dot(h, wm_ref[...])
    ps, pr = _dot(h, ws_ref[...]), _dot(h, wr_ref[...])
    rep = lambda t_ref: jnp.concatenate([t_ref[...]] * NB, axis=0)
    q, k, ckv = _mla_project(pm, rep(cq_ref), rep(sq_ref), rep(ck_ref), rep(sk_ref), qn_ref[...], kvn_ref[...],
                             wq_ref[...], wk_ref[...], gn_ref[...])
    for i in range(NB):
        rows = slice(i * tm, (i + 1) * tm)
        os_ref[:, i * W:(i + 1) * W] = ps[rows]
        or_ref[i] = pr[rows]
        q_ref[i] = q[rows]
        k_ref[i] = k[rows]
        v_ref[i] = _dot_nt(wv_ref[...], ckv[rows]).astype(BF16)


def _input_proj(z, g, modsel, ws, wr, wm, tables, mla_params, ctx_len):
    B, S, D = z.shape
    tm = _token_tile(ctx_len, S - ctx_len)
    nct = ctx_len // tm
    NB = 2
    HW, VW = MLA_HEADS * HEAD_PAD, MLA_HEADS * MLA_V
    tok = lambda n: pl.BlockSpec((NB, tm, n), lambda b, i: (b, i, 0))
    tab = pl.BlockSpec((tm, HEAD_PAD), lambda b, i: (i, 0))
    full = lambda a: pl.BlockSpec(a.shape, lambda b, i: (0,) * a.ndim)
    return pl.pallas_call(
        _kin_kernel,
        grid=(B // NB, S // tm),
        in_specs=[tok(D), full(g),
                  pl.BlockSpec((NB, 1, N_MOD, D), lambda b, i: (b, jnp.where(i >= nct, 1, 0), 0, 0)),
                  full(ws), full(wr), full(wm), tab, tab, tab, tab] + [full(a) for a in mla_params],
        out_specs=[pl.BlockSpec((tm, NB * ws.shape[1]), lambda b, i: (i, b)), tok(wr.shape[1]),
                   tok(HW), tok(HW), pl.BlockSpec((NB, VW, tm), lambda b, i: (b, 0, i))],
        out_shape=[jax.ShapeDtypeStruct((S, B * ws.shape[1]), F32),
                   jax.ShapeDtypeStruct((B, S, wr.shape[1]), F32),
                   jax.ShapeDtypeStruct((B, S, HW), BF16), jax.ShapeDtypeStruct((B, S, HW), BF16),
                   jax.ShapeDtypeStruct((B, VW, S), BF16)],
        compiler_params=_params("parallel", "parallel"),
        name="input_proj",
    )(z, g, modsel, ws, wr, wm, *tables, *mla_params)


def _cpow(ar, ai, lag, shape, nbits=5):
    pr = jnp.ones(shape, F32)
    pi = jnp.zeros(shape, F32)
    for b in range(nbits):
        bit = ((lag >> b) & 1) == 1
        fr = jnp.where(bit, ar, 1.0)
        fi = jnp.where(bit, ai, 0.0)
        pr, pi = pr * fr - pi * fi, pr * fi + pi * fr
        ar, ai = ar * ar - ai * ai, 2.0 * ar * ai
    return pr, pi


def _ssm_prep_kernel(lc_re_ref, lc_im_ref, lr_re_ref, lr_im_ref, ldt_ref, bt_re_ref, bt_im_ref,
                     ct_re_ref, ct_im_ref, wt_ref, wso_ref, wsi_ref, a_ref):
    C, GC, P = SSM_CHUNK, SSM_GROUP, SSM_STATE
    W = C * GC
    lane = lax.broadcasted_iota(jnp.int32, (1, W), 1)
    quarter = lane // P
    is_re = (quarter == 0) | (quarter == 3)
    jcol = lane // GC
    srow = lax.broadcasted_iota(jnp.int32, (W, 1), 0) // GC
    for d in (0, 1):
        dt = jnp.exp(ldt_ref[d, 0])
        lr, li = lc_re_ref[d, 0], lc_im_ref[d, 0]
        mag = jnp.exp(lr * dt)
        ar, ai = mag * jnp.cos(li * dt), mag * jnp.sin(li * dt)
        cr, ci = ct_re_ref[d, 0], ct_im_ref[d, 0]
        lag_z = jcol if d == 0 else (C - 1) - jcol
        lag_s = jcol + 1 if d == 0 else C - jcol

        def q_of(lag):
            pr, pi = _cpow(ar, ai, lag, (P, W))
            q_re = cr * pr - ci * pi
            q_im = -(cr * pi + ci * pr)
            return q_re, q_im

        qz_re, qz_im = q_of(lag_z)
        qs_re, qs_im = q_of(lag_s)
        lr4, li4 = lr_re_ref[d, 0], lr_im_ref[d, 0]
        mag4 = jnp.exp(lr4 * dt)
        ar4, ai4 = mag4 * jnp.cos(li4 * dt), mag4 * jnp.sin(li4 * dt)
        den = lr4 * lr4 + li4 * li4
        nr, ni = ar4 - 1.0, ai4
        coef_re = (nr * lr4 + ni * li4) / den
        coef_im = (ni * lr4 - nr * li4) / den
        br, bi = bt_re_ref[0], bt_im_ref[0]
        bb_re = coef_re * br - coef_im * bi
        bb_im = coef_re * bi + coef_im * br
        bcat = jnp.where(lane < P, bb_re, bb_im)[:, :2 * P]
        qz = jnp.concatenate([qz_re, qz_im], axis=0)
        z = jnp.dot(bcat, qz, preferred_element_type=F32,
                    precision=lax.Precision.HIGHEST)
        rows = []
        for s in range(C):
            if d == 0:
                sh = pltpu.roll(z, GC * s, axis=1) if s else z
                rows.append(jnp.where(lane >= GC * s, sh, 0.0))
            else:
                m = C - 1 - s
                sh = pltpu.roll(z, W - GC * m, axis=1) if m else z
                rows.append(jnp.where(lane < W - GC * m, sh, 0.0))
        wt_ref[d, 0] = jnp.concatenate(rows, axis=0).astype(BF16)
        wsi_ref[d, 0] = jnp.concatenate([qs_re, qs_im], axis=0).astype(BF16)
        e_row = (C - 1) - srow if d == 0 else srow
        er, ei = _cpow(ar4, ai4, e_row, (W, W), nbits=4)
        bbx = jnp.where(is_re, bb_re, bb_im)
        bby = jnp.where(is_re, -bb_im, bb_re)
        bbx = jnp.concatenate([bbx] * C, axis=0)
        bby = jnp.concatenate([bby] * C, axis=0)
        wso_ref[d, 0] = (er * bbx + ei * bby).astype(BF16)
        cr16, ci16 = _cpow(ar4, ai4, jnp.full((1, W), C, jnp.int32), (1, W))
        a_ref[d, 0, 0:1, :] = cr16
        a_ref[d, 0, 1:2, :] = jnp.where(is_re, -ci16, ci16)


def _ssm_prep(lam_re, lam_im, log_dt, b_re, b_im, c_re, c_im):
    G, P, GC, C = SSM_GROUPS, SSM_STATE, SSM_GROUP, SSM_CHUNK
    W = C * GC
    lc_re, lc_im = lam_re[..., None], lam_im[..., None]
    lr_re = jnp.tile(lam_re, (1, 1, 4))[:, :, None, :]
    lr_im = jnp.tile(lam_im, (1, 1, 4))[:, :, None, :]
    ldt = log_dt[..., None, None]
    bt_re = jnp.tile(jnp.swapaxes(b_re, 1, 2), (1, 1, 4))
    bt_im = jnp.tile(jnp.swapaxes(b_im, 1, 2), (1, 1, 4))
    ct_re = jnp.tile(jnp.swapaxes(c_re, 2, 3), (1, 1, 1, C))
    ct_im = jnp.tile(jnp.swapaxes(c_im, 2, 3), (1, 1, 1, C))
    d4 = lambda a, b: pl.BlockSpec((2, 1, a, b), lambda g: (0, g, 0, 0))
    mat = jax.ShapeDtypeStruct((2, G, W, W), BF16)
    return pl.pallas_call(
        _ssm_prep_kernel,
        grid=(G,),
        in_specs=[d4(P, 1), d4(P, 1), d4(1, W), d4(1, W), d4(1, 1),
                  pl.BlockSpec((1, GC, W), lambda g: (g, 0, 0)),
                  pl.BlockSpec((1, GC, W), lambda g: (g, 0, 0)),
                  d4(P, W), d4(P, W)],
        out_specs=[d4(W, W), d4(W, W), d4(2 * P, W), d4(2, W)],
        out_shape=[mat, mat, jax.ShapeDtypeStruct((2, G, 2 * P, W), BF16),
                   jax.ShapeDtypeStruct((2, G, 2, W), F32)],
        compiler_params=_params("parallel"),
        name="ssm_prep",
    )(lc_re, lc_im, lr_re, lr_im, ldt, bt_re, bt_im, ct_re, ct_im)


def _ssm_kernel(*refs, d, final, KT):
    if final:
        x_ref, ub_ref, yp_ref, wt_ref, wso_ref, wsi_ref, a_ref, dn_ref, o_ref, st_ref = refs
    else:
        x_ref, wt_ref, wso_ref, wsi_ref, a_ref, ub_ref, o_ref, st_ref = refs
    C, GC, P = SSM_CHUNK, SSM_GROUP, SSM_STATE
    BT = x_ref.shape[2]
    GB = x_ref.shape[3] // GC
    R = KT * BT

    @pl.when(pl.program_id(2) == 0)
    def _():
        st_ref[...] = jnp.zeros_like(st_ref)

    gs = range(GB)
    if final:
        ub = [ub_ref[g, 0, 0] for g in gs]
    else:
        xt = [x_ref[:, t].reshape(R, GB * GC) for t in range(C)]
        ub = [jnp.concatenate([xt[t][:, g * GC:(g + 1) * GC] for t in range(C)], axis=1).astype(BF16) for g in gs]
        for g in gs:
            ub_ref[g, 0, 0] = ub[g]
    loc = [_dot(ub[g], wso_ref[0, g]) for g in gs]
    a1 = [a_ref[0, g, 0:1, :] for g in gs]
    a2 = [a_ref[0, g, 1:2, :] for g in gs]
    x = [st_ref[g] for g in gs]
    xs = [[None] * KT for _ in gs]
    for k in (range(KT) if d == 0 else range(KT - 1, -1, -1)):
        for g in gs:
            xs[g][k] = x[g][:, :2 * P]
            x[g] = a1[g] * x[g] + a2[g] * pltpu.roll(x[g], 2 * P, axis=1) + loc[g][k * BT:(k + 1) * BT]
    ys = []
    for g in gs:
        st_ref[g] = x[g]
        xin = jnp.concatenate(xs[g], axis=0).astype(BF16)
        ys.append(_dot(ub[g], wt_ref[0, g]) + _dot(xin, wsi_ref[0, g]))
    if not final:
        for g in gs:
            o_ref[g, 0, 0] = ys[g]
        return
    ys = [ys[g] + yp_ref[g, 0, 0] for g in gs]
    for t in range(C):
        yt = jnp.concatenate([ys[g][:, t * GC:(t + 1) * GC] for g in gs], axis=1)
        o_ref[:, t] = (yt + x_ref[:, t].reshape(R, GB * GC) * dn_ref[...]).reshape(KT, BT, GB * GC)


def _ssm_dir(p_tm, prev, prep, d_skip, d, B, ctx_len):
    S = p_tm.shape[0]
    G, GC, C, P = SSM_GROUPS, SSM_GROUP, SSM_CHUNK, SSM_STATE
    W = C * GC
    LW = 128
    GB = LW // GC
    BT = 16 if B % 16 == 0 else 8
    KT = _token_tile(ctx_len, S - ctx_len) // C
    NT, NTc = S // (C * KT), ctx_len // (C * KT)
    wt, wso, wsi, a4 = prep
    final = prev is not None
    R = KT * BT

    def tile(ti):
        if d == 0:
            return ti
        return jnp.where(ti < NTc, NTc - 1 - ti, NT - 1 - (ti - NTc))

    x4 = p_tm.reshape(S // C, C, B, G * GC)
    xspec = pl.BlockSpec((KT, C, BT, LW), lambda gb, bt, ti: (tile(ti), 0, bt, gb))
    gspec = pl.BlockSpec((GB, 1, 1, R, W), lambda gb, bt, ti: (gb, bt, tile(ti), 0, 0))
    gshape = (G, B // BT, NT, R, W)
    wspec = lambda r: pl.BlockSpec((1, GB, r, W), lambda gb, bt, ti: (d, gb, 0, 0))
    in_specs, args = [xspec], [x4]
    if final:
        in_specs += [gspec, gspec]
        args += list(prev)
    in_specs += [wspec(W), wspec(W), wspec(2 * P), wspec(2)]
    args += [wt, wso, wsi, a4]
    if final:
        in_specs.append(pl.BlockSpec((1, LW), lambda gb, bt, ti: (0, gb)))
        args.append(d_skip.reshape(1, G * GC))
        out_specs, out_shape = xspec, jax.ShapeDtypeStruct(x4.shape, F32)
    else:
        out_specs = [gspec, gspec]
        out_shape = [jax.ShapeDtypeStruct(gshape, BF16), jax.ShapeDtypeStruct(gshape, F32)]
    y = pl.pallas_call(
        functools.partial(_ssm_kernel, d=d, final=final, KT=KT),
        grid=(G // GB, B // BT, NT),
        in_specs=in_specs,
        out_specs=out_specs,
        out_shape=out_shape,
        scratch_shapes=[pltpu.VMEM((GB, BT, W), F32)],
        compiler_params=_params("parallel", "parallel", "arbitrary"),
        name="ssm_rev" if d else "ssm_fwd",
    )(*args)
    return y.reshape(p_tm.shape) if final else y


def _ssm_scan(p_tm, prep, d_skip, B, ctx_len):
    fwd = _ssm_dir(p_tm, None, prep, d_skip, 0, B, ctx_len)
    return _ssm_dir(p_tm, fwd, prep, d_skip, 1, B, ctx_len)


def _rwkv_kernel(*refs, d, final, NC, NCc):
    if final:
        (p_ref, hp_ref, hn_ref, yp_ref, mu_ref, w0_ref, w2_ref, a0_ref, a2_ref, g2_ref, pv_ref,
         o_ref, s_ref, xb_ref, yb_ref) = refs
    else:
        (p_ref, hp_ref, hn_ref, mu_ref, w0_ref, w2_ref, a0_ref, a2_ref, g2_ref, pv_ref,
         o_ref, s_ref, xb_ref, yb_ref) = refs
    L, N, H, W = RWKV_CHUNK, RWKV_HEAD, RWKV_HEADS, RWKV_W
    NB = p_ref.shape[0]
    ci = pl.program_id(1)
    if d == 0:
        c = ci
    else:
        c = jnp.where(ci < NCc, NCc - 1 - ci, NC - 1 - (ci - NCc))

    @pl.when(ci == 0)
    def _():
        s_ref[...] = jnp.zeros_like(s_ref)

    keep_prev = jnp.where((c == 0) | (c == NCc), 0.0, 1.0)
    keep_next = jnp.where((c == NCc - 1) | (c == NC - 1), 0.0, 1.0)
    xs = []
    for bi in range(NB):
        p = p_ref[bi]
        xb_ref[bi, 0:8, :] = hp_ref[bi] * keep_prev
        xb_ref[bi, 8:8 + L, :] = p
        xb_ref[bi, 8 + L:16 + L, :] = hn_ref[bi] * keep_next
        prev = xb_ref[bi, 7:7 + L, :]
        nxt = xb_ref[bi, 9:9 + L, :]
        xs.append(p + mu_ref[...] * (0.5 * (prev + nxt) - p))
    x = jnp.concatenate(xs, axis=0)

    r, k, v = x[:, 0:W], x[:, W:2 * W], x[:, 2 * W:3 * W]
    pw = x[:, 3 * W:3 * W + LORA_W]
    pa = x[:, 3 * W + LORA_W:3 * W + 2 * LORA_W]
    pg = x[:, 3 * W + 2 * LORA_W:3 * W + 3 * LORA_W]
    k_k, k_a, r_k = pv_ref[0:1, :], pv_ref[1:2, :], pv_ref[2:3, :]
    ln_w, ln_b = pv_ref[3:4, :], pv_ref[4:5, :]
    pab = pa.astype(BF16)

    zw = w0_ref[d:d + 1, :] + _dot(jnp.tanh(pw).astype(BF16), w2_ref[d])
    nz = -zw
    softplus = jnp.maximum(nz, 0.0) + jnp.log(1.0 + jnp.exp(-jnp.abs(nz)))
    lw = -jnp.exp(-softplus - 0.5)
    a = _sigmoid(a0_ref[d:d + 1, :] + _dot(pab, a2_ref[d]))
    kd = k * (1.0 + (a - 1.0) * k_a)

    hrow = lax.broadcasted_iota(jnp.int32, (2 * N, 2 * N), 0) // N
    hcol = lax.broadcasted_iota(jnp.int32, (2 * N, 2 * N), 1) // N
    head_ones = jnp.where(hrow == hcol, 1.0, 0.0).astype(BF16)

    kk = k * k_k
    kk = kk * lax.rsqrt(_dot_exact_rhs([kk * kk], head_ones)[0] + 1e-12)
    b = a * kk

    PW2 = 2 * N
    trow = lax.broadcasted_iota(jnp.int32, (L, PW2), 0)
    lane2 = lax.broadcasted_iota(jnp.int32, (L, PW2), 1)
    left = lane2 < N
    tcol = jnp.where(left, lane2, lane2 - N)
    if d == 0:
        strict, incl = tcol < trow, tcol <= trow
    else:
        strict, incl = tcol > trow, tcol >= trow

    def bdiag(t):
        return jnp.concatenate([jnp.where(left, t, 0.0), jnp.where(left, 0.0, t)], axis=0).astype(BF16)

    brow = lax.broadcasted_iota(jnp.int32, (NB * L, NB * L), 0)
    bcol = lax.broadcasted_iota(jnp.int32, (NB * L, NB * L), 1)
    upto = (bcol <= brow) if d == 0 else (bcol >= brow)
    cum = jnp.where(upto, jnp.where((brow // L) == (bcol // L), 1.0, 0.0), 0.0).astype(BF16)
    cs = _dot_exact_lhs(cum, lw, 3)
    last = L - 1 if d == 0 else 0
    cls = [cs[bi * L + last:bi * L + last + 1, :] for bi in range(NB)]
    cl = jnp.concatenate([jnp.broadcast_to(t, (L, W)) for t in cls], axis=0)
    e_to_end = jnp.exp(cl - cs)
    e_neg = jnp.exp(-cs)
    rt = (r * jnp.exp(cs)).astype(BF16)
    at = (kk * jnp.exp(cs - lw)).astype(BF16)
    bt = b * e_neg
    kt = kd * e_neg
    kh = (kd * e_to_end).astype(BF16)
    bh = (b * e_to_end).astype(BF16)
    e_chunk = [jnp.exp(t) for t in cls]
    vb = v.astype(BF16)

    ids = [(bi, j) for bi in range(NB) for j in range(H // 2)]
    n = range(len(ids))
    rs = [slice(bi * L, (bi + 1) * L) for bi, _ in ids]
    ls = [slice(j * PW2, (j + 1) * PW2) for _, j in ids]
    ar = [jnp.concatenate([at[rs[i], ls[i]], rt[rs[i], ls[i]]], axis=0) for i in n]
    s0 = [s_ref[bi, j] for bi, j in ids]
    g_b = [_dot_nt(ar[i], bdiag(bt[rs[i], ls[i]])) for i in n]
    g_k = [_dot_nt(ar[i], bdiag(kt[rs[i], ls[i]])) for i in n]
    g_s = [_dot_nt(ar[i], bdiag(s0[i])) for i in n]
    nab = [jnp.where(strict, g_b[i][:L], 0.0) for i in n]
    mrb = [jnp.where(incl, g_b[i][L:], 0.0).astype(BF16) for i in n]
    nm = [jnp.concatenate([jnp.where(strict, g_k[i][:L], 0.0), jnp.where(incl, g_k[i][L:], 0.0)],
                          axis=0).astype(BF16) for i in n]
    nv = [g_s[i] + _dot(nm[i], bdiag(v[rs[i], ls[i]])) for i in n]
    pm = [-t for t in nab]
    q = [_dot(t.astype(BF16), bdiag(t)) for t in nab]
    steps = int(math.log2(L)) - 1
    for it in range(steps):
        qd = [bdiag(t) for t in q]
        if it + 1 < steps:
            pq = [_dot(jnp.concatenate([pm[i], q[i]], axis=0).astype(BF16), qd[i]) for i in n]
            pm = [pm[i] + q[i] + pq[i][:L] for i in n]
            q = [t[L:] for t in pq]
        else:
            pm = [pm[i] + q[i] + _dot(pm[i].astype(BF16), qd[i]) for i in n]
    u = [nv[i][:L] + _dot(pm[i].astype(BF16), bdiag(nv[i][:L])) for i in n]
    for i in n:
        yb_ref[rs[i], ls[i]] = nv[i][L:] - _dot(mrb[i], bdiag(u[i]))
    for i, (bi, j) in enumerate(ids):
        vu = jnp.concatenate([vb[rs[i], ls[i]], -u[i].astype(BF16)], axis=0)
        kb = jnp.concatenate([kh[rs[i], ls[i]], bh[rs[i], ls[i]]], axis=0)
        full = _dot_tn(vu, kb)
        s_ref[bi, j] = s0[i] * e_chunk[bi][:, ls[i]] + jnp.where(left, full[:L], full[L:])

    if not final:
        o_ref[...] = yb_ref[...].reshape(NB, L, W)
        return
    y = yp_ref[...].reshape(NB * L, W) + yb_ref[...]
    o = 1 - d
    a_o = _sigmoid(a0_ref[o:o + 1, :] + _dot(pab, a2_ref[o]))
    kd_sum = kd + k * (1.0 + (a_o - 1.0) * k_a)
    y_sum, rk_sum = _dot_exact_rhs([y, r * kd_sum * r_k], head_ones)
    dev = y - y_sum * (1.0 / N)
    var = _dot_exact_rhs([dev * dev], head_ones)[0] * (1.0 / N)
    yn = dev * lax.rsqrt(var + GN_EPS) * ln_w + ln_b
    bonus = rk_sum * v
    g = _dot(_sigmoid(pg).astype(BF16), g2_ref[...])
    o_ref[...] = ((yn + bonus) * g).reshape(NB, L, W)


def _rwkv_dir(p_rwkv, y_prev, params, d, ctx_len):
    B, S, PW = p_rwkv.shape
    L, W = RWKV_CHUNK, RWKV_W
    NC, NCc = S // L, ctx_len // L
    NB = 4
    final = y_prev is not None

    def chunk(ci):
        if d == 0:
            return ci
        return jnp.where(ci < NCc, NCc - 1 - ci, NC - 1 - (ci - NCc))

    hb = L // 8
    tok = lambda n: pl.BlockSpec((NB, L, n), lambda b, ci: (b, chunk(ci), 0))
    full = lambda a: pl.BlockSpec(a.shape, lambda b, ci: (0,) * a.ndim)
    in_specs = [tok(PW),
                pl.BlockSpec((NB, 8, PW), lambda b, ci: (b, jnp.maximum(chunk(ci) * hb - 1, 0), 0)),
                pl.BlockSpec((NB, 8, PW), lambda b, ci: (b, jnp.minimum((chunk(ci) + 1) * hb, S // 8 - 1), 0))]
    args = [p_rwkv, p_rwkv, p_rwkv]
    if final:
        in_specs.append(tok(W))
        args.append(y_prev)
    in_specs += [full(a) for a in params]
    args += list(params)
    return pl.pallas_call(
        functools.partial(_rwkv_kernel, d=d, final=final, NC=NC, NCc=NCc),
        grid=(B // NB, NC),
        in_specs=in_specs,
        out_specs=tok(W),
        out_shape=jax.ShapeDtypeStruct((B, S, W), F32),
        scratch_shapes=[pltpu.VMEM((NB, RWKV_HEADS // 2, RWKV_HEAD, 2 * RWKV_HEAD), F32),
                        pltpu.VMEM((NB, L + 16, PW), F32),
                        pltpu.VMEM((NB * L, W), F32)],
        compiler_params=_params("parallel", "arbitrary"),
        name="rwkv_rev" if d else "rwkv_fwd",
    )(*args)


def _rwkv_params(mu, w0, w2, a0, a2, g2, k_k, k_a, r_k, ln_w, ln_b):
    W = RWKV_W
    half = LORA_W // 2

    def pad_dir(w):
        out = jnp.zeros((2, LORA_W, W), F32)
        out = out.at[0, :half].set(w[0]).at[1, half:].set(w[1])
        return out.astype(BF16)

    pv = jnp.zeros((8, W), F32)
    pv = pv.at[0].set(k_k).at[1].set(k_a).at[2].set(r_k.reshape(W)).at[3].set(ln_w).at[4].set(ln_b)
    return (mu.reshape(1, RWKV_IN), w0, pad_dir(w2), a0, pad_dir(a2), g2.astype(BF16), pv)


def _mla_project(p, cq_t, sq_t, ck_t, sk_t, qn_g, kvn_g, wq, wk, gn):
    HP = HEAD_PAD
    lane = lax.broadcasted_iota(jnp.int32, (1, HP), 1)
    m_nope = jnp.where(lane < MLA_NOPE, 1.0, 0.0)
    m_rope = jnp.where(lane < MLA_NOPE, 0.0, jnp.where(lane < MLA_NOPE + MLA_ROPE, 1.0, 0.0))
    g_q, g_kn, g_kr = gn[0:1, :], gn[1:2, :], gn[2:3, :]

    def rms(x, n):
        return lax.rsqrt(jnp.sum(x * x, axis=-1, keepdims=True) * (1.0 / n) + NORM_EPS)

    xq = p[:, :Q_LORA]
    cq = (xq * rms(xq, Q_LORA) * qn_g).astype(BF16)
    xkv = p[:, Q_LORA:Q_LORA + KV_LORA]
    ckv = (xkv * rms(xkv, KV_LORA) * kvn_g).astype(BF16)
    q = _dot(cq, wq)
    kn = _dot(ckv, wk)

    kr = p[:, Q_LORA + KV_LORA:Q_LORA + KV_LORA + HP]
    krn = kr * rms(kr, MLA_ROPE) * g_kr
    to_rope = pltpu.roll(krn, MLA_NOPE, axis=1)
    sw_a = jnp.where(lane >= 112, 0.0, jnp.where(lane >= 96, pltpu.roll(krn, 80, axis=1), 0.0))
    sw_b = jnp.where(lane >= 112, pltpu.roll(krn, 112, axis=1), 0.0)
    kext = to_rope + sw_a + sw_b
    krot = kext * ck_t + pltpu.roll(kext, HP - MLA_ROPE, axis=1) * sk_t

    qs, ks = [], []
    for h in range(MLA_HEADS):
        sl = slice(h * HP, (h + 1) * HP)
        qh = q[:, sl]
        scale = m_nope * rms(qh * m_nope, MLA_NOPE) + (1.0 - m_nope) * rms(qh * m_rope, MLA_ROPE)
        qn = qh * scale * g_q
        qs.append((qn * cq_t + pltpu.roll(qn, HP - MLA_ROPE, axis=1) * sq_t).astype(BF16))
        kh = kn[:, sl]
        ks.append((kh * rms(kh, MLA_NOPE) * g_kn + krot).astype(BF16))
    return jnp.concatenate(qs, axis=1), jnp.concatenate(ks, axis=1), ckv


def _mla_params(q_norm, kv_norm, w_uq, w_ukv, qn_nope, kn_nope, qn_rope, kn_rope):
    H, NP, RP, HP = MLA_HEADS, MLA_NOPE, MLA_ROPE, HEAD_PAD
    half = RP // 2
    swap = jnp.concatenate([jnp.arange(half, RP), jnp.arange(0, half)])
    wq = w_uq.reshape(Q_LORA, H, NP + RP)
    wq = jnp.concatenate([wq, wq[:, :, NP + swap]], axis=-1).reshape(Q_LORA, H * HP)
    wkv = w_ukv.reshape(KV_LORA, H, NP + MLA_V)
    wk = jnp.concatenate([wkv[:, :, :NP], jnp.zeros((KV_LORA, H, HP - NP), F32)], axis=-1)
    wk = wk.reshape(KV_LORA, H * HP)
    wv = wkv[:, :, NP:].reshape(KV_LORA, H * MLA_V)
    gn = jnp.zeros((8, HP), F32)
    gn = gn.at[0].set(jnp.concatenate([qn_nope, qn_rope, qn_rope[swap]]))
    gn = gn.at[1, :NP].set(kn_nope).at[2, :RP].set(kn_rope)
    return (q_norm.reshape(1, Q_LORA), kv_norm.reshape(1, KV_LORA),
            wq.astype(BF16), wk.astype(BF16), wv.T.astype(BF16), gn)


def _rope_tables(ctx_len, seq):
    rows = seq // GRID_W
    axis_dims = MLA_ROPE // 2
    row = jnp.repeat(jnp.arange(rows), GRID_W).astype(F32)
    col = jnp.tile(jnp.arange(GRID_W), rows).astype(F32)
    inv = ROPE_BASE ** (-jnp.arange(0, axis_dims, 2, dtype=F32) / axis_dims)
    ang = jnp.concatenate([row[:, None] * inv, col[:, None] * inv], axis=-1)
    cos = jnp.concatenate([jnp.ones((ctx_len, axis_dims), F32), jnp.cos(ang)], axis=0)
    sin = jnp.concatenate([jnp.zeros((ctx_len, axis_dims), F32), jnp.sin(ang)], axis=0)
    S = ctx_len + seq
    pad = jnp.zeros((S, HEAD_PAD - MLA_NOPE - MLA_ROPE), F32)
    cos_t = jnp.concatenate([jnp.ones((S, MLA_NOPE), F32), cos, cos, pad], axis=-1)
    sin_t = jnp.concatenate([jnp.zeros((S, MLA_NOPE), F32), -sin, sin, pad], axis=-1)
    qs = MLA_SCALE * math.log2(math.e)
    return cos_t * qs, sin_t * qs, cos_t, sin_t


def _attn_kernel(q_ref, k_ref, vt_ref, o_ref, *, nct, ctx_len):
    S = k_ref.shape[1]
    i = pl.program_id(2)

    def attend(nk):
        hs = range(q_ref.shape[2] // HEAD_PAD)
        s = [_dot_nt(k_ref[0, :nk, h * HEAD_PAD:(h + 1) * HEAD_PAD],
                     q_ref[0, :, h * HEAD_PAD:(h + 1) * HEAD_PAD]) for h in hs]
        m = [jnp.max(t, axis=0, keepdims=True) for t in s]
        e = [jnp.exp2(s[h] - m[h]) for h in hs]
        l = [jnp.sum(t, axis=0, keepdims=True) for t in e]
        o = [_dot(vt_ref[0, h * MLA_V:(h + 1) * MLA_V, :nk], e[h].astype(BF16)) / l[h] for h in hs]
        for j in range(len(hs) // 2):
            pair = jnp.concatenate([o[2 * j], o[2 * j + 1]], axis=0)
            o_ref[0, :, j * 2 * MLA_V:(j + 1) * 2 * MLA_V] = pair.T.astype(BF16)

    @pl.when(i < nct)
    def _():
        attend(ctx_len)

    @pl.when(i >= nct)
    def _():
        attend(S)


def _attention(q, k, v, ctx_len):
    B, S, _ = q.shape
    tq = _token_tile(ctx_len, S - ctx_len)
    HS = 8
    return pl.pallas_call(
        functools.partial(_attn_kernel, nct=ctx_len // tq, ctx_len=ctx_len),
        grid=(B, MLA_HEADS // HS, S // tq),
        in_specs=[pl.BlockSpec((1, tq, HS * HEAD_PAD), lambda b, h, i: (b, i, h)),
                  pl.BlockSpec((1, S, HS * HEAD_PAD), lambda b, h, i: (b, 0, h)),
                  pl.BlockSpec((1, HS * MLA_V, S), lambda b, h, i: (b, h, 0))],
        out_specs=pl.BlockSpec((1, tq, HS * MLA_V), lambda b, h, i: (b, i, h)),
        out_shape=jax.ShapeDtypeStruct((B, S, MLA_HEADS * MLA_V), BF16),
        compiler_params=_params("parallel", "parallel", "parallel"),
        name="mla_attention",
    )(q, k, v)


def _merge_kernel(z_ref, g1_ref, g2_ref, mod_ref, ys_ref, yr_ref, ym_ref, wg_ref, wb_ref, wo_ref,
                  glw_ref, glb_ref, rw_ref, zo_ref, h2_ref, lg_ref):
    D, W = D_MODEL, BRANCH_WIDTH
    NB, tm = z_ref.shape[0], z_ref.shape[1]
    ns = range(NB)
    rows = lambda t: jnp.concatenate(t, axis=0)
    z = [z_ref[i] for i in ns]
    m = [mod_ref[i, 0] for i in ns]
    h = rows([_norm_mod(z[i], g1_ref[...], m[i][0:1], m[i][1:2]) for i in ns]).astype(BF16)
    ys = rows([ys_ref[:, i * W:(i + 1) * W] for i in ns])
    ys = 0.5 * ys * (1.0 + jnp.tanh(math.sqrt(2.0 / math.pi) * (ys + 0.044715 * ys * ys * ys)))
    ys = ys * _sigmoid(_dot(ys.astype(BF16), glw_ref[...]) + glb_ref[...])
    branches = (ys.astype(BF16), rows([yr_ref[i] for i in ns]).astype(BF16), rows([ym_ref[i] for i in ns]))
    acc = jnp.zeros((NB * tm, D), F32)
    for j, yj in enumerate(branches):
        gate = _sigmoid(_dot(h, wg_ref[:, j * D:(j + 1) * D]))
        acc = acc + gate * _dot(yj, wb_ref[j])
    mix = _dot(acc.astype(BF16), wo_ref[...])
    zn = [z[i] + m[i][2:3] * mix[i * tm:(i + 1) * tm] for i in ns]
    h2 = rows([_norm_mod(zn[i], g2_ref[...], m[i][3:4], m[i][4:5]) for i in ns])
    E = rw_ref.shape[0]
    rw_rows = jnp.concatenate(_split3(rw_ref[...]), axis=0)
    hp = _split3(h2)
    t0 = _dot_nt(rw_rows, hp[0])
    t1 = _dot_nt(rw_rows[:2 * E], hp[1])
    t2 = _dot_nt(rw_rows[:E], hp[2])
    lg = t0[:E] + t0[E:2 * E] + t0[2 * E:] + t1[:E] + t1[E:] + t2
    for i in ns:
        zo_ref[i] = zn[i]
        h2_ref[i] = hp[0][i * tm:(i + 1) * tm]
        lg_ref[i] = lg[:, i * tm:(i + 1) * tm]


def _merge(z, g1, g2, modsel, ys, yr, ym, wg, wb, wo, glw, glb, rwt, ctx_len):
    B, S, D = z.shape
    tm = _token_tile(ctx_len, S - ctx_len)
    nct = ctx_len // tm
    E = rwt.shape[0]
    NB = 2
    tok = lambda n: pl.BlockSpec((NB, tm, n), lambda b, i: (b, i, 0))
    full = lambda a: pl.BlockSpec(a.shape, lambda b, i: (0,) * a.ndim)
    W = BRANCH_WIDTH
    return pl.pallas_call(
        _merge_kernel,
        grid=(B // NB, S // tm),
        in_specs=[tok(D), full(g1), full(g2),
                  pl.BlockSpec((NB, 1, N_MOD, D), lambda b, i: (b, jnp.where(i >= nct, 1, 0), 0, 0)),
                  pl.BlockSpec((tm, NB * W), lambda b, i: (i, b)),
                  tok(W), tok(W), full(wg), full(wb), full(wo), full(glw), full(glb), full(rwt)],
        out_specs=[tok(D), tok(D), pl.BlockSpec((NB, E, tm), lambda b, i: (b, 0, i))],
        out_shape=[jax.ShapeDtypeStruct((B, S, D), F32), jax.ShapeDtypeStruct((B, S, D), BF16),
                   jax.ShapeDtypeStruct((B, E, S), F32)],
        compiler_params=_params("parallel", "parallel"),
        name="merge",
    )(z, g1, g2, modsel, ys, yr, ym, wg, wb, wo, glw, glb, rwt)


def _route_kernel(lg_ref, slot_ref, gate_ref, *, ctx_len, cap_c, cap_l):
    lg = lg_ref[0]
    E, S = lg.shape
    e = jnp.exp(lg - jnp.max(lg, axis=0, keepdims=True))
    aff = e / jnp.sum(e, axis=0, keepdims=True)
    bits = pltpu.bitcast(aff, jnp.int32)
    lane = lax.broadcasted_iota(jnp.int32, (1, S), 1)
    in_c = lane < ctx_len
    ms_c = jnp.where(in_c, 1.0, 0.0)
    LT = 128
    tr = lax.broadcasted_iota(jnp.int32, (LT, LT), 0)
    tc = lax.broadcasted_iota(jnp.int32, (LT, LT), 1)
    tri = jnp.where(tr <= tc, 1.0, 0.0).astype(BF16)

    def counts(x):
        n_c = jnp.sum(x * ms_c, axis=1, keepdims=True)
        return n_c, jnp.sum(x, axis=1, keepdims=True) - n_c

    def prefix(x):
        tiles = [x[:, t * LT:(t + 1) * LT] for t in range(S // LT)]
        inc = _dot(jnp.concatenate(tiles, axis=0).astype(BF16), tri)
        off = jnp.zeros((E, 1), F32)
        out = []
        for t, xt in enumerate(tiles):
            it = inc[t * E:(t + 1) * E]
            out.append(it - xt + off)
            off = off + it[:, LT - 1:LT]
        n_c, _ = counts(x)
        return jnp.concatenate(out, axis=1) - jnp.where(in_c, 0.0, n_c)

    def body(_, carry):
        lo_c, hi_c, lo_l, hi_l = carry
        mid_c = lo_c + ((hi_c - lo_c + 1) >> 1)
        mid_l = lo_l + ((hi_l - lo_l + 1) >> 1)
        n_c, n_l = counts(jnp.where(bits >= jnp.where(in_c, mid_c, mid_l), 1.0, 0.0))
        ok_c, ok_l = n_c >= cap_c, n_l >= cap_l
        return (jnp.where(ok_c, mid_c, lo_c), jnp.where(ok_c, hi_c, mid_c - 1),
                jnp.where(ok_l, mid_l, lo_l), jnp.where(ok_l, hi_l, mid_l - 1))

    zero = jnp.zeros((E, 1), jnp.int32)
    top = jnp.full((E, 1), 0x7F800000, jnp.int32)
    lo_c, _, lo_l, _ = lax.fori_loop(0, 31, body, (zero, top, zero, top))
    thr = jnp.where(in_c, lo_c, lo_l)
    gt = jnp.where(bits > thr, 1.0, 0.0)
    eq = jnp.where(bits == thr, 1.0, 0.0)
    g_c, g_l = counts(gt)
    need = jnp.where(in_c, cap_c - g_c, cap_l - g_l)
    sel = gt + eq * jnp.where(prefix(eq) < need, 1.0, 0.0)
    rank = prefix(sel) + jnp.where(in_c, 0.0, float(cap_c))
    slot_ref[0] = jnp.where(sel > 0.0, rank, -1.0)
    gate_ref[0] = aff * sel


def _route(logits_t, ctx_len, cap_c, cap_l):
    B, E, S = logits_t.shape
    assert S % 128 == 0
    spec = pl.BlockSpec((1, E, S), lambda b: (b, 0, 0))
    return pl.pallas_call(
        functools.partial(_route_kernel, ctx_len=ctx_len, cap_c=cap_c, cap_l=cap_l),
        grid=(B,),
        in_specs=[spec],
        out_specs=[spec, spec],
        out_shape=[jax.ShapeDtypeStruct((B, E, S), F32)] * 2,
        compiler_params=_params("parallel"),
        name="route",
    )(logits_t)


def _expert_kernel(h_ref, slot_ref, gate_ref, w1_ref, w3_ref, w2_ref, oc_ref, ol_ref, *, cap, cap_c, ctx_len):
    NB, S = h_ref.shape[0], h_ref.shape[1]
    xs, gc = [], []
    for i in range(NB):
        for (c0, c1), (t0, t1) in (((0, cap_c), (0, ctx_len)), ((cap_c, cap), (ctx_len, S))):
            slot = slot_ref[i, 0, :, t0:t1]
            gate = gate_ref[i, 0, :, t0:t1]
            cidx = lax.broadcasted_iota(jnp.int32, (c1 - c0, t1 - t0), 0).astype(F32) + float(c0)
            hit = slot == cidx
            onehot = jnp.where(hit, 1.0, 0.0).astype(BF16)
            xs.append(_dot(onehot, h_ref[i, t0:t1, :]))
            gc.append(jnp.sum(jnp.where(hit, gate, 0.0), axis=1, keepdims=True))
    xs = jnp.concatenate(xs, axis=0).astype(BF16)
    gc = jnp.concatenate(gc, axis=0)
    a1 = _dot(xs, w1_ref[0])
    a3 = _dot(xs, w3_ref[0])
    hid = (a1 * _sigmoid(a1) * a3).astype(BF16)
    y = (_dot(hid, w2_ref[0]) * gc).astype(BF16)
    for i in range(NB):
        oc_ref[i] = y[i * cap:i * cap + cap_c]
        ol_ref[i] = y[i * cap + cap_c:(i + 1) * cap]


def _experts(h2, slot, gate, w1, w3, w2, layer, cap, cap_c, ctx_len):
    B, S, D = h2.shape
    _, E, _, F = w1.shape
    NB = 2
    row = pl.BlockSpec((NB, 1, 1, S), lambda e, b: (b, e, 0, 0))
    return pl.pallas_call(
        functools.partial(_expert_kernel, cap=cap, cap_c=cap_c, ctx_len=ctx_len),
        grid=(E, B // NB),
        in_specs=[pl.BlockSpec((NB, S, D), lambda e, b: (b, 0, 0)), row, row,
                  pl.BlockSpec((None, 1, D, F), lambda e, b: (layer, e, 0, 0)),
                  pl.BlockSpec((None, 1, D, F), lambda e, b: (layer, e, 0, 0)),
                  pl.BlockSpec((None, 1, F, D), lambda e, b: (layer, e, 0, 0))],
        out_specs=[pl.BlockSpec((NB, cap_c, D), lambda e, b: (b, e, 0)),
                   pl.BlockSpec((NB, cap - cap_c, D), lambda e, b: (b, e, 0))],
        out_shape=[jax.ShapeDtypeStruct((B, E * cap_c, D), BF16),
                   jax.ShapeDtypeStruct((B, E * (cap - cap_c), D), BF16)],
        compiler_params=_params("parallel", "parallel"),
        name="experts",
    )(h2, slot.reshape(B, E, 1, S), gate.reshape(B, E, 1, S), w1, w3, w2)


def _scatter_kernel(win_ref, z_ref, mod_ref, sl_ref, yc_ref, yl_ref, o_ref, ycat_ref, *, cap_c, cap_l, nct, ntl, win,
                    tile0):
    sl = sl_ref[0]
    E, tm = sl.shape
    b, i = pl.program_id(0), pl.program_id(1) + tile0

    def onehot_t(lo, n, e):
        cidx = lax.broadcasted_iota(jnp.int32, (n, tm), 0).astype(F32) + lo
        return jnp.where(sl[e:e + 1, :] == cidx, 1.0, 0.0).astype(BF16)

    def finish(onehots, rows):
        acc = _dot_tn(jnp.concatenate(onehots, axis=0), rows)
        o_ref[0] = z_ref[0] + mod_ref[0, 0][5:6] * acc

    if tile0 < nct:
        @pl.when(i < nct)
        def _():
            finish([onehot_t(0.0, cap_c, e) for e in range(E)], yc_ref[0])

    base =(b * ntl + jnp.maximum(i - nct, 0)) * (E + 1)
    fits = win_ref[base + E] == 1

    @pl.when((i >= nct) & fits)
    def _():
        onehots = []
        for e in range(E):
            w = win_ref[base + e]
            r0 = pl.multiple_of(e * cap_l + w, 16)
            ycat_ref[e * win:(e + 1) * win, :] = yl_ref[0, pl.ds(r0, win), :]
            onehots.append(onehot_t((w + cap_c).astype(F32), win, e))
        finish(onehots, ycat_ref[...])

    @pl.when((i >= nct) & jnp.logical_not(fits))
    def _():
        finish([onehot_t(float(cap_c), cap_l, e) for e in range(E)], yl_ref[0])


def _scatter_windows(slot, ctx_len, tm, cap_l, win):
    B, E, S = slot.shape
    ntl = (S - ctx_len) // tm
    cnt = (slot[:, :, ctx_len:] >= 0).reshape(B, E, ntl, tm).sum(-1).astype(jnp.int32)
    lo = jnp.cumsum(cnt, axis=-1) - cnt
    w = jnp.minimum((lo // 16) * 16, cap_l - win)
    fits = jnp.all(lo - w + cnt <= win, axis=1)
    table = jnp.concatenate([jnp.swapaxes(w, 1, 2), fits[..., None].astype(jnp.int32)], axis=-1)
    return table.reshape(-1)


def _scatter(z, modsel, slot, yc, yl, ctx_len, cap_c, win=64, latent_only=False):
    B, S, D = z.shape
    E = slot.shape[1]
    cap_l = yl.shape[1] // E
    tm = _token_tile(ctx_len, S - ctx_len)
    nct = ctx_len // tm
    ntl = (S - ctx_len) // tm
    t0 = nct if latent_only else 0
    win = min(win, cap_l)
    tok = lambda n: pl.BlockSpec((1, tm, n), lambda b, i, w: (b, i + t0, 0))
    whole = lambda a: pl.BlockSpec((1,) + a.shape[1:], lambda b, i, w: (b, 0, 0))
    grid_spec = pltpu.PrefetchScalarGridSpec(
        num_scalar_prefetch=1,
        grid=(B, S // tm - t0),
        in_specs=[tok(D),
                  pl.BlockSpec((1, 1, N_MOD, D), lambda b, i, w: (b, jnp.where(i + t0 >= nct, 1, 0), 0, 0)),
                  pl.BlockSpec((1, E, tm), lambda b, i, w: (b, 0, i + t0)), whole(yc), whole(yl)],
        out_specs=pl.BlockSpec((1, tm, D), lambda b, i, w: (b, i, 0)),
        scratch_shapes=[pltpu.VMEM((E * win, D), BF16)])
    return pl.pallas_call(
        functools.partial(_scatter_kernel, cap_c=cap_c, cap_l=cap_l, nct=nct, ntl=ntl, win=win, tile0=t0),
        grid_spec=grid_spec,
        out_shape=jax.ShapeDtypeStruct((B, S - t0 * tm, D), F32),
        compiler_params=_params("parallel", "parallel"),
        name="moe_scatter",
    )(_scatter_windows(slot, ctx_len, tm, cap_l, win), z, modsel, slot, yc, yl)


def kernel(x, c, ctx, c_ctx, ada_w, ada_b, norm1_g, norm2_g, w_in, ssm_lambda_re, ssm_lambda_im, ssm_log_dt, ssm_b_re, ssm_b_im, ssm_c_re, ssm_c_im, ssm_d, ssm_glu_w, ssm_glu_b, rwkv_mu, rwkv_w0, rwkv_w2, rwkv_a0, rwkv_a2, rwkv_g2, rwkv_k_k, rwkv_k_a, rwkv_r_k, rwkv_ln_w, rwkv_ln_b, mla_q_norm, mla_kv_norm, mla_w_uq, mla_w_ukv, mla_qn_nope, mla_kn_nope, mla_qn_rope, mla_kn_rope, w_branch, w_out, router_w, moe_w1, moe_w3, moe_w2):
    B, T, D = x.shape
    CTX = ctx.shape[1]
    S = CTX + T
    depth = ada_w.shape[0]
    cap_c = EC_CAPACITY * CTX // N_EXPERTS
    cap_l = EC_CAPACITY * T // N_EXPERTS
    assert D == D_MODEL and T % GRID_W == 0 and cap_c % 8 == 0 and cap_l % 8 == 0 and B % 8 == 0

    rows = -(-(B + 1) // 8) * 8
    cc = jnp.concatenate([c, c_ctx[None, :], jnp.zeros((rows - B - 1, D), F32)], axis=0)
    mods = _mods(cc, ada_w, ada_b).reshape(depth, rows, N_MOD, D)
    mod_lat = mods[:, :B]
    mod_ctx = jnp.broadcast_to(mods[:, B:B + 1], mod_lat.shape)
    modsel = jnp.stack([mod_ctx, mod_lat], axis=2)

    tables = _rope_tables(CTX, T)

    z = jnp.concatenate([ctx, x], axis=1)
    w_in_b = w_in.astype(BF16)
    w1_b, w3_b, w2_b = moe_w1.astype(BF16), moe_w3.astype(BF16), moe_w2.astype(BF16)
    for l in range(depth):
        g1, g2 = norm1_g[l].reshape(1, D), norm2_g[l].reshape(1, D)
        wl = w_in_b[l]
        w_ssm = wl[:, :RWKV_OFF]
        w_rwkv = wl[:, RWKV_OFF:MLA_OFF]
        w_mla = jnp.pad(wl[:, MLA_OFF:GATE_OFF], ((0, 0), (0, MLA_IN_PAD - MLA_IN)))
        w_gate = wl[:, GATE_OFF:]
        mp = _mla_params(mla_q_norm[l], mla_kv_norm[l], mla_w_uq[l], mla_w_ukv[l], mla_qn_nope[l],
                         mla_kn_nope[l], mla_qn_rope[l], mla_kn_rope[l])
        p_ssm, p_rwkv, q, k, v = _input_proj(z, g1, modsel[l], w_ssm, w_rwkv, w_mla, tables, mp, CTX)

        prep = _ssm_prep(ssm_lambda_re[l], ssm_lambda_im[l], ssm_log_dt[l], ssm_b_re[l], ssm_b_im[l],
                         ssm_c_re[l], ssm_c_im[l])
        y_ssm = _ssm_scan(p_ssm, prep, ssm_d[l], B, CTX)

        rp = _rwkv_params(rwkv_mu[l], rwkv_w0[l], rwkv_w2[l], rwkv_a0[l], rwkv_a2[l], rwkv_g2[l],
                          rwkv_k_k[l], rwkv_k_a[l], rwkv_r_k[l], rwkv_ln_w[l], rwkv_ln_b[l])
        y_fwd = _rwkv_dir(p_rwkv, None, rp, 0, CTX)
        y_rwkv = _rwkv_dir(p_rwkv, y_fwd, rp, 1, CTX)

        y_mla = _attention(q, k, v, CTX)

        z, h2, logits_t = _merge(z, g1, g2, modsel[l], y_ssm, y_rwkv, y_mla, w_gate,
                                 w_branch[l].astype(BF16), w_out[l].astype(BF16),
                                 ssm_glu_w[l].astype(BF16), ssm_glu_b[l].reshape(1, -1),
                                 router_w[l].T, CTX)
        slot, gate = _route(logits_t, CTX, cap_c, cap_l)
        yc, yl = _experts(h2, slot, gate, w1_b, w3_b, w2_b, l, cap_c + cap_l, cap_c, CTX)
        z = _scatter(z, modsel[l], slot, yc, yl, CTX, cap_c, latent_only=(l == depth - 1))
    return z
```

```python
import functools
import math

import jax
import jax.numpy as jnp
from jax import lax
from jax.experimental import pallas as pl
from jax.experimental.pallas import tpu as pltpu

F32 = jnp.float32
BF16 = jnp.bfloat16

D_MODEL = 1024
GRID_W = 64
N_MOD = 6
NORM_EPS = 1e-6
GN_EPS = 64e-5
BRANCH_WIDTH = 512
SSM_GROUP = 16
SSM_GROUPS = BRANCH_WIDTH // SSM_GROUP
SSM_STATE = 64
SSM_CHUNK = 16
RWKV_W = BRANCH_WIDTH
RWKV_HEAD = 64
RWKV_HEADS = RWKV_W // RWKV_HEAD
RWKV_CHUNK = 64
LORA_W = 128
RWKV_IN = 3 * RWKV_W + 3 * LORA_W
MLA_HEADS = 8
MLA_NOPE = 64
MLA_ROPE = 32
MLA_V = 64
Q_LORA = 384
KV_LORA = 256
MLA_IN = Q_LORA + KV_LORA + MLA_ROPE
MLA_IN_PAD = 768
MLA_SCALE = 1.0 / math.sqrt(MLA_NOPE + MLA_ROPE)
ROPE_BASE = 10000.0
HEAD_PAD = 128
RWKV_OFF = BRANCH_WIDTH
MLA_OFF = RWKV_OFF + RWKV_IN
GATE_OFF = MLA_OFF + MLA_IN
N_EXPERTS = 16
EXPERT_FF = 1536
EC_CAPACITY = 2
VMEM_LIMIT = 56 * 1024 * 1024


def _params(*sem):
    return pltpu.CompilerParams(dimension_semantics=sem, vmem_limit_bytes=VMEM_LIMIT)


def _dot(a, b):
    return jnp.dot(a, b, preferred_element_type=F32)


def _dot_nt(a, b):
    return lax.dot_general(a, b, (((1,), (1,)), ((), ())), preferred_element_type=F32)


def _dot_tn(a, b):
    return lax.dot_general(a, b, (((0,), (0,)), ((), ())), preferred_element_type=F32)


def _split2(x):
    hi = x.astype(BF16)
    lo = (x - hi.astype(F32)).astype(BF16)
    return hi, lo


def _split3(x):
    hi = x.astype(BF16)
    r1 = x - hi.astype(F32)
    mid = r1.astype(BF16)
    lo = (r1 - mid.astype(F32)).astype(BF16)
    return hi, mid, lo


def _dot_exact_rhs(xs, m):
    rows, width = xs[0].shape
    k = m.shape[0]
    nt = width // k
    parts = [p[:, j * k:(j + 1) * k] for x in xs for p in _split2(x) for j in range(nt)]
    res = _dot(jnp.concatenate(parts, axis=0), m)
    blk = lambda i: res[i * rows:(i + 1) * rows]
    return [jnp.concatenate([blk((2 * i) * nt + j) + blk((2 * i + 1) * nt + j) for j in range(nt)], axis=1)
            for i in range(len(xs))]


def _dot_exact_lhs(m, x, parts):
    pieces = _split2(x) if parts == 2 else _split3(x)
    acc = _dot(m, pieces[0])
    for p in pieces[1:]:
        acc = acc + _dot(m, p)
    return acc


def _sigmoid(x):
    return 0.5 * jnp.tanh(0.5 * x) + 0.5


def _norm_mod(x, g, shift, scale):
    y = x * lax.rsqrt(jnp.mean(x * x, axis=-1, keepdims=True) + NORM_EPS)
    return (y * g) * (1.0 + scale) + shift


def _token_tile(ctx_len, seq):
    for tm in (256, 128, 64):
        if ctx_len % tm == 0 and seq % tm == 0:
            return tm
    raise ValueError("context and latent lengths must be multiples of 64")


def _mods_kernel(cc_ref, w_ref, b_ref, o_ref):
    c = cc_ref[...]
    s = (c * _sigmoid(c)).astype(BF16)
    o_ref[0] = _dot(s, w_ref[0].astype(BF16)) + b_ref[0]


def _mods(cc, ada_w, ada_b):
    L, D, N = ada_w.shape
    R = cc.shape[0]
    tn = 1536
    return pl.pallas_call(
        _mods_kernel,
        grid=(L, N // tn),
        in_specs=[pl.BlockSpec((R, D), lambda l, j: (0, 0)),
                  pl.BlockSpec((1, D, tn), lambda l, j: (l, 0, j)),
                  pl.BlockSpec((1, 1, tn), lambda l, j: (l, 0, j))],
        out_specs=pl.BlockSpec((1, R, tn), lambda l, j: (l, 0, j)),
        out_shape=jax.ShapeDtypeStruct((L, R, N), F32),
        compiler_params=_params("parallel", "parallel"),
        name="adaln_mods",
    )(cc, ada_w, ada_b.reshape(L, 1, N))


def _kin_kernel(z_ref, g_ref, mod_ref, ws_ref, wr_ref, wm_ref, cq_ref, sq_ref, ck_ref, sk_ref,
                qn_ref, kvn_ref, wq_ref, wk_ref, wv_ref, gn_ref, os_ref, or_ref, q_ref, k_ref, v_ref):
    NB, tm = z_ref.shape[0], z_ref.shape[1]
    W = ws_ref.shape[1]
    h = jnp.concatenate([_norm_mod(z_ref[i], g_ref[...], mod_ref[i, 0][0:1], mod_ref[i, 0][1:2])
                         for i in range(NB)], axis=0).astype(BF16)
    pm = _dot(h, wm_ref[...])
    rep = lambda t_ref: jnp.concatenate([t_ref[...]] * NB, axis=0)
    q, k, ckv = _mla_project(pm, rep(cq_ref), rep(sq_ref), rep(ck_ref), rep(sk_ref), qn_ref[...], kvn_ref[...],
                             wq_ref[...], wk_ref[...], gn_ref[...])
    ps, pr = _dot(h, ws_ref[...]), _dot(h, wr_ref[...])
    for i in range(NB):
        rows = slice(i * tm, (i + 1) * tm)
        os_ref[:, i * W:(i + 1) * W] = ps[rows]
        or_ref[i] = pr[rows]
        q_ref[i] = q[rows]
        k_ref[i] = k[rows]
        v_ref[i] = _dot_nt(wv_ref[...], ckv[rows]).astype(BF16)


def _input_proj(z, g, modsel, ws, wr, wm, tables, mla_params, ctx_len):
    B, S, D = z.shape
    tm = _token_tile(ctx_len, S - ctx_len)
    nct = ctx_len // tm
    NB = 2
    HW, VW = MLA_HEADS * HEAD_PAD, MLA_HEADS * MLA_V
    tok = lambda n: pl.BlockSpec((NB, tm, n), lambda b, i: (b, i, 0))
    tab = pl.BlockSpec((tm, HEAD_PAD), lambda b, i: (i, 0))
    full = lambda a: pl.BlockSpec(a.shape, lambda b, i: (0,) * a.ndim)
    return pl.pallas_call(
        _kin_kernel,
        grid=(B // NB, S // tm),
        in_specs=[tok(D), full(g),
                  pl.BlockSpec((NB, 1, N_MOD, D), lambda b, i: (b, jnp.where(i >= nct, 1, 0), 0, 0)),
                  full(ws), full(wr), full(wm), tab, tab, tab, tab] + [full(a) for a in mla_params],
        out_specs=[pl.BlockSpec((tm, NB * ws.shape[1]), lambda b, i: (i, b)), tok(wr.shape[1]),
                   tok(HW), tok(HW), pl.BlockSpec((NB, VW, tm), lambda b, i: (b, 0, i))],
        out_shape=[jax.ShapeDtypeStruct((S, B * ws.shape[1]), F32),
                   jax.ShapeDtypeStruct((B, S, wr.shape[1]), F32),
                   jax.ShapeDtypeStruct((B, S, HW), BF16), jax.ShapeDtypeStruct((B, S, HW), BF16),
                   jax.ShapeDtypeStruct((B, VW, S), BF16)],
        compiler_params=_params("parallel", "parallel"),
        name="input_proj",
    )(z, g, modsel, ws, wr, wm, *tables, *mla_params)


def _cpow(ar, ai, lag, shape, nbits=5):
    pr = jnp.ones(shape, F32)
    pi = jnp.zeros(shape, F32)
    for b in range(nbits):
        bit = ((lag >> b) & 1) == 1
        fr = jnp.where(bit, ar, 1.0)
        fi = jnp.where(bit, ai, 0.0)
        pr, pi = pr * fr - pi * fi, pr * fi + pi * fr
        ar, ai = ar * ar - ai * ai, 2.0 * ar * ai
    return pr, pi


def _ssm_prep_kernel(lc_re_ref, lc_im_ref, lr_re_ref, lr_im_ref, ldt_ref, bt_re_ref, bt_im_ref,
                     ct_re_ref, ct_im_ref, wt_ref, wso_ref, wsi_ref, a_ref):
    C, GC, P = SSM_CHUNK, SSM_GROUP, SSM_STATE
    W = C * GC
    lane = lax.broadcasted_iota(jnp.int32, (1, W), 1)
    quarter = lane // P
    is_re = (quarter == 0) | (quarter == 3)
    jcol = lane // GC
    srow = lax.broadcasted_iota(jnp.int32, (W, 1), 0) // GC
    for d in (0, 1):
        dt = jnp.exp(ldt_ref[d, 0])
        lr, li = lc_re_ref[d, 0], lc_im_ref[d, 0]
        mag = jnp.exp(lr * dt)
        ar, ai = mag * jnp.cos(li * dt), mag * jnp.sin(li * dt)
        cr, ci = ct_re_ref[d, 0], ct_im_ref[d, 0]
        lag_z = jcol if d == 0 else (C - 1) - jcol
        lag_s = jcol + 1 if d == 0 else C - jcol

        def q_of(lag):
            pr, pi = _cpow(ar, ai, lag, (P, W))
            q_re = cr * pr - ci * pi
            q_im = -(cr * pi + ci * pr)
            return q_re, q_im

        qz_re, qz_im = q_of(lag_z)
        qs_re, qs_im = q_of(lag_s)
        lr4, li4 = lr_re_ref[d, 0], lr_im_ref[d, 0]
        mag4 = jnp.exp(lr4 * dt)
        ar4, ai4 = mag4 * jnp.cos(li4 * dt), mag4 * jnp.sin(li4 * dt)
        den = lr4 * lr4 + li4 * li4
        nr, ni = ar4 - 1.0, ai4
        coef_re = (nr * lr4 + ni * li4) / den
        coef_im = (ni * lr4 - nr * li4) / den
        br, bi = bt_re_ref[0], bt_im_ref[0]
        bb_re = coef_re * br - coef_im * bi
        bb_im = coef_re * bi + coef_im * br
        bcat = jnp.where(lane < P, bb_re, bb_im)[:, :2 * P]
        qz = jnp.concatenate([qz_re, qz_im], axis=0)
        z = jnp.dot(bcat, qz, preferred_element_type=F32,
                    precision=lax.Precision.HIGHEST)
        rows = []
        for s in range(C):
            if d == 0:
                sh = pltpu.roll(z, GC * s, axis=1) if s else z
                rows.append(jnp.where(lane >= GC * s, sh, 0.0))
            else:
                m = C - 1 - s
                sh = pltpu.roll(z, W - GC * m, axis=1) if m else z
                rows.append(jnp.where(lane < W - GC * m, sh, 0.0))
        wt_ref[d, 0] = jnp.concatenate(rows, axis=0).astype(BF16)
        wsi_ref[d, 0] = jnp.concatenate([qs_re, qs_im], axis=0).astype(BF16)
        e_row = (C - 1) - srow if d == 0 else srow
        er, ei = _cpow(ar4, ai4, e_row, (W, W), nbits=4)
        bbx = jnp.where(is_re, bb_re, bb_im)
        bby = jnp.where(is_re, -bb_im, bb_re)
        bbx = jnp.concatenate([bbx] * C, axis=0)
        bby = jnp.concatenate([bby] * C, axis=0)
        wso_ref[d, 0] = (er * bbx + ei * bby).astype(BF16)
        cr16, ci16 = _cpow(ar4, ai4, jnp.full((1, W), C, jnp.int32), (1, W))
        a_ref[d, 0, 0:1, :] = cr16
        a_ref[d, 0, 1:2, :] = jnp.where(is_re, -ci16, ci16)


def _ssm_prep(lam_re, lam_im, log_dt, b_re, b_im, c_re, c_im):
    G, P, GC, C = SSM_GROUPS, SSM_STATE, SSM_GROUP, SSM_CHUNK
    W = C * GC
    lc_re, lc_im = lam_re[..., None], lam_im[..., None]
    lr_re = jnp.tile(lam_re, (1, 1, 4))[:, :, None, :]
    lr_im = jnp.tile(lam_im, (1, 1, 4))[:, :, None, :]
    ldt = log_dt[..., None, None]
    bt_re = jnp.tile(jnp.swapaxes(b_re, 1, 2), (1, 1, 4))
    bt_im = jnp.tile(jnp.swapaxes(b_im, 1, 2), (1, 1, 4))
    ct_re = jnp.tile(jnp.swapaxes(c_re, 2, 3), (1, 1, 1, C))
    ct_im = jnp.tile(jnp.swapaxes(c_im, 2, 3), (1, 1, 1, C))
    d4 = lambda a, b: pl.BlockSpec((2, 1, a, b), lambda g: (0, g, 0, 0))
    mat = jax.ShapeDtypeStruct((2, G, W, W), BF16)
    return pl.pallas_call(
        _ssm_prep_kernel,
        grid=(G,),
        in_specs=[d4(P, 1), d4(P, 1), d4(1, W), d4(1, W), d4(1, 1),
                  pl.BlockSpec((1, GC, W), lambda g: (g, 0, 0)),
                  pl.BlockSpec((1, GC, W), lambda g: (g, 0, 0)),
                  d4(P, W), d4(P, W)],
        out_specs=[d4(W, W), d4(W, W), d4(2 * P, W), d4(2, W)],
        out_shape=[mat, mat, jax.ShapeDtypeStruct((2, G, 2 * P, W), BF16),
                   jax.ShapeDtypeStruct((2, G, 2, W), F32)],
        compiler_params=_params("parallel"),
        name="ssm_prep",
    )(lc_re, lc_im, lr_re, lr_im, ldt, bt_re, bt_im, ct_re, ct_im)


def _ssm_kernel(*refs, d, final, KT):
    if final:
        x_ref, ub_ref, yp_ref, wt_ref, wso_ref, wsi_ref, a_ref, dn_ref, o_ref, st_ref = refs
    else:
        x_ref, wt_ref, wso_ref, wsi_ref, a_ref, ub_ref, o_ref, st_ref = refs
    C, GC, P = SSM_CHUNK, SSM_GROUP, SSM_STATE
    BT = x_ref.shape[2]
    GB = x_ref.shape[3] // GC
    R = KT * BT

    @pl.when(pl.program_id(2) == 0)
    def _():
        st_ref[...] = jnp.zeros_like(st_ref)

    gs = range(GB)
    if final:
        ub = [ub_ref[g, 0, 0] for g in gs]
    else:
        xt = [x_ref[:, t].reshape(R, GB * GC) for t in range(C)]
        ub = [jnp.concatenate([xt[t][:, g * GC:(g + 1) * GC] for t in range(C)], axis=1).astype(BF16) for g in gs]
        for g in gs:
            ub_ref[g, 0, 0] = ub[g]
    loc = [_dot(ub[g], wso_ref[0, g]) for g in gs]
    a1 = [a_ref[0, g, 0:1, :] for g in gs]
    a2 = [a_ref[0, g, 1:2, :] for g in gs]
    x = [st_ref[g] for g in gs]
    xs = [[None] * KT for _ in gs]
    for k in (range(KT) if d == 0 else range(KT - 1, -1, -1)):
        for g in gs:
            xs[g][k] = x[g][:, :2 * P]
            x[g] = a1[g] * x[g] + a2[g] * pltpu.roll(x[g], 2 * P, axis=1) + loc[g][k * BT:(k + 1) * BT]
    ys = []
    for g in gs:
        st_ref[g] = x[g]
        xin = jnp.concatenate(xs[g], axis=0).astype(BF16)
        ys.append(_dot(ub[g], wt_ref[0, g]) + _dot(xin, wsi_ref[0, g]))
    if not final:
        for g in gs:
            o_ref[g, 0, 0] = ys[g]
        return
    ys = [ys[g] + yp_ref[g, 0, 0] for g in gs]
    for t in range(C):
        yt = jnp.concatenate([ys[g][:, t * GC:(t + 1) * GC] for g in gs], axis=1)
        o_ref[:, t] = (yt + x_ref[:, t].reshape(R, GB * GC) * dn_ref[...]).reshape(KT, BT, GB * GC)


def _ssm_dir(p_tm, prev, prep, d_skip, d, B, ctx_len):
    S = p_tm.shape[0]
    G, GC, C, P = SSM_GROUPS, SSM_GROUP, SSM_CHUNK, SSM_STATE
    W = C * GC
    LW = 128
    GB = LW // GC
    BT = 16 if B % 16 == 0 else 8
    KT = _token_tile(ctx_len, S - ctx_len) // C
    NT, NTc = S // (C * KT), ctx_len // (C * KT)
    wt, wso, wsi, a4 = prep
    final = prev is not None
    R = KT * BT

    def tile(ti):
        if d == 0:
            return ti
        return jnp.where(ti < NTc, NTc - 1 - ti, NT - 1 - (ti - NTc))

    x4 = p_tm.reshape(S // C, C, B, G * GC)
    xspec = pl.BlockSpec((KT, C, BT, LW), lambda gb, bt, ti: (tile(ti), 0, bt, gb))
    gspec = pl.BlockSpec((GB, 1, 1, R, W), lambda gb, bt, ti: (gb, bt, tile(ti), 0, 0))
    gshape = (G, B // BT, NT, R, W)
    wspec = lambda r: pl.BlockSpec((1, GB, r, W), lambda gb, bt, ti: (d, gb, 0, 0))
    in_specs, args = [xspec], [x4]
    if final:
        in_specs += [gspec, gspec]
        args += list(prev)
    in_specs += [wspec(W), wspec(W), wspec(2 * P), wspec(2)]
    args += [wt, wso, wsi, a4]
    if final:
        in_specs.append(pl.BlockSpec((1, LW), lambda gb, bt, ti: (0, gb)))
        args.append(d_skip.reshape(1, G * GC))
        out_specs, out_shape = xspec, jax.ShapeDtypeStruct(x4.shape, F32)
    else:
        out_specs = [gspec, gspec]
        out_shape = [jax.ShapeDtypeStruct(gshape, BF16), jax.ShapeDtypeStruct(gshape, F32)]
    y = pl.pallas_call(
        functools.partial(_ssm_kernel, d=d, final=final, KT=KT),
        grid=(G // GB, B // BT, NT),
        in_specs=in_specs,
        out_specs=out_specs,
        out_shape=out_shape,
        scratch_shapes=[pltpu.VMEM((GB, BT, W), F32)],
        compiler_params=_params("parallel", "parallel", "arbitrary"),
        name="ssm_rev" if d else "ssm_fwd",
    )(*args)
    return y.reshape(p_tm.shape) if final else y


def _ssm_scan(p_tm, prep, d_skip, B, ctx_len):
    fwd = _ssm_dir(p_tm, None, prep, d_skip, 0, B, ctx_len)
    return _ssm_dir(p_tm, fwd, prep, d_skip, 1, B, ctx_len)


def _rwkv_kernel(*refs, d, final, NC, NCc):
    if final:
        (p_ref, hp_ref, hn_ref, yp_ref, mu_ref, w0_ref, w2_ref, a0_ref, a2_ref, g2_ref, pv_ref,
         o_ref, s_ref, xb_ref, yb_ref) = refs
    else:
        (p_ref, hp_ref, hn_ref, mu_ref, w0_ref, w2_ref, a0_ref, a2_ref, g2_ref, pv_ref,
         o_ref, s_ref, xb_ref, yb_ref) = refs
    L, N, H, W = RWKV_CHUNK, RWKV_HEAD, RWKV_HEADS, RWKV_W
    NB = p_ref.shape[0]
    ci = pl.program_id(1)
    if d == 0:
        c = ci
    else:
        c = jnp.where(ci < NCc, NCc - 1 - ci, NC - 1 - (ci - NCc))

    @pl.when(ci == 0)
    def _():
        s_ref[...] = jnp.zeros_like(s_ref)

    keep_prev = jnp.where((c == 0) | (c == NCc), 0.0, 1.0)
    keep_next = jnp.where((c == NCc - 1) | (c == NC - 1), 0.0, 1.0)
    xs = []
    for bi in range(NB):
        p = p_ref[bi]
        xb_ref[bi, 0:8, :] = hp_ref[bi] * keep_prev
        xb_ref[bi, 8:8 + L, :] = p
        xb_ref[bi, 8 + L:16 + L, :] = hn_ref[bi] * keep_next
        prev = xb_ref[bi, 7:7 + L, :]
        nxt = xb_ref[bi, 9:9 + L, :]
        xs.append(p + mu_ref[...] * (0.5 * (prev + nxt) - p))
    x = jnp.concatenate(xs, axis=0)

    r, k, v = x[:, 0:W], x[:, W:2 * W], x[:, 2 * W:3 * W]
    pw = x[:, 3 * W:3 * W + LORA_W]
    pa = x[:, 3 * W + LORA_W:3 * W + 2 * LORA_W]
    pg = x[:, 3 * W + 2 * LORA_W:3 * W + 3 * LORA_W]
    k_k, k_a, r_k = pv_ref[0:1, :], pv_ref[1:2, :], pv_ref[2:3, :]
    ln_w, ln_b = pv_ref[3:4, :], pv_ref[4:5, :]
    pab = pa.astype(BF16)

    zw = w0_ref[d:d + 1, :] + _dot(jnp.tanh(pw).astype(BF16), w2_ref[d])
    nz = -zw
    softplus = jnp.maximum(nz, 0.0) + jnp.log(1.0 + jnp.exp(-jnp.abs(nz)))
    lw = -jnp.exp(-softplus - 0.5)
    a = _sigmoid(a0_ref[d:d + 1, :] + _dot(pab, a2_ref[d]))
    kd = k * (1.0 + (a - 1.0) * k_a)

    hrow = lax.broadcasted_iota(jnp.int32, (2 * N, 2 * N), 0) // N
    hcol = lax.broadcasted_iota(jnp.int32, (2 * N, 2 * N), 1) // N
    head_ones = jnp.where(hrow == hcol, 1.0, 0.0).astype(BF16)

    kk = k * k_k
    kk = kk * lax.rsqrt(_dot_exact_rhs([kk * kk], head_ones)[0] + 1e-12)
    b = a * kk

    PW2 = 2 * N
    trow = lax.broadcasted_iota(jnp.int32, (L, PW2), 0)
    lane2 = lax.broadcasted_iota(jnp.int32, (L, PW2), 1)
    left = lane2 < N
    tcol = jnp.where(left, lane2, lane2 - N)
    if d == 0:
        strict, incl = tcol < trow, tcol <= trow
    else:
        strict, incl = tcol > trow, tcol >= trow

    def bdiag(t):
        return jnp.concatenate([jnp.where(left, t, 0.0), jnp.where(left, 0.0, t)], axis=0).astype(BF16)

    brow = lax.broadcasted_iota(jnp.int32, (NB * L, NB * L), 0)
    bcol = lax.broadcasted_iota(jnp.int32, (NB * L, NB * L), 1)
    upto = (bcol <= brow) if d == 0 else (bcol >= brow)
    cum = jnp.where(upto, jnp.where((brow // L) == (bcol // L), 1.0, 0.0), 0.0).astype(BF16)
    cs = _dot_exact_lhs(cum, lw, 3)
    last = L - 1 if d == 0 else 0
    cls = [cs[bi * L + last:bi * L + last + 1, :] for bi in range(NB)]
    cl = jnp.concatenate([jnp.broadcast_to(t, (L, W)) for t in cls], axis=0)
    e_to_end = jnp.exp(cl - cs)
    e_neg = jnp.exp(-cs)
    rt = (r * jnp.exp(cs)).astype(BF16)
    at = (kk * jnp.exp(cs - lw)).astype(BF16)
    bt = b * e_neg
    kt = kd * e_neg
    kh = (kd * e_to_end).astype(BF16)
    bh = (b * e_to_end).astype(BF16)
    e_chunk = [jnp.exp(t) for t in cls]
    vb = v.astype(BF16)

    ids = [(bi, j) for bi in range(NB) for j in range(H // 2)]
    n = range(len(ids))
    rs = [slice(bi * L, (bi + 1) * L) for bi, _ in ids]
    ls = [slice(j * PW2, (j + 1) * PW2) for _, j in ids]
    ar = [jnp.concatenate([at[rs[i], ls[i]], rt[rs[i], ls[i]]], axis=0) for i in n]
    s0 = [s_ref[bi, j] for bi, j in ids]
    g_b = [_dot_nt(ar[i], bdiag(bt[rs[i], ls[i]])) for i in n]
    g_k = [_dot_nt(ar[i], bdiag(kt[rs[i], ls[i]])) for i in n]
    g_s = [_dot_nt(ar[i], bdiag(s0[i])) for i in n]
    nab = [jnp.where(strict, g_b[i][:L], 0.0) for i in n]
    mrb = [jnp.where(incl, g_b[i][L:], 0.0).astype(BF16) for i in n]
    nm = [jnp.concatenate([jnp.where(strict, g_k[i][:L], 0.0), jnp.where(incl, g_k[i][L:], 0.0)],
                          axis=0).astype(BF16) for i in n]
    nv = [g_s[i] + _dot(nm[i], bdiag(v[rs[i], ls[i]])) for i in n]
    pm = [-t for t in nab]
    q = [_dot(t.astype(BF16), bdiag(t)) for t in nab]
    steps = int(math.log2(L)) - 1
    for it in range(steps):
        qd = [bdiag(t) for t in q]
        if it + 1 < steps:
            pq = [_dot(jnp.concatenate([pm[i], q[i]], axis=0).astype(BF16), qd[i]) for i in n]
            pm = [pm[i] + q[i] + pq[i][:L] for i in n]
            q = [t[L:] for t in pq]
        else:
            pm = [pm[i] + q[i] + _dot(pm[i].astype(BF16), qd[i]) for i in n]
    u = [nv[i][:L] + _dot(pm[i].astype(BF16), bdiag(nv[i][:L])) for i in n]
    for i in n:
        yb_ref[rs[i], ls[i]] = nv[i][L:] - _dot(mrb[i], bdiag(u[i]))
    for i, (bi, j) in enumerate(ids):
        vu = jnp.concatenate([vb[rs[i], ls[i]], -u[i].astype(BF16)], axis=0)
        kb = jnp.concatenate([kh[rs[i], ls[i]], bh[rs[i], ls[i]]], axis=0)
        full = _dot_tn(vu, kb)
        s_ref[bi, j] = s0[i] * e_chunk[bi][:, ls[i]] + jnp.where(left, full[:L], full[L:])

    if not final:
        o_ref[...] = yb_ref[...].reshape(NB, L, W)
        return
    y = yp_ref[...].reshape(NB * L, W) + yb_ref[...]
    o = 1 - d
    a_o = _sigmoid(a0_ref[o:o + 1, :] + _dot(pab, a2_ref[o]))
    kd_sum = kd + k * (1.0 + (a_o - 1.0) * k_a)
    y_sum, rk_sum = _dot_exact_rhs([y, r * kd_sum * r_k], head_ones)
    dev = y - y_sum * (1.0 / N)
    var = _dot_exact_rhs([dev * dev], head_ones)[0] * (1.0 / N)
    yn = dev * lax.rsqrt(var + GN_EPS) * ln_w + ln_b
    bonus = rk_sum * v
    g = _dot(_sigmoid(pg).astype(BF16), g2_ref[...])
    o_ref[...] = ((yn + bonus) * g).reshape(NB, L, W)


def _rwkv_dir(p_rwkv, y_prev, params, d, ctx_len):
    B, S, PW = p_rwkv.shape
    L, W = RWKV_CHUNK, RWKV_W
    NC, NCc = S // L, ctx_len // L
    NB = 4
    final = y_prev is not None

    def chunk(ci):
        if d == 0:
            return ci
        return jnp.where(ci < NCc, NCc - 1 - ci, NC - 1 - (ci - NCc))

    hb = L // 8
    tok = lambda n: pl.BlockSpec((NB, L, n), lambda b, ci: (b, chunk(ci), 0))
    full = lambda a: pl.BlockSpec(a.shape, lambda b, ci: (0,) * a.ndim)
    in_specs = [tok(PW),
                pl.BlockSpec((NB, 8, PW), lambda b, ci: (b, jnp.maximum(chunk(ci) * hb - 1, 0), 0)),
                pl.BlockSpec((NB, 8, PW), lambda b, ci: (b, jnp.minimum((chunk(ci) + 1) * hb, S // 8 - 1), 0))]
    args = [p_rwkv, p_rwkv, p_rwkv]
    if final:
        in_specs.append(tok(W))
        args.append(y_prev)
    in_specs += [full(a) for a in params]
    args += list(params)
    return pl.pallas_call(
        functools.partial(_rwkv_kernel, d=d, final=final, NC=NC, NCc=NCc),
        grid=(B // NB, NC),
        in_specs=in_specs,
        out_specs=tok(W),
        out_shape=jax.ShapeDtypeStruct((B, S, W), F32),
        scratch_shapes=[pltpu.VMEM((NB, RWKV_HEADS // 2, RWKV_HEAD, 2 * RWKV_HEAD), F32),
                        pltpu.VMEM((NB, L + 16, PW), F32),
                        pltpu.VMEM((NB * L, W), F32)],
        compiler_params=_params("parallel", "arbitrary"),
        name="rwkv_rev" if d else "rwkv_fwd",
    )(*args)


def _rwkv_params(mu, w0, w2, a0, a2, g2, k_k, k_a, r_k, ln_w, ln_b):
    W = RWKV_W
    half = LORA_W // 2

    def pad_dir(w):
        out = jnp.zeros((2, LORA_W, W), F32)
        out = out.at[0, :half].set(w[0]).at[1, half:].set(w[1])
        return out.astype(BF16)

    pv = jnp.zeros((8, W), F32)
    pv = pv.at[0].set(k_k).at[1].set(k_a).at[2].set(r_k.reshape(W)).at[3].set(ln_w).at[4].set(ln_b)
    return (mu.reshape(1, RWKV_IN), w0, pad_dir(w2), a0, pad_dir(a2), g2.astype(BF16), pv)


def _mla_project(p, cq_t, sq_t, ck_t, sk_t, qn_g, kvn_g, wq, wk, gn):
    HP = HEAD_PAD
    lane = lax.broadcasted_iota(jnp.int32, (1, HP), 1)
    m_nope = jnp.where(lane < MLA_NOPE, 1.0, 0.0)
    m_rope = jnp.where(lane < MLA_NOPE, 0.0, jnp.where(lane < MLA_NOPE + MLA_ROPE, 1.0, 0.0))
    g_q, g_kn, g_kr = gn[0:1, :], gn[1:2, :], gn[2:3, :]

    def rms(x, n):
        return lax.rsqrt(jnp.sum(x * x, axis=-1, keepdims=True) * (1.0 / n) + NORM_EPS)

    xq = p[:, :Q_LORA]
    cq = (xq * rms(xq, Q_LORA) * qn_g).astype(BF16)
    xkv = p[:, Q_LORA:Q_LORA + KV_LORA]
    ckv = (xkv * rms(xkv, KV_LORA) * kvn_g).astype(BF16)
    q = _dot(cq, wq)
    kn = _dot(ckv, wk)

    kr = p[:, Q_LORA + KV_LORA:Q_LORA + KV_LORA + HP]
    krn = kr * rms(kr, MLA_ROPE) * g_kr
    to_rope = pltpu.roll(krn, MLA_NOPE, axis=1)
    sw_a = jnp.where(lane >= 112, 0.0, jnp.where(lane >= 96, pltpu.roll(krn, 80, axis=1), 0.0))
    sw_b = jnp.where(lane >= 112, pltpu.roll(krn, 112, axis=1), 0.0)
    kext = to_rope + sw_a + sw_b
    krot = kext * ck_t + pltpu.roll(kext, HP - MLA_ROPE, axis=1) * sk_t

    qs, ks = [], []
    for h in range(MLA_HEADS):
        sl = slice(h * HP, (h + 1) * HP)
        qh = q[:, sl]
        scale = m_nope * rms(qh * m_nope, MLA_NOPE) + (1.0 - m_nope) * rms(qh * m_rope, MLA_ROPE)
        qn = qh * scale * g_q
        qs.append((qn * cq_t + pltpu.roll(qn, HP - MLA_ROPE, axis=1) * sq_t).astype(BF16))
        kh = kn[:, sl]
        ks.append((kh * rms(kh, MLA_NOPE) * g_kn + krot).astype(BF16))
    return jnp.concatenate(qs, axis=1), jnp.concatenate(ks, axis=1), ckv


def _mla_params(q_norm, kv_norm, w_uq, w_ukv, qn_nope, kn_nope, qn_rope, kn_rope):
    H, NP, RP, HP = MLA_HEADS, MLA_NOPE, MLA_ROPE, HEAD_PAD
    half = RP // 2
    swap = jnp.concatenate([jnp.arange(half, RP), jnp.arange(0, half)])
    wq = w_uq.reshape(Q_LORA, H, NP + RP)
    wq = jnp.concatenate([wq, wq[:, :, NP + swap]], axis=-1).reshape(Q_LORA, H * HP)
    wkv = w_ukv.reshape(KV_LORA, H, NP + MLA_V)
    wk = jnp.concatenate([wkv[:, :, :NP], jnp.zeros((KV_LORA, H, HP - NP), F32)], axis=-1)
    wk = wk.reshape(KV_LORA, H * HP)
    wv = wkv[:, :, NP:].reshape(KV_LORA, H * MLA_V)
    gn = jnp.zeros((8, HP), F32)
    gn = gn.at[0].set(jnp.concatenate([qn_nope, qn_rope, qn_rope[swap]]))
    gn = gn.at[1, :NP].set(kn_nope).at[2, :RP].set(kn_rope)
    return (q_norm.reshape(1, Q_LORA), kv_norm.reshape(1, KV_LORA),
            wq.astype(BF16), wk.astype(BF16), wv.T.astype(BF16), gn)


def _rope_tables(ctx_len, seq):
    rows = seq // GRID_W
    axis_dims = MLA_ROPE // 2
    row = jnp.repeat(jnp.arange(rows), GRID_W).astype(F32)
    col = jnp.tile(jnp.arange(GRID_W), rows).astype(F32)
    inv = ROPE_BASE ** (-jnp.arange(0, axis_dims, 2, dtype=F32) / axis_dims)
    ang = jnp.concatenate([row[:, None] * inv, col[:, None] * inv], axis=-1)
    cos = jnp.concatenate([jnp.ones((ctx_len, axis_dims), F32), jnp.cos(ang)], axis=0)
    sin = jnp.concatenate([jnp.zeros((ctx_len, axis_dims), F32), jnp.sin(ang)], axis=0)
    S = ctx_len + seq
    pad = jnp.zeros((S, HEAD_PAD - MLA_NOPE - MLA_ROPE), F32)
    cos_t = jnp.concatenate([jnp.ones((S, MLA_NOPE), F32), cos, cos, pad], axis=-1)
    sin_t = jnp.concatenate([jnp.zeros((S, MLA_NOPE), F32), -sin, sin, pad], axis=-1)
    qs = MLA_SCALE * math.log2(math.e)
    return cos_t * qs, sin_t * qs, cos_t, sin_t


def _attn_kernel(q_ref, k_ref, vt_ref, o_ref, *, nct, ctx_len):
    S = k_ref.shape[1]
    i = pl.program_id(2)

    def attend(nk):
        hs = range(q_ref.shape[2] // HEAD_PAD)
        s = [_dot_nt(k_ref[0, :nk, h * HEAD_PAD:(h + 1) * HEAD_PAD],
                     q_ref[0, :, h * HEAD_PAD:(h + 1) * HEAD_PAD]) for h in hs]
        m = [jnp.max(t, axis=0, keepdims=True) for t in s]
        e = [jnp.exp2(s[h] - m[h]) for h in hs]
        l = [jnp.sum(t, axis=0, keepdims=True) for t in e]
        o = [_dot(vt_ref[0, h * MLA_V:(h + 1) * MLA_V, :nk], e[h].astype(BF16)) / l[h] for h in hs]
        for j in range(len(hs) // 2):
            pair = jnp.concatenate([o[2 * j], o[2 * j + 1]], axis=0)
            o_ref[0, :, j * 2 * MLA_V:(j + 1) * 2 * MLA_V] = pair.T.astype(BF16)

    @pl.when(i < nct)
    def _():
        attend(ctx_len)

    @pl.when(i >= nct)
    def _():
        attend(S)


def _attention(q, k, v, ctx_len):
    B, S, _ = q.shape
    tq = _token_tile(ctx_len, S - ctx_len)
    HS = 8
    return pl.pallas_call(
        functools.partial(_attn_kernel, nct=ctx_len // tq, ctx_len=ctx_len),
        grid=(B, MLA_HEADS // HS, S // tq),
        in_specs=[pl.BlockSpec((1, tq, HS * HEAD_PAD), lambda b, h, i: (b, i, h)),
                  pl.BlockSpec((1, S, HS * HEAD_PAD), lambda b, h, i: (b, 0, h)),
                  pl.BlockSpec((1, HS * MLA_V, S), lambda b, h, i: (b, h, 0))],
        out_specs=pl.BlockSpec((1, tq, HS * MLA_V), lambda b, h, i: (b, i, h)),
        out_shape=jax.ShapeDtypeStruct((B, S, MLA_HEADS * MLA_V), BF16),
        compiler_params=_params("parallel", "parallel", "parallel"),
        name="mla_attention",
    )(q, k, v)


def _merge_kernel(z_ref, g1_ref, g2_ref, mod_ref, ys_ref, yr_ref, ym_ref, wg_ref, wb_ref, wo_ref,
                  glw_ref, glb_ref, rw_ref, zo_ref, h2_ref, lg_ref):
    D, W = D_MODEL, BRANCH_WIDTH
    NB, tm = z_ref.shape[0], z_ref.shape[1]
    ns = range(NB)
    rows = lambda t: jnp.concatenate(t, axis=0)
    z = [z_ref[i] for i in ns]
    m = [mod_ref[i, 0] for i in ns]
    h = rows([_norm_mod(z[i], g1_ref[...], m[i][0:1], m[i][1:2]) for i in ns]).astype(BF16)
    ys = rows([ys_ref[:, i * W:(i + 1) * W] for i in ns])
    ys = 0.5 * ys * (1.0 + jnp.tanh(math.sqrt(2.0 / math.pi) * (ys + 0.044715 * ys * ys * ys)))
    ys = ys * _sigmoid(_dot(ys.astype(BF16), glw_ref[...]) + glb_ref[...])
    branches = (ys.astype(BF16), rows([yr_ref[i] for i in ns]).astype(BF16), rows([ym_ref[i] for i in ns]))
    acc = jnp.zeros((NB * tm, D), F32)
    for j, yj in enumerate(branches):
        gate = _sigmoid(_dot(h, wg_ref[:, j * D:(j + 1) * D]))
        acc = acc + gate * _dot(yj, wb_ref[j])
    mix = _dot(acc.astype(BF16), wo_ref[...])
    zn = [z[i] + m[i][2:3] * mix[i * tm:(i + 1) * tm] for i in ns]
    h2 = rows([_norm_mod(zn[i], g2_ref[...], m[i][3:4], m[i][4:5]) for i in ns])
    E = rw_ref.shape[0]
    rw_rows = jnp.concatenate(_split3(rw_ref[...]), axis=0)
    hp = _split3(h2)
    t0 = _dot_nt(rw_rows, hp[0])
    t1 = _dot_nt(rw_rows[:2 * E], hp[1])
    t2 = _dot_nt(rw_rows[:E], hp[2])
    lg = t0[:E] + t0[E:2 * E] + t0[2 * E:] + t1[:E] + t1[E:] + t2
    for i in ns:
        zo_ref[i] = zn[i]
        h2_ref[i] = hp[0][i * tm:(i + 1) * tm]
        lg_ref[i] = lg[:, i * tm:(i + 1) * tm]


def _merge(z, g1, g2, modsel, ys, yr, ym, wg, wb, wo, glw, glb, rwt, ctx_len):
    B, S, D = z.shape
    tm = _token_tile(ctx_len, S - ctx_len)
    nct = ctx_len // tm
    E = rwt.shape[0]
    NB = 2
    tok = lambda n: pl.BlockSpec((NB, tm, n), lambda b, i: (b, i, 0))
    full = lambda a: pl.BlockSpec(a.shape, lambda b, i: (0,) * a.ndim)
    W = BRANCH_WIDTH
    return pl.pallas_call(
        _merge_kernel,
        grid=(B // NB, S // tm),
        in_specs=[tok(D), full(g1), full(g2),
                  pl.BlockSpec((NB, 1, N_MOD, D), lambda b, i: (b, jnp.where(i >= nct, 1, 0), 0, 0)),
                  pl.BlockSpec((tm, NB * W), lambda b, i: (i, b)),
                  tok(W), tok(W), full(wg), full(wb), full(wo), full(glw), full(glb), full(rwt)],
        out_specs=[tok(D), tok(D), pl.BlockSpec((NB, E, tm), lambda b, i: (b, 0, i))],
        out_shape=[jax.ShapeDtypeStruct((B, S, D), F32), jax.ShapeDtypeStruct((B, S, D), BF16),
                   jax.ShapeDtypeStruct((B, E, S), F32)],
        compiler_params=_params("parallel", "parallel"),
        name="merge",
    )(z, g1, g2, modsel, ys, yr, ym, wg, wb, wo, glw, glb, rwt)


def _route_kernel(lg_ref, slot_ref, gate_ref, *, ctx_len, cap_c, cap_l):
    lg = lg_ref[0]
    E, S = lg.shape
    e = jnp.exp(lg - jnp.max(lg, axis=0, keepdims=True))
    aff = e / jnp.sum(e, axis=0, keepdims=True)
    bits = pltpu.bitcast(aff, jnp.int32)
    lane = lax.broadcasted_iota(jnp.int32, (1, S), 1)
    in_c = lane < ctx_len
    ms_c = jnp.where(in_c, 1.0, 0.0)
    LT = 128
    tr = lax.broadcasted_iota(jnp.int32, (LT, LT), 0)
    tc = lax.broadcasted_iota(jnp.int32, (LT, LT), 1)
    tri = jnp.where(tr <= tc, 1.0, 0.0).astype(BF16)

    def counts(x):
        n_c = jnp.sum(x * ms_c, axis=1, keepdims=True)
        return n_c, jnp.sum(x, axis=1, keepdims=True) - n_c

    def prefix(x):
        tiles = [x[:, t * LT:(t + 1) * LT] for t in range(S // LT)]
        inc = _dot(jnp.concatenate(tiles, axis=0).astype(BF16), tri)
        off = jnp.zeros((E, 1), F32)
        out = []
        for t, xt in enumerate(tiles):
            it = inc[t * E:(t + 1) * E]
            out.append(it - xt + off)
            off = off + it[:, LT - 1:LT]
        n_c, _ = counts(x)
        return jnp.concatenate(out, axis=1) - jnp.where(in_c, 0.0, n_c)

    def body(_, carry):
        lo_c, hi_c, lo_l, hi_l = carry
        mid_c = lo_c + ((hi_c - lo_c + 1) >> 1)
        mid_l = lo_l + ((hi_l - lo_l + 1) >> 1)
        n_c, n_l = counts(jnp.where(bits >= jnp.where(in_c, mid_c, mid_l), 1.0, 0.0))
        ok_c, ok_l = n_c >= cap_c, n_l >= cap_l
        return (jnp.where(ok_c, mid_c, lo_c), jnp.where(ok_c, hi_c, mid_c - 1),
                jnp.where(ok_l, mid_l, lo_l), jnp.where(ok_l, hi_l, mid_l - 1))

    zero = jnp.zeros((E, 1), jnp.int32)
    top = jnp.full((E, 1), 0x7F800000, jnp.int32)
    lo_c, _, lo_l, _ = lax.fori_loop(0, 31, body, (zero, top, zero, top))
    thr = jnp.where(in_c, lo_c, lo_l)
    gt = jnp.where(bits > thr, 1.0, 0.0)
    eq = jnp.where(bits == thr, 1.0, 0.0)
    g_c, g_l = counts(gt)
    need = jnp.where(in_c, cap_c - g_c, cap_l - g_l)
    sel = gt + eq * jnp.where(prefix(eq) < need, 1.0, 0.0)
    rank = prefix(sel) + jnp.where(in_c, 0.0, float(cap_c))
    slot_ref[0] = jnp.where(sel > 0.0, rank, -1.0)
    gate_ref[0] = aff * sel


def _route(logits_t, ctx_len, cap_c, cap_l):
    B, E, S = logits_t.shape
    assert S % 128 == 0
    spec = pl.BlockSpec((1, E, S), lambda b: (b, 0, 0))
    return pl.pallas_call(
        functools.partial(_route_kernel, ctx_len=ctx_len, cap_c=cap_c, cap_l=cap_l),
        grid=(B,),
        in_specs=[spec],
        out_specs=[spec, spec],
        out_shape=[jax.ShapeDtypeStruct((B, E, S), F32)] * 2,
        compiler_params=_params("parallel"),
        name="route",
    )(logits_t)


def _expert_kernel(h_ref, slot_ref, gate_ref, w1_ref, w3_ref, w2_ref, oc_ref, ol_ref, *, cap, cap_c, ctx_len):
    NB, S = h_ref.shape[0], h_ref.shape[1]
    onehots, gc = [], []
    for i in range(NB):
        for (c0, c1), (t0, t1) in (((0, cap_c), (0, ctx_len)), ((cap_c, cap), (ctx_len, S))):
            slot = slot_ref[i, 0, :, t0:t1]
            gate = gate_ref[i, 0, :, t0:t1]
            cidx = lax.broadcasted_iota(jnp.int32, (c1 - c0, t1 - t0), 0).astype(F32) + float(c0)
            hit = slot == cidx
            onehots.append((jnp.where(hit, 1.0, 0.0).astype(BF16), i, t0, t1))
            gc.append(jnp.sum(jnp.where(hit, gate, 0.0), axis=1, keepdims=True))
    xs = [_dot(oh, h_ref[i, t0:t1, :]) for oh, i, t0, t1 in onehots]
    xs = jnp.concatenate(xs, axis=0).astype(BF16)
    gc = jnp.concatenate(gc, axis=0)
    a1 = _dot(xs, w1_ref[0])
    a3 = _dot(xs, w3_ref[0])
    hid = (a1 * _sigmoid(a1) * a3).astype(BF16)
    y = (_dot(hid, w2_ref[0]) * gc).astype(BF16)
    for i in range(NB):
        oc_ref[i] = y[i * cap:i * cap + cap_c]
        ol_ref[i] = y[i * cap + cap_c:(i + 1) * cap]


def _experts(h2, slot, gate, w1, w3, w2, layer, cap, cap_c, ctx_len):
    B, S, D = h2.shape
    _, E, _, F = w1.shape
    NB = 2
    row = pl.BlockSpec((NB, 1, 1, S), lambda e, b: (b, e, 0, 0))
    return pl.pallas_call(
        functools.partial(_expert_kernel, cap=cap, cap_c=cap_c, ctx_len=ctx_len),
        grid=(E, B // NB),
        in_specs=[pl.BlockSpec((NB, S, D), lambda e, b: (b, 0, 0)), row, row,
                  pl.BlockSpec((None, 1, D, F), lambda e, b: (layer, e, 0, 0)),
                  pl.BlockSpec((None, 1, D, F), lambda e, b: (layer, e, 0, 0)),
                  pl.BlockSpec((None, 1, F, D), lambda e, b: (layer, e, 0, 0))],
        out_specs=[pl.BlockSpec((NB, cap_c, D), lambda e, b: (b, e, 0)),
                   pl.BlockSpec((NB, cap - cap_c, D), lambda e, b: (b, e, 0))],
        out_shape=[jax.ShapeDtypeStruct((B, E * cap_c, D), BF16),
                   jax.ShapeDtypeStruct((B, E * (cap - cap_c), D), BF16)],
        compiler_params=_params("parallel", "parallel"),
        name="experts",
    )(h2, slot.reshape(B, E, 1, S), gate.reshape(B, E, 1, S), w1, w3, w2)


def _scatter_kernel(win_ref, z_ref, mod_ref, sl_ref, yc_ref, yl_ref, o_ref, ycat_ref, *, cap_c, cap_l, nct, ntl, win,
                    tile0):
    sl = sl_ref[0]
    E, tm = sl.shape
    b, i = pl.program_id(0), pl.program_id(1) + tile0

    def onehot_t(lo, n, e):
        cidx = lax.broadcasted_iota(jnp.int32, (n, tm), 0).astype(F32) + lo
        return jnp.where(sl[e:e + 1, :] == cidx, 1.0, 0.0).astype(BF16)

    def finish(onehots, rows):
        acc = _dot_tn(jnp.concatenate(onehots, axis=0), rows)
        o_ref[0] = z_ref[0] + mod_ref[0, 0][5:6] * acc

    if tile0 < nct:
        @pl.when(i < nct)
        def _():
            finish([onehot_t(0.0, cap_c, e) for e in range(E)], yc_ref[0])

    base =(b * ntl + jnp.maximum(i - nct, 0)) * (E + 1)
    fits = win_ref[base + E] == 1

    @pl.when((i >= nct) & fits)
    def _():
        onehots = []
        for e in range(E):
            w = win_ref[base + e]
            r0 = pl.multiple_of(e * cap_l + w, 16)
            ycat_ref[e * win:(e + 1) * win, :] = yl_ref[0, pl.ds(r0, win), :]
            onehots.append(onehot_t((w + cap_c).astype(F32), win, e))
        finish(onehots, ycat_ref[...])

    @pl.when((i >= nct) & jnp.logical_not(fits))
    def _():
        finish([onehot_t(float(cap_c), cap_l, e) for e in range(E)], yl_ref[0])


def _scatter_windows(slot, ctx_len, tm, cap_l, win):
    B, E, S = slot.shape
    ntl = (S - ctx_len) // tm
    cnt = (slot[:, :, ctx_len:] >= 0).reshape(B, E, ntl, tm).sum(-1).astype(jnp.int32)
    lo = jnp.cumsum(cnt, axis=-1) - cnt
    w = jnp.minimum((lo // 16) * 16, cap_l - win)
    fits = jnp.all(lo - w + cnt <= win, axis=1)
    table = jnp.concatenate([jnp.swapaxes(w, 1, 2), fits[..., None].astype(jnp.int32)], axis=-1)
    return table.reshape(-1)


def _scatter(z, modsel, slot, yc, yl, ctx_len, cap_c, win=64, latent_only=False):
    B, S, D = z.shape
    E = slot.shape[1]
    cap_l = yl.shape[1] // E
    tm = _token_tile(ctx_len, S - ctx_len)
    nct = ctx_len // tm
    ntl = (S - ctx_len) // tm
    t0 = nct if latent_only else 0
    win = min(win, cap_l)
    tok = lambda n: pl.BlockSpec((1, tm, n), lambda b, i, w: (b, i + t0, 0))
    whole = lambda a: pl.BlockSpec((1,) + a.shape[1:], lambda b, i, w: (b, 0, 0))
    grid_spec = pltpu.PrefetchScalarGridSpec(
        num_scalar_prefetch=1,
        grid=(B, S // tm - t0),
        in_specs=[tok(D),
                  pl.BlockSpec((1, 1, N_MOD, D), lambda b, i, w: (b, jnp.where(i + t0 >= nct, 1, 0), 0, 0)),
                  pl.BlockSpec((1, E, tm), lambda b, i, w: (b, 0, i + t0)), whole(yc), whole(yl)],
        out_specs=pl.BlockSpec((1, tm, D), lambda b, i, w: (b, i, 0)),
        scratch_shapes=[pltpu.VMEM((E * win, D), BF16)])
    return pl.pallas_call(
        functools.partial(_scatter_kernel, cap_c=cap_c, cap_l=cap_l, nct=nct, ntl=ntl, win=win, tile0=t0),
        grid_spec=grid_spec,
        out_shape=jax.ShapeDtypeStruct((B, S - t0 * tm, D), F32),
        compiler_params=_params("parallel", "parallel"),
        name="moe_scatter",
    )(_scatter_windows(slot, ctx_len, tm, cap_l, win), z, modsel, slot, yc, yl)


def kernel(x, c, ctx, c_ctx, ada_w, ada_b, norm1_g, norm2_g, w_in, ssm_lambda_re, ssm_lambda_im, ssm_log_dt, ssm_b_re, ssm_b_im, ssm_c_re, ssm_c_im, ssm_d, ssm_glu_w, ssm_glu_b, rwkv_mu, rwkv_w0, rwkv_w2, rwkv_a0, rwkv_a2, rwkv_g2, rwkv_k_k, rwkv_k_a, rwkv_r_k, rwkv_ln_w, rwkv_ln_b, mla_q_norm, mla_kv_norm, mla_w_uq, mla_w_ukv, mla_qn_nope, mla_kn_nope, mla_qn_rope, mla_kn_rope, w_branch, w_out, router_w, moe_w1, moe_w3, moe_w2):
    B, T, D = x.shape
    CTX = ctx.shape[1]
    S = CTX + T
    depth = ada_w.shape[0]
    cap_c = EC_CAPACITY * CTX // N_EXPERTS
    cap_l = EC_CAPACITY * T // N_EXPERTS
    assert D == D_MODEL and T % GRID_W == 0 and cap_c % 8 == 0 and cap_l % 8 == 0 and B % 8 == 0

    rows = -(-(B + 1) // 8) * 8
    cc = jnp.concatenate([c, c_ctx[None, :], jnp.zeros((rows - B - 1, D), F32)], axis=0)
    mods = _mods(cc, ada_w, ada_b).reshape(depth, rows, N_MOD, D)
    mod_lat = mods[:, :B]
    mod_ctx = jnp.broadcast_to(mods[:, B:B + 1], mod_lat.shape)
    modsel = jnp.stack([mod_ctx, mod_lat], axis=2)

    tables = _rope_tables(CTX, T)

    z = jnp.concatenate([ctx, x], axis=1)
    w_in_b = w_in.astype(BF16)
    w1_b, w3_b, w2_b = moe_w1.astype(BF16), moe_w3.astype(BF16), moe_w2.astype(BF16)
    for l in range(depth):
        g1, g2 = norm1_g[l].reshape(1, D), norm2_g[l].reshape(1, D)
        wl = w_in_b[l]
        w_ssm = wl[:, :RWKV_OFF]
        w_rwkv = wl[:, RWKV_OFF:MLA_OFF]
        w_mla = jnp.pad(wl[:, MLA_OFF:GATE_OFF], ((0, 0), (0, MLA_IN_PAD - MLA_IN)))
        w_gate = wl[:, GATE_OFF:]
        mp = _mla_params(mla_q_norm[l], mla_kv_norm[l], mla_w_uq[l], mla_w_ukv[l], mla_qn_nope[l],
                         mla_kn_nope[l], mla_qn_rope[l], mla_kn_rope[l])
        p_ssm, p_rwkv, q, k, v = _input_proj(z, g1, modsel[l], w_ssm, w_rwkv, w_mla, tables, mp, CTX)

        prep = _ssm_prep(ssm_lambda_re[l], ssm_lambda_im[l], ssm_log_dt[l], ssm_b_re[l], ssm_b_im[l],
                         ssm_c_re[l], ssm_c_im[l])
        y_ssm = _ssm_scan(p_ssm, prep, ssm_d[l], B, CTX)

        rp = _rwkv_params(rwkv_mu[l], rwkv_w0[l], rwkv_w2[l], rwkv_a0[l], rwkv_a2[l], rwkv_g2[l],
                          rwkv_k_k[l], rwkv_k_a[l], rwkv_r_k[l], rwkv_ln_w[l], rwkv_ln_b[l])
        y_fwd = _rwkv_dir(p_rwkv, None, rp, 0, CTX)
        y_rwkv = _rwkv_dir(p_rwkv, y_fwd, rp, 1, CTX)

        y_mla = _attention(q, k, v, CTX)

        z, h2, logits_t = _merge(z, g1, g2, modsel[l], y_ssm, y_rwkv, y_mla, w_gate,
                                 w_branch[l].astype(BF16), w_out[l].astype(BF16),
                                 ssm_glu_w[l].astype(BF16), ssm_glu_b[l].reshape(1, -1),
                                 router_w[l].T, CTX)
        slot, gate = _route(logits_t, CTX, cap_c, cap_l)
        yc, yl = _experts(h2, slot, gate, w1_b, w3_b, w2_b, l, cap_c + cap_l, cap_c, CTX)
        z = _scatter(z, modsel[l], slot, yc, yl, CTX, cap_c, latent_only=(l == depth - 1))
    return z
```

```python
import functools
import math

import jax
import jax.numpy as jnp
from jax import lax
from jax.experimental import pallas as pl
from jax.experimental.pallas import tpu as pltpu

F32 = jnp.float32
BF16 = jnp.bfloat16

D_MODEL = 1024
GRID_W = 64
N_MOD = 6
NORM_EPS = 1e-6
GN_EPS = 64e-5
BRANCH_WIDTH = 512
SSM_GROUP = 16
SSM_GROUPS = BRANCH_WIDTH // SSM_GROUP
SSM_STATE = 64
SSM_CHUNK = 16
RWKV_W = BRANCH_WIDTH
RWKV_HEAD = 64
RWKV_HEADS = RWKV_W // RWKV_HEAD
RWKV_CHUNK = 64
LORA_W = 128
RWKV_IN = 3 * RWKV_W + 3 * LORA_W
MLA_HEADS = 8
MLA_NOPE = 64
MLA_ROPE = 32
MLA_V = 64
Q_LORA = 384
KV_LORA = 256
MLA_IN = Q_LORA + KV_LORA + MLA_ROPE
MLA_IN_PAD = 768
MLA_SCALE = 1.0 / math.sqrt(MLA_NOPE + MLA_ROPE)
ROPE_BASE = 10000.0
HEAD_PAD = 128
RWKV_OFF = BRANCH_WIDTH
MLA_OFF = RWKV_OFF + RWKV_IN
GATE_OFF = MLA_OFF + MLA_IN
N_EXPERTS = 16
EXPERT_FF = 1536
EC_CAPACITY = 2
VMEM_LIMIT = 56 * 1024 * 1024


def _params(*sem):
    return pltpu.CompilerParams(dimension_semantics=sem, vmem_limit_bytes=VMEM_LIMIT)


def _dot(a, b):
    return jnp.dot(a, b, preferred_element_type=F32)


def _dot_nt(a, b):
    return lax.dot_general(a, b, (((1,), (1,)), ((), ())), preferred_element_type=F32)


def _dot_tn(a, b):
    return lax.dot_general(a, b, (((0,), (0,)), ((), ())), preferred_element_type=F32)


def _split2(x):
    hi = x.astype(BF16)
    lo = (x - hi.astype(F32)).astype(BF16)
    return hi, lo


def _split3(x):
    hi = x.astype(BF16)
    r1 = x - hi.astype(F32)
    mid = r1.astype(BF16)
    lo = (r1 - mid.astype(F32)).astype(BF16)
    return hi, mid, lo


def _dot_exact_rhs(xs, m):
    rows, width = xs[0].shape
    k = m.shape[0]
    nt = width // k
    parts = [p[:, j * k:(j + 1) * k] for x in xs for p in _split2(x) for j in range(nt)]
    res = _dot(jnp.concatenate(parts, axis=0), m)
    blk = lambda i: res[i * rows:(i + 1) * rows]
    return [jnp.concatenate([blk((2 * i) * nt + j) + blk((2 * i + 1) * nt + j) for j in range(nt)], axis=1)
            for i in range(len(xs))]


def _dot_exact_lhs(m, x, parts):
    pieces = _split2(x) if parts == 2 else _split3(x)
    acc = _dot(m, pieces[0])
    for p in pieces[1:]:
        acc = acc + _dot(m, p)
    return acc


def _sigmoid(x):
    return 0.5 * jnp.tanh(0.5 * x) + 0.5


def _norm_mod(x, g, shift, scale):
    y = x * lax.rsqrt(jnp.mean(x * x, axis=-1, keepdims=True) + NORM_EPS)
    return (y * g) * (1.0 + scale) + shift


def _token_tile(ctx_len, seq):
    for tm in (256, 128, 64):
        if ctx_len % tm == 0 and seq % tm == 0:
            return tm
    raise ValueError("context and latent lengths must be multiples of 64")


def _mods_kernel(cc_ref, w_ref, b_ref, o_ref):
    c = cc_ref[...]
    s = (c * _sigmoid(c)).astype(BF16)
    o_ref[0] = _dot(s, w_ref[0].astype(BF16)) + b_ref[0]


def _mods(cc, ada_w, ada_b):
    L, D, N = ada_w.shape
    R = cc.shape[0]
    tn = 1536
    return pl.pallas_call(
        _mods_kernel,
        grid=(L, N // tn),
        in_specs=[pl.BlockSpec((R, D), lambda l, j: (0, 0)),
                  pl.BlockSpec((1, D, tn), lambda l, j: (l, 0, j)),
                  pl.BlockSpec((1, 1, tn), lambda l, j: (l, 0, j))],
        out_specs=pl.BlockSpec((1, R, tn), lambda l, j: (l, 0, j)),
        out_shape=jax.ShapeDtypeStruct((L, R, N), F32),
        compiler_params=_params("parallel", "parallel"),
        name="adaln_mods",
    )(cc, ada_w, ada_b.reshape(L, 1, N))


def _kin_kernel(z_ref, g_ref, mod_ref, ws_ref, wr_ref, wm_ref, cq_ref, sq_ref, ck_ref, sk_ref,
                qn_ref, kvn_ref, wq_ref, wk_ref, wv_ref, gn_ref, os_ref, or_ref, q_ref, k_ref, v_ref):
    NB, tm = z_ref.shape[0], z_ref.shape[1]
    W = ws_ref.shape[1]
    h = jnp.concatenate([_norm_mod(z_ref[i], g_ref[...], mod_ref[i, 0][0:1], mod_ref[i, 0][1:2])
                         for i in range(NB)], axis=0).astype(BF16)
    pm = _dot(h, wm_ref[...])
    rep = lambda t_ref: jnp.concatenate([t_ref[...]] * NB, axis=0)
    q, k, ckv = _mla_project(pm, rep(cq_ref), rep(sq_ref), rep(ck_ref), rep(sk_ref), qn_ref[...], kvn_ref[...],
                             wq_ref[...], wk_ref[...], gn_ref[...])
    ps, pr = _dot(h, ws_ref[...]), _dot(h, wr_ref[...])
    for i in range(NB):
        rows = slice(i * tm, (i + 1) * tm)
        os_ref[:, i * W:(i + 1) * W] = ps[rows]
        or_ref[i] = pr[rows]
        q_ref[i] = q[rows]
        k_ref[i] = k[rows]
        v_ref[i] = _dot_nt(wv_ref[...], ckv[rows]).astype(BF16)


def _input_proj(z, g, modsel, ws, wr, wm, tables, mla_params, ctx_len):
    B, S, D = z.shape
    tm = _token_tile(ctx_len, S - ctx_len)
    nct = ctx_len // tm
    NB = 2
    HW, VW = MLA_HEADS * HEAD_PAD, MLA_HEADS * MLA_V
    tok = lambda n: pl.BlockSpec((NB, tm, n), lambda b, i: (b, i, 0))
    tab = pl.BlockSpec((tm, HEAD_PAD), lambda b, i: (i, 0))
    full = lambda a: pl.BlockSpec(a.shape, lambda b, i: (0,) * a.ndim)
    return pl.pallas_call(
        _kin_kernel,
        grid=(B // NB, S // tm),
        in_specs=[tok(D), full(g),
                  pl.BlockSpec((NB, 1, N_MOD, D), lambda b, i: (b, jnp.where(i >= nct, 1, 0), 0, 0)),
                  full(ws), full(wr), full(wm), tab, tab, tab, tab] + [full(a) for a in mla_params],
        out_specs=[pl.BlockSpec((tm, NB * ws.shape[1]), lambda b, i: (i, b)), tok(wr.shape[1]),
                   tok(HW), tok(HW), pl.BlockSpec((NB, VW, tm), lambda b, i: (b, 0, i))],
        out_shape=[jax.ShapeDtypeStruct((S, B * ws.shape[1]), F32),
                   jax.ShapeDtypeStruct((B, S, wr.shape[1]), F32),
                   jax.ShapeDtypeStruct((B, S, HW), BF16), jax.ShapeDtypeStruct((B, S, HW), BF16),
                   jax.ShapeDtypeStruct((B, VW, S), BF16)],
        compiler_params=_params("parallel", "parallel"),
        name="input_proj",
    )(z, g, modsel, ws, wr, wm, *tables, *mla_params)


def _cpow(ar, ai, lag, shape, nbits=5):
    pr = jnp.ones(shape, F32)
    pi = jnp.zeros(shape, F32)
    for b in range(nbits):
        bit = ((lag >> b) & 1) == 1
        fr = jnp.where(bit, ar, 1.0)
        fi = jnp.where(bit, ai, 0.0)
        pr, pi = pr * fr - pi * fi, pr * fi + pi * fr
        ar, ai = ar * ar - ai * ai, 2.0 * ar * ai
    return pr, pi


def _ssm_prep_kernel(lc_re_ref, lc_im_ref, lr_re_ref, lr_im_ref, ldt_ref, bt_re_ref, bt_im_ref,
                     ct_re_ref, ct_im_ref, wt_ref, wso_ref, wsi_ref, a_ref):
    C, GC, P = SSM_CHUNK, SSM_GROUP, SSM_STATE
    W = C * GC
    lane = lax.broadcasted_iota(jnp.int32, (1, W), 1)
    quarter = lane // P
    is_re = (quarter == 0) | (quarter == 3)
    jcol = lane // GC
    srow = lax.broadcasted_iota(jnp.int32, (W, 1), 0) // GC
    for d in (0, 1):
        dt = jnp.exp(ldt_ref[d, 0])
        lr, li = lc_re_ref[d, 0], lc_im_ref[d, 0]
        mag = jnp.exp(lr * dt)
        ar, ai = mag * jnp.cos(li * dt), mag * jnp.sin(li * dt)
        cr, ci = ct_re_ref[d, 0], ct_im_ref[d, 0]
        lag_z = jcol if d == 0 else (C - 1) - jcol
        lag_s = jcol + 1 if d == 0 else C - jcol

        def q_of(lag):
            pr, pi = _cpow(ar, ai, lag, (P, W))
            q_re = cr * pr - ci * pi
            q_im = -(cr * pi + ci * pr)
            return q_re, q_im

        qz_re, qz_im = q_of(lag_z)
        qs_re, qs_im = q_of(lag_s)
        lr4, li4 = lr_re_ref[d, 0], lr_im_ref[d, 0]
        mag4 = jnp.exp(lr4 * dt)
        ar4, ai4 = mag4 * jnp.cos(li4 * dt), mag4 * jnp.sin(li4 * dt)
        den = lr4 * lr4 + li4 * li4
        nr, ni = ar4 - 1.0, ai4
        coef_re = (nr * lr4 + ni * li4) / den
        coef_im = (ni * lr4 - nr * li4) / den
        br, bi = bt_re_ref[0], bt_im_ref[0]
        bb_re = coef_re * br - coef_im * bi
        bb_im = coef_re * bi + coef_im * br
        bcat = jnp.where(lane < P, bb_re, bb_im)[:, :2 * P]
        qz = jnp.concatenate([qz_re, qz_im], axis=0)
        z = jnp.dot(bcat, qz, preferred_element_type=F32,
                    precision=lax.Precision.HIGHEST)
        rows = []
        for s in range(C):
            if d == 0:
                sh = pltpu.roll(z, GC * s, axis=1) if s else z
                rows.append(jnp.where(lane >= GC * s, sh, 0.0))
            else:
                m = C - 1 - s
                sh = pltpu.roll(z, W - GC * m, axis=1) if m else z
                rows.append(jnp.where(lane < W - GC * m, sh, 0.0))
        wt_ref[d, 0] = jnp.concatenate(rows, axis=0).astype(BF16)
        wsi_ref[d, 0] = jnp.concatenate([qs_re, qs_im], axis=0).astype(BF16)
        e_row = (C - 1) - srow if d == 0 else srow
        er, ei = _cpow(ar4, ai4, e_row, (W, W), nbits=4)
        bbx = jnp.where(is_re, bb_re, bb_im)
        bby = jnp.where(is_re, -bb_im, bb_re)
        bbx = jnp.concatenate([bbx] * C, axis=0)
        bby = jnp.concatenate([bby] * C, axis=0)
        wso_ref[d, 0] = (er * bbx + ei * bby).astype(BF16)
        cr16, ci16 = _cpow(ar4, ai4, jnp.full((1, W), C, jnp.int32), (1, W))
        a_ref[d, 0, 0:1, :] = cr16
        a_ref[d, 0, 1:2, :] = jnp.where(is_re, -ci16, ci16)


def _ssm_prep(lam_re, lam_im, log_dt, b_re, b_im, c_re, c_im):
    G, P, GC, C = SSM_GROUPS, SSM_STATE, SSM_GROUP, SSM_CHUNK
    W = C * GC
    lc_re, lc_im = lam_re[..., None], lam_im[..., None]
    lr_re = jnp.tile(lam_re, (1, 1, 4))[:, :, None, :]
    lr_im = jnp.tile(lam_im, (1, 1, 4))[:, :, None, :]
    ldt = log_dt[..., None, None]
    bt_re = jnp.tile(jnp.swapaxes(b_re, 1, 2), (1, 1, 4))
    bt_im = jnp.tile(jnp.swapaxes(b_im, 1, 2), (1, 1, 4))
    ct_re = jnp.tile(jnp.swapaxes(c_re, 2, 3), (1, 1, 1, C))
    ct_im = jnp.tile(jnp.swapaxes(c_im, 2, 3), (1, 1, 1, C))
    d4 = lambda a, b: pl.BlockSpec((2, 1, a, b), lambda g: (0, g, 0, 0))
    mat = jax.ShapeDtypeStruct((2, G, W, W), BF16)
    return pl.pallas_call(
        _ssm_prep_kernel,
        grid=(G,),
        in_specs=[d4(P, 1), d4(P, 1), d4(1, W), d4(1, W), d4(1, 1),
                  pl.BlockSpec((1, GC, W), lambda g: (g, 0, 0)),
                  pl.BlockSpec((1, GC, W), lambda g: (g, 0, 0)),
                  d4(P, W), d4(P, W)],
        out_specs=[d4(W, W), d4(W, W), d4(2 * P, W), d4(2, W)],
        out_shape=[mat, mat, jax.ShapeDtypeStruct((2, G, 2 * P, W), BF16),
                   jax.ShapeDtypeStruct((2, G, 2, W), F32)],
        compiler_params=_params("parallel"),
        name="ssm_prep",
    )(lc_re, lc_im, lr_re, lr_im, ldt, bt_re, bt_im, ct_re, ct_im)


def _ssm_kernel(*refs, d, final, KT):
    if final:
        x_ref, ub_ref, yp_ref, wt_ref, wso_ref, wsi_ref, a_ref, dn_ref, o_ref, st_ref = refs
    else:
        x_ref, wt_ref, wso_ref, wsi_ref, a_ref, ub_ref, o_ref, st_ref = refs
    C, GC, P = SSM_CHUNK, SSM_GROUP, SSM_STATE
    BT = x_ref.shape[2]
    GB = x_ref.shape[3] // GC
    R = KT * BT

    @pl.when(pl.program_id(2) == 0)
    def _():
        st_ref[...] = jnp.zeros_like(st_ref)

    gs = range(GB)
    if final:
        ub = [ub_ref[g, 0, 0] for g in gs]
    else:
        xt = [x_ref[:, t].reshape(R, GB * GC) for t in range(C)]
        ub = [jnp.concatenate([xt[t][:, g * GC:(g + 1) * GC] for t in range(C)], axis=1).astype(BF16) for g in gs]
        for g in gs:
            ub_ref[g, 0, 0] = ub[g]
    loc = [_dot(ub[g], wso_ref[0, g]) for g in gs]
    a1 = [a_ref[0, g, 0:1, :] for g in gs]
    a2 = [a_ref[0, g, 1:2, :] for g in gs]
    x = [st_ref[g] for g in gs]
    xs = [[None] * KT for _ in gs]
    for k in (range(KT) if d == 0 else range(KT - 1, -1, -1)):
        for g in gs:
            xs[g][k] = x[g][:, :2 * P]
            x[g] = a1[g] * x[g] + a2[g] * pltpu.roll(x[g], 2 * P, axis=1) + loc[g][k * BT:(k + 1) * BT]
    ys = []
    for g in gs:
        st_ref[g] = x[g]
        xin = jnp.concatenate(xs[g], axis=0).astype(BF16)
        ys.append(_dot(ub[g], wt_ref[0, g]) + _dot(xin, wsi_ref[0, g]))
    if not final:
        for g in gs:
            o_ref[g, 0, 0] = ys[g]
        return
    ys = [ys[g] + yp_ref[g, 0, 0] for g in gs]
    for t in range(C):
        yt = jnp.concatenate([ys[g][:, t * GC:(t + 1) * GC] for g in gs], axis=1)
        o_ref[:, t] = (yt + x_ref[:, t].reshape(R, GB * GC) * dn_ref[...]).reshape(KT, BT, GB * GC)


def _ssm_dir(p_tm, prev, prep, d_skip, d, B, ctx_len):
    S = p_tm.shape[0]
    G, GC, C, P = SSM_GROUPS, SSM_GROUP, SSM_CHUNK, SSM_STATE
    W = C * GC
    LW = 128
    GB = LW // GC
    BT = 16 if B % 16 == 0 else 8
    KT = _token_tile(ctx_len, S - ctx_len) // C
    NT, NTc = S // (C * KT), ctx_len // (C * KT)
    wt, wso, wsi, a4 = prep
    final = prev is not None
    R = KT * BT

    def tile(ti):
        if d == 0:
            return ti
        return jnp.where(ti < NTc, NTc - 1 - ti, NT - 1 - (ti - NTc))

    x4 = p_tm.reshape(S // C, C, B, G * GC)
    xspec = pl.BlockSpec((KT, C, BT, LW), lambda gb, bt, ti: (tile(ti), 0, bt, gb))
    gspec = pl.BlockSpec((GB, 1, 1, R, W), lambda gb, bt, ti: (gb, bt, tile(ti), 0, 0))
    gshape = (G, B // BT, NT, R, W)
    wspec = lambda r: pl.BlockSpec((1, GB, r, W), lambda gb, bt, ti: (d, gb, 0, 0))
    in_specs, args = [xspec], [x4]
    if final:
        in_specs += [gspec, gspec]
        args += list(prev)
    in_specs += [wspec(W), wspec(W), wspec(2 * P), wspec(2)]
    args += [wt, wso, wsi, a4]
    if final:
        in_specs.append(pl.BlockSpec((1, LW), lambda gb, bt, ti: (0, gb)))
        args.append(d_skip.reshape(1, G * GC))
        out_specs, out_shape = xspec, jax.ShapeDtypeStruct(x4.shape, F32)
    else:
        out_specs = [gspec, gspec]
        out_shape = [jax.ShapeDtypeStruct(gshape, BF16), jax.ShapeDtypeStruct(gshape, F32)]
    y = pl.pallas_call(
        functools.partial(_ssm_kernel, d=d, final=final, KT=KT),
        grid=(G // GB, B // BT, NT),
        in_specs=in_specs,
        out_specs=out_specs,
        out_shape=out_shape,
        scratch_shapes=[pltpu.VMEM((GB, BT, W), F32)],
        compiler_params=_params("parallel", "parallel", "arbitrary"),
        name="ssm_rev" if d else "ssm_fwd",
    )(*args)
    return y.reshape(p_tm.shape) if final else y


def _ssm_scan(p_tm, prep, d_skip, B, ctx_len):
    fwd = _ssm_dir(p_tm, None, prep, d_skip, 0, B, ctx_len)
    return _ssm_dir(p_tm, fwd, prep, d_skip, 1, B, ctx_len)


def _rwkv_kernel(*refs, d, final, NC, NCc):
    if final:
        (p_ref, hp_ref, hn_ref, yp_ref, mu_ref, w0_ref, w2_ref, a0_ref, a2_ref, g2_ref, pv_ref,
         o_ref, s_ref, xb_ref, yb_ref) = refs
    else:
        (p_ref, hp_ref, hn_ref, mu_ref, w0_ref, w2_ref, a0_ref, a2_ref, g2_ref, pv_ref,
         o_ref, s_ref, xb_ref, yb_ref) = refs
    L, N, H, W = RWKV_CHUNK, RWKV_HEAD, RWKV_HEADS, RWKV_W
    NB = p_ref.shape[0]
    ci = pl.program_id(1)
    if d == 0:
        c = ci
    else:
        c = jnp.where(ci < NCc, NCc - 1 - ci, NC - 1 - (ci - NCc))

    @pl.when(ci == 0)
    def _():
        s_ref[...] = jnp.zeros_like(s_ref)

    keep_prev = jnp.where((c == 0) | (c == NCc), 0.0, 1.0)
    keep_next = jnp.where((c == NCc - 1) | (c == NC - 1), 0.0, 1.0)
    xs = []
    for bi in range(NB):
        p = p_ref[bi]
        xb_ref[bi, 0:8, :] = hp_ref[bi] * keep_prev
        xb_ref[bi, 8:8 + L, :] = p
        xb_ref[bi, 8 + L:16 + L, :] = hn_ref[bi] * keep_next
        prev = xb_ref[bi, 7:7 + L, :]
        nxt = xb_ref[bi, 9:9 + L, :]
        xs.append(p + mu_ref[...] * (0.5 * (prev + nxt) - p))
    x = jnp.concatenate(xs, axis=0)

    r, k, v = x[:, 0:W], x[:, W:2 * W], x[:, 2 * W:3 * W]
    pw = x[:, 3 * W:3 * W + LORA_W]
    pa = x[:, 3 * W + LORA_W:3 * W + 2 * LORA_W]
    pg = x[:, 3 * W + 2 * LORA_W:3 * W + 3 * LORA_W]
    k_k, k_a, r_k = pv_ref[0:1, :], pv_ref[1:2, :], pv_ref[2:3, :]
    ln_w, ln_b = pv_ref[3:4, :], pv_ref[4:5, :]
    pab = pa.astype(BF16)

    zw = w0_ref[d:d + 1, :] + _dot(jnp.tanh(pw).astype(BF16), w2_ref[d])
    nz = -zw
    softplus = jnp.maximum(nz, 0.0) + jnp.log(1.0 + jnp.exp(-jnp.abs(nz)))
    lw = -jnp.exp(-softplus - 0.5)
    a = _sigmoid(a0_ref[d:d + 1, :] + _dot(pab, a2_ref[d]))
    kd = k * (1.0 + (a - 1.0) * k_a)

    hrow = lax.broadcasted_iota(jnp.int32, (2 * N, 2 * N), 0) // N
    hcol = lax.broadcasted_iota(jnp.int32, (2 * N, 2 * N), 1) // N
    head_ones = jnp.where(hrow == hcol, 1.0, 0.0).astype(BF16)

    kk = k * k_k
    kk = kk * lax.rsqrt(_dot_exact_rhs([kk * kk], head_ones)[0] + 1e-12)
    b = a * kk

    PW2 = 2 * N
    trow = lax.broadcasted_iota(jnp.int32, (L, PW2), 0)
    lane2 = lax.broadcasted_iota(jnp.int32, (L, PW2), 1)
    left = lane2 < N
    tcol = jnp.where(left, lane2, lane2 - N)
    if d == 0:
        strict, incl = tcol < trow, tcol <= trow
    else:
        strict, incl = tcol > trow, tcol >= trow

    def bdiag(t):
        return jnp.concatenate([jnp.where(left, t, 0.0), jnp.where(left, 0.0, t)], axis=0).astype(BF16)

    brow = lax.broadcasted_iota(jnp.int32, (NB * L, NB * L), 0)
    bcol = lax.broadcasted_iota(jnp.int32, (NB * L, NB * L), 1)
    upto = (bcol <= brow) if d == 0 else (bcol >= brow)
    cum = jnp.where(upto, jnp.where((brow // L) == (bcol // L), 1.0, 0.0), 0.0).astype(BF16)
    cs = _dot_exact_lhs(cum, lw, 3)
    last = L - 1 if d == 0 else 0
    cls = [cs[bi * L + last:bi * L + last + 1, :] for bi in range(NB)]
    cl = jnp.concatenate([jnp.broadcast_to(t, (L, W)) for t in cls], axis=0)
    e_to_end = jnp.exp(cl - cs)
    e_neg = jnp.exp(-cs)
    rt = (r * jnp.exp(cs)).astype(BF16)
    at = (kk * jnp.exp(cs - lw)).astype(BF16)
    bt = b * e_neg
    kt = kd * e_neg
    kh = (kd * e_to_end).astype(BF16)
    bh = (b * e_to_end).astype(BF16)
    e_chunk = [jnp.exp(t) for t in cls]
    vb = v.astype(BF16)

    ids = [(bi, j) for bi in range(NB) for j in range(H // 2)]
    n = range(len(ids))
    rs = [slice(bi * L, (bi + 1) * L) for bi, _ in ids]
    ls = [slice(j * PW2, (j + 1) * PW2) for _, j in ids]
    ar = [jnp.concatenate([at[rs[i], ls[i]], rt[rs[i], ls[i]]], axis=0) for i in n]
    s0 = [s_ref[bi, j] for bi, j in ids]
    g_b = [_dot_nt(ar[i], bdiag(bt[rs[i], ls[i]])) for i in n]
    g_k = [_dot_nt(ar[i], bdiag(kt[rs[i], ls[i]])) for i in n]
    g_s = [_dot_nt(ar[i], bdiag(s0[i])) for i in n]
    nab = [jnp.where(strict, g_b[i][:L], 0.0) for i in n]
    mrb = [jnp.where(incl, g_b[i][L:], 0.0).astype(BF16) for i in n]
    nm = [jnp.concatenate([jnp.where(strict, g_k[i][:L], 0.0), jnp.where(incl, g_k[i][L:], 0.0)],
                          axis=0).astype(BF16) for i in n]
    nv = [g_s[i] + _dot(nm[i], bdiag(v[rs[i], ls[i]])) for i in n]
    pm = [-t for t in nab]
    q = [_dot(t.astype(BF16), bdiag(t)) for t in nab]
    steps = int(math.log2(L)) - 1
    for it in range(steps):
        qd = [bdiag(t) for t in q]
        if it + 1 < steps:
            pq = [_dot(jnp.concatenate([pm[i], q[i]], axis=0).astype(BF16), qd[i]) for i in n]
            pm = [pm[i] + q[i] + pq[i][:L] for i in n]
            q = [t[L:] for t in pq]
        else:
            pm = [pm[i] + q[i] + _dot(pm[i].astype(BF16), qd[i]) for i in n]
    u = [nv[i][:L] + _dot(pm[i].astype(BF16), bdiag(nv[i][:L])) for i in n]
    for i in n:
        yb_ref[rs[i], ls[i]] = nv[i][L:] - _dot(mrb[i], bdiag(u[i]))
    for i, (bi, j) in enumerate(ids):
        vu = jnp.concatenate([vb[rs[i], ls[i]], -u[i].astype(BF16)], axis=0)
        kb = jnp.concatenate([kh[rs[i], ls[i]], bh[rs[i], ls[i]]], axis=0)
        full = _dot_tn(vu, kb)
        s_ref[bi, j] = s0[i] * e_chunk[bi][:, ls[i]] + jnp.where(left, full[:L], full[L:])

    if not final:
        o_ref[...] = yb_ref[...].reshape(NB, L, W)
        return
    y = yp_ref[...].reshape(NB * L, W) + yb_ref[...]
    o = 1 - d
    a_o = _sigmoid(a0_ref[o:o + 1, :] + _dot(pab, a2_ref[o]))
    kd_sum = kd + k * (1.0 + (a_o - 1.0) * k_a)
    y_sum, rk_sum = _dot_exact_rhs([y, r * kd_sum * r_k], head_ones)
    dev = y - y_sum * (1.0 / N)
    var = _dot_exact_rhs([dev * dev], head_ones)[0] * (1.0 / N)
    yn = dev * lax.rsqrt(var + GN_EPS) * ln_w + ln_b
    bonus = rk_sum * v
    g = _dot(_sigmoid(pg).astype(BF16), g2_ref[...])
    o_ref[...] = ((yn + bonus) * g).reshape(NB, L, W)


def _rwkv_dir(p_rwkv, y_prev, params, d, ctx_len):
    B, S, PW = p_rwkv.shape
    L, W = RWKV_CHUNK, RWKV_W
    NC, NCc = S // L, ctx_len // L
    NB = 4
    final = y_prev is not None

    def chunk(ci):
        if d == 0:
            return ci
        return jnp.where(ci < NCc, NCc - 1 - ci, NC - 1 - (ci - NCc))

    hb = L // 8
    tok = lambda n: pl.BlockSpec((NB, L, n), lambda b, ci: (b, chunk(ci), 0))
    full = lambda a: pl.BlockSpec(a.shape, lambda b, ci: (0,) * a.ndim)
    in_specs = [tok(PW),
                pl.BlockSpec((NB, 8, PW), lambda b, ci: (b, jnp.maximum(chunk(ci) * hb - 1, 0), 0)),
                pl.BlockSpec((NB, 8, PW), lambda b, ci: (b, jnp.minimum((chunk(ci) + 1) * hb, S // 8 - 1), 0))]
    args = [p_rwkv, p_rwkv, p_rwkv]
    if final:
        in_specs.append(tok(W))
        args.append(y_prev)
    in_specs += [full(a) for a in params]
    args += list(params)
    return pl.pallas_call(
        functools.partial(_rwkv_kernel, d=d, final=final, NC=NC, NCc=NCc),
        grid=(B // NB, NC),
        in_specs=in_specs,
        out_specs=tok(W),
        out_shape=jax.ShapeDtypeStruct((B, S, W), F32),
        scratch_shapes=[pltpu.VMEM((NB, RWKV_HEADS // 2, RWKV_HEAD, 2 * RWKV_HEAD), F32),
                        pltpu.VMEM((NB, L + 16, PW), F32),
                        pltpu.VMEM((NB * L, W), F32)],
        compiler_params=_params("parallel", "arbitrary"),
        name="rwkv_rev" if d else "rwkv_fwd",
    )(*args)


def _rwkv_params(mu, w0, w2, a0, a2, g2, k_k, k_a, r_k, ln_w, ln_b):
    W = RWKV_W
    half = LORA_W // 2

    def pad_dir(w):
        out = jnp.zeros((2, LORA_W, W), F32)
        out = out.at[0, :half].set(w[0]).at[1, half:].set(w[1])
        return out.astype(BF16)

    pv = jnp.zeros((8, W), F32)
    pv = pv.at[0].set(k_k).at[1].set(k_a).at[2].set(r_k.reshape(W)).at[3].set(ln_w).at[4].set(ln_b)
    return (mu.reshape(1, RWKV_IN), w0, pad_dir(w2), a0, pad_dir(a2), g2.astype(BF16), pv)


def _mla_project(p, cq_t, sq_t, ck_t, sk_t, qn_g, kvn_g, wq, wk, gn):
    HP = HEAD_PAD
    lane = lax.broadcasted_iota(jnp.int32, (1, HP), 1)
    m_nope = jnp.where(lane < MLA_NOPE, 1.0, 0.0)
    m_rope = jnp.where(lane < MLA_NOPE, 0.0, jnp.where(lane < MLA_NOPE + MLA_ROPE, 1.0, 0.0))
    g_q, g_kn, g_kr = gn[0:1, :], gn[1:2, :], gn[2:3, :]

    def rms(x, n):
        return lax.rsqrt(jnp.sum(x * x, axis=-1, keepdims=True) * (1.0 / n) + NORM_EPS)

    xq = p[:, :Q_LORA]
    cq = (xq * rms(xq, Q_LORA) * qn_g).astype(BF16)
    xkv = p[:, Q_LORA:Q_LORA + KV_LORA]
    ckv = (xkv * rms(xkv, KV_LORA) * kvn_g).astype(BF16)
    q = _dot(cq, wq)
    kn = _dot(ckv, wk)

    kr = p[:, Q_LORA + KV_LORA:Q_LORA + KV_LORA + HP]
    krn = kr * rms(kr, MLA_ROPE) * g_kr
    to_rope = pltpu.roll(krn, MLA_NOPE, axis=1)
    sw_a = jnp.where(lane >= 112, 0.0, jnp.where(lane >= 96, pltpu.roll(krn, 80, axis=1), 0.0))
    sw_b = jnp.where(lane >= 112, pltpu.roll(krn, 112, axis=1), 0.0)
    kext = to_rope + sw_a + sw_b
    krot = kext * ck_t + pltpu.roll(kext, HP - MLA_ROPE, axis=1) * sk_t

    qs, ks = [], []
    for h in range(MLA_HEADS):
        sl = slice(h * HP, (h + 1) * HP)
        qh = q[:, sl]
        scale = m_nope * rms(qh * m_nope, MLA_NOPE) + (1.0 - m_nope) * rms(qh * m_rope, MLA_ROPE)
        qn = qh * scale * g_q
        qs.append((qn * cq_t + pltpu.roll(qn, HP - MLA_ROPE, axis=1) * sq_t).astype(BF16))
        kh = kn[:, sl]
        ks.append((kh * rms(kh, MLA_NOPE) * g_kn + krot).astype(BF16))
    return jnp.concatenate(qs, axis=1), jnp.concatenate(ks, axis=1), ckv


def _mla_params(q_norm, kv_norm, w_uq, w_ukv, qn_nope, kn_nope, qn_rope, kn_rope):
    H, NP, RP, HP = MLA_HEADS, MLA_NOPE, MLA_ROPE, HEAD_PAD
    half = RP // 2
    swap = jnp.concatenate([jnp.arange(half, RP), jnp.arange(0, half)])
    wq = w_uq.reshape(Q_LORA, H, NP + RP)
    wq = jnp.concatenate([wq, wq[:, :, NP + swap]], axis=-1).reshape(Q_LORA, H * HP)
    wkv = w_ukv.reshape(KV_LORA, H, NP + MLA_V)
    wk = jnp.concatenate([wkv[:, :, :NP], jnp.zeros((KV_LORA, H, HP - NP), F32)], axis=-1)
    wk = wk.reshape(KV_LORA, H * HP)
    wv = wkv[:, :, NP:].reshape(KV_LORA, H * MLA_V)
    gn = jnp.zeros((8, HP), F32)
    gn = gn.at[0].set(jnp.concatenate([qn_nope, qn_rope, qn_rope[swap]]))
    gn = gn.at[1, :NP].set(kn_nope).at[2, :RP].set(kn_rope)
    return (q_norm.reshape(1, Q_LORA), kv_norm.reshape(1, KV_LORA),
            wq.astype(BF16), wk.astype(BF16), wv.T.astype(BF16), gn)


def _rope_tables(ctx_len, seq):
    rows = seq // GRID_W
    axis_dims = MLA_ROPE // 2
    row = jnp.repeat(jnp.arange(rows), GRID_W).astype(F32)
    col = jnp.tile(jnp.arange(GRID_W), rows).astype(F32)
    inv = ROPE_BASE ** (-jnp.arange(0, axis_dims, 2, dtype=F32) / axis_dims)
    ang = jnp.concatenate([row[:, None] * inv, col[:, None] * inv], axis=-1)
    cos = jnp.concatenate([jnp.ones((ctx_len, axis_dims), F32), jnp.cos(ang)], axis=0)
    sin = jnp.concatenate([jnp.zeros((ctx_len, axis_dims), F32), jnp.sin(ang)], axis=0)
    S = ctx_len + seq
    pad = jnp.zeros((S, HEAD_PAD - MLA_NOPE - MLA_ROPE), F32)
    cos_t = jnp.concatenate([jnp.ones((S, MLA_NOPE), F32), cos, cos, pad], axis=-1)
    sin_t = jnp.concatenate([jnp.zeros((S, MLA_NOPE), F32), -sin, sin, pad], axis=-1)
    qs = MLA_SCALE * math.log2(math.e)
    return cos_t * qs, sin_t * qs, cos_t, sin_t


def _attn_kernel(q_ref, k_ref, vt_ref, o_ref, *, nct, ctx_len):
    S = k_ref.shape[1]
    i = pl.program_id(2)

    def attend(nk):
        hs = range(q_ref.shape[2] // HEAD_PAD)
        s = [_dot_nt(k_ref[0, :nk, h * HEAD_PAD:(h + 1) * HEAD_PAD],
                     q_ref[0, :, h * HEAD_PAD:(h + 1) * HEAD_PAD]) for h in hs]
        m = [jnp.max(t, axis=0, keepdims=True) for t in s]
        e = [jnp.exp2(s[h] - m[h]) for h in hs]
        l = [jnp.sum(t, axis=0, keepdims=True) for t in e]
        o = [_dot(vt_ref[0, h * MLA_V:(h + 1) * MLA_V, :nk], e[h].astype(BF16)) / l[h] for h in hs]
        for j in range(len(hs) // 2):
            pair = jnp.concatenate([o[2 * j], o[2 * j + 1]], axis=0)
            o_ref[0, :, j * 2 * MLA_V:(j + 1) * 2 * MLA_V] = pair.T.astype(BF16)

    @pl.when(i < nct)
    def _():
        attend(ctx_len)

    @pl.when(i >= nct)
    def _():
        attend(S)


def _attention(q, k, v, ctx_len):
    B, S, _ = q.shape
    tq = _token_tile(ctx_len, S - ctx_len)
    HS = 8
    return pl.pallas_call(
        functools.partial(_attn_kernel, nct=ctx_len // tq, ctx_len=ctx_len),
        grid=(B, MLA_HEADS // HS, S // tq),
        in_specs=[pl.BlockSpec((1, tq, HS * HEAD_PAD), lambda b, h, i: (b, i, h)),
                  pl.BlockSpec((1, S, HS * HEAD_PAD), lambda b, h, i: (b, 0, h)),
                  pl.BlockSpec((1, HS * MLA_V, S), lambda b, h, i: (b, h, 0))],
        out_specs=pl.BlockSpec((1, tq, HS * MLA_V), lambda b, h, i: (b, i, h)),
        out_shape=jax.ShapeDtypeStruct((B, S, MLA_HEADS * MLA_V), BF16),
        compiler_params=_params("parallel", "parallel", "parallel"),
        name="mla_attention",
    )(q, k, v)


def _merge_kernel(z_ref, g1_ref, g2_ref, mod_ref, ys_ref, yr_ref, ym_ref, wg_ref, wb_ref, wo_ref,
                  glw_ref, glb_ref, rw_ref, zo_ref, h2_ref, lg_ref):
    D, W = D_MODEL, BRANCH_WIDTH
    NB, tm = z_ref.shape[0], z_ref.shape[1]
    ns = range(NB)
    rows = lambda t: jnp.concatenate(t, axis=0)
    z = [z_ref[i] for i in ns]
    m = [mod_ref[i, 0] for i in ns]
    h = rows([_norm_mod(z[i], g1_ref[...], m[i][0:1], m[i][1:2]) for i in ns]).astype(BF16)
    ys = rows([ys_ref[:, i * W:(i + 1) * W] for i in ns])
    ys = 0.5 * ys * (1.0 + jnp.tanh(math.sqrt(2.0 / math.pi) * (ys + 0.044715 * ys * ys * ys)))
    ys = ys * _sigmoid(_dot(ys.astype(BF16), glw_ref[...]) + glb_ref[...])
    branches = (ys.astype(BF16), rows([yr_ref[i] for i in ns]).astype(BF16), rows([ym_ref[i] for i in ns]))
    acc = jnp.zeros((NB * tm, D), F32)
    for j, yj in enumerate(branches):
        gate = _sigmoid(_dot(h, wg_ref[:, j * D:(j + 1) * D]))
        acc = acc + gate * _dot(yj, wb_ref[j])
    mix = _dot(acc.astype(BF16), wo_ref[...])
    zn = [z[i] + m[i][2:3] * mix[i * tm:(i + 1) * tm] for i in ns]
    h2 = rows([_norm_mod(zn[i], g2_ref[...], m[i][3:4], m[i][4:5]) for i in ns])
    E = rw_ref.shape[0]
    rw_rows = jnp.concatenate(_split3(rw_ref[...]), axis=0)
    hp = _split3(h2)
    t0 = _dot_nt(rw_rows, hp[0])
    t1 = _dot_nt(rw_rows[:2 * E], hp[1])
    t2 = _dot_nt(rw_rows[:E], hp[2])
    lg = t0[:E] + t0[E:2 * E] + t0[2 * E:] + t1[:E] + t1[E:] + t2
    for i in ns:
        zo_ref[i] = zn[i]
        h2_ref[i] = hp[0][i * tm:(i + 1) * tm]
        lg_ref[i] = lg[:, i * tm:(i + 1) * tm]


def _merge(z, g1, g2, modsel, ys, yr, ym, wg, wb, wo, glw, glb, rwt, ctx_len):
    B, S, D = z.shape
    tm = _token_tile(ctx_len, S - ctx_len)
    nct = ctx_len // tm
    E = rwt.shape[0]
    NB = 4
    tok = lambda n: pl.BlockSpec((NB, tm, n), lambda b, i: (b, i, 0))
    full = lambda a: pl.BlockSpec(a.shape, lambda b, i: (0,) * a.ndim, pipeline_mode=pl.Buffered(1))
    W = BRANCH_WIDTH
    return pl.pallas_call(
        _merge_kernel,
        grid=(B // NB, S // tm),
        in_specs=[tok(D), full(g1), full(g2),
                  pl.BlockSpec((NB, 1, N_MOD, D), lambda b, i: (b, jnp.where(i >= nct, 1, 0), 0, 0)),
                  pl.BlockSpec((tm, NB * W), lambda b, i: (i, b)),
                  tok(W), tok(W), full(wg), full(wb), full(wo), full(glw), full(glb), full(rwt)],
        out_specs=[tok(D), tok(D), pl.BlockSpec((NB, E, tm), lambda b, i: (b, 0, i))],
        out_shape=[jax.ShapeDtypeStruct((B, S, D), F32), jax.ShapeDtypeStruct((B, S, D), BF16),
                   jax.ShapeDtypeStruct((B, E, S), F32)],
        compiler_params=_params("parallel", "parallel"),
        name="merge",
    )(z, g1, g2, modsel, ys, yr, ym, wg, wb, wo, glw, glb, rwt)


def _route_kernel(lg_ref, slot_ref, gate_ref, *, ctx_len, cap_c, cap_l):
    lg = lg_ref[0]
    E, S = lg.shape
    e = jnp.exp(lg - jnp.max(lg, axis=0, keepdims=True))
    aff = e / jnp.sum(e, axis=0, keepdims=True)
    bits = pltpu.bitcast(aff, jnp.int32)
    lane = lax.broadcasted_iota(jnp.int32, (1, S), 1)
    in_c = lane < ctx_len
    ms_c = jnp.where(in_c, 1.0, 0.0)
    LT = 128
    tr = lax.broadcasted_iota(jnp.int32, (LT, LT), 0)
    tc = lax.broadcasted_iota(jnp.int32, (LT, LT), 1)
    tri = jnp.where(tr <= tc, 1.0, 0.0).astype(BF16)

    def counts(x):
        n_c = jnp.sum(x * ms_c, axis=1, keepdims=True)
        return n_c, jnp.sum(x, axis=1, keepdims=True) - n_c

    def prefix(x):
        tiles = [x[:, t * LT:(t + 1) * LT] for t in range(S // LT)]
        inc = _dot(jnp.concatenate(tiles, axis=0).astype(BF16), tri)
        off = jnp.zeros((E, 1), F32)
        out = []
        for t, xt in enumerate(tiles):
            it = inc[t * E:(t + 1) * E]
            out.append(it - xt + off)
            off = off + it[:, LT - 1:LT]
        n_c, _ = counts(x)
        return jnp.concatenate(out, axis=1) - jnp.where(in_c, 0.0, n_c)

    def body(_, carry):
        lo_c, hi_c, lo_l, hi_l = carry
        mid_c = lo_c + ((hi_c - lo_c + 1) >> 1)
        mid_l = lo_l + ((hi_l - lo_l + 1) >> 1)
        n_c, n_l = counts(jnp.where(bits >= jnp.where(in_c, mid_c, mid_l), 1.0, 0.0))
        ok_c, ok_l = n_c >= cap_c, n_l >= cap_l
        return (jnp.where(ok_c, mid_c, lo_c), jnp.where(ok_c, hi_c, mid_c - 1),
                jnp.where(ok_l, mid_l, lo_l), jnp.where(ok_l, hi_l, mid_l - 1))

    zero = jnp.zeros((E, 1), jnp.int32)
    top = jnp.full((E, 1), 0x7F800000, jnp.int32)
    lo_c, _, lo_l, _ = lax.fori_loop(0, 31, body, (zero, top, zero, top))
    thr = jnp.where(in_c, lo_c, lo_l)
    gt = jnp.where(bits > thr, 1.0, 0.0)
    eq = jnp.where(bits == thr, 1.0, 0.0)
    g_c, g_l = counts(gt)
    need = jnp.where(in_c, cap_c - g_c, cap_l - g_l)
    sel = gt + eq * jnp.where(prefix(eq) < need, 1.0, 0.0)
    rank = prefix(sel) + jnp.where(in_c, 0.0, float(cap_c))
    slot_ref[0] = jnp.where(sel > 0.0, rank, -1.0)
    gate_ref[0] = aff * sel


def _route(logits_t, ctx_len, cap_c, cap_l):
    B, E, S = logits_t.shape
    assert S % 128 == 0
    spec = pl.BlockSpec((1, E, S), lambda b: (b, 0, 0))
    return pl.pallas_call(
        functools.partial(_route_kernel, ctx_len=ctx_len, cap_c=cap_c, cap_l=cap_l),
        grid=(B,),
        in_specs=[spec],
        out_specs=[spec, spec],
        out_shape=[jax.ShapeDtypeStruct((B, E, S), F32)] * 2,
        compiler_params=_params("parallel"),
        name="route",
    )(logits_t)


def _expert_kernel(h_ref, slot_ref, gate_ref, w1_ref, w3_ref, w2_ref, oc_ref, ol_ref, *, cap, cap_c, ctx_len):
    NB, S = h_ref.shape[0], h_ref.shape[1]
    onehots, gc = [], []
    for i in range(NB):
        for (c0, c1), (t0, t1) in (((0, cap_c), (0, ctx_len)), ((cap_c, cap), (ctx_len, S))):
            slot = slot_ref[i, 0, :, t0:t1]
            gate = gate_ref[i, 0, :, t0:t1]
            cidx = lax.broadcasted_iota(jnp.int32, (c1 - c0, t1 - t0), 0).astype(F32) + float(c0)
            hit = slot == cidx
            onehots.append((jnp.where(hit, 1.0, 0.0).astype(BF16), i, t0, t1))
            gc.append(jnp.sum(jnp.where(hit, gate, 0.0), axis=1, keepdims=True))
    xs = [_dot(oh, h_ref[i, t0:t1, :]) for oh, i, t0, t1 in onehots]
    xs = jnp.concatenate(xs, axis=0).astype(BF16)
    gc = jnp.concatenate(gc, axis=0)
    a1 = _dot(xs, w1_ref[0])
    a3 = _dot(xs, w3_ref[0])
    hid = (a1 * _sigmoid(a1) * a3).astype(BF16)
    y = (_dot(hid, w2_ref[0]) * gc).astype(BF16)
    for i in range(NB):
        oc_ref[i] = y[i * cap:i * cap + cap_c]
        ol_ref[i] = y[i * cap + cap_c:(i + 1) * cap]


def _experts(h2, slot, gate, w1, w3, w2, layer, cap, cap_c, ctx_len):
    B, S, D = h2.shape
    _, E, _, F = w1.shape
    NB = 2
    row = pl.BlockSpec((NB, 1, 1, S), lambda e, b: (b, e, 0, 0))
    return pl.pallas_call(
        functools.partial(_expert_kernel, cap=cap, cap_c=cap_c, ctx_len=ctx_len),
        grid=(E, B // NB),
        in_specs=[pl.BlockSpec((NB, S, D), lambda e, b: (b, 0, 0)), row, row,
                  pl.BlockSpec((None, 1, D, F), lambda e, b: (layer, e, 0, 0)),
                  pl.BlockSpec((None, 1, D, F), lambda e, b: (layer, e, 0, 0)),
                  pl.BlockSpec((None, 1, F, D), lambda e, b: (layer, e, 0, 0))],
        out_specs=[pl.BlockSpec((NB, cap_c, D), lambda e, b: (b, e, 0)),
                   pl.BlockSpec((NB, cap - cap_c, D), lambda e, b: (b, e, 0))],
        out_shape=[jax.ShapeDtypeStruct((B, E * cap_c, D), BF16),
                   jax.ShapeDtypeStruct((B, E * (cap - cap_c), D), BF16)],
        compiler_params=_params("parallel", "parallel"),
        name="experts",
    )(h2, slot.reshape(B, E, 1, S), gate.reshape(B, E, 1, S), w1, w3, w2)


def _scatter_kernel(win_ref, z_ref, mod_ref, sl_ref, yc_ref, yl_ref, o_ref, ycat_ref, *, cap_c, cap_l, nct, ntl, win,
                    tile0):
    sl = sl_ref[0]
    E, tm = sl.shape
    b, i = pl.program_id(0), pl.program_id(1) + tile0

    def onehot_t(lo, n, e):
        cidx = lax.broadcasted_iota(jnp.int32, (n, tm), 0).astype(F32) + lo
        return jnp.where(sl[e:e + 1, :] == cidx, 1.0, 0.0).astype(BF16)

    def finish(onehots, rows):
        acc = _dot_tn(jnp.concatenate(onehots, axis=0), rows)
        o_ref[0] = z_ref[0] + mod_ref[0, 0][5:6] * acc

    if tile0 < nct:
        @pl.when(i < nct)
        def _():
            finish([onehot_t(0.0, cap_c, e) for e in range(E)], yc_ref[0])

    base =(b * ntl + jnp.maximum(i - nct, 0)) * (E + 1)
    fits = win_ref[base + E] == 1

    @pl.when((i >= nct) & fits)
    def _():
        onehots = []
        for e in range(E):
            w = win_ref[base + e]
            r0 = pl.multiple_of(e * cap_l + w, 16)
            ycat_ref[e * win:(e + 1) * win, :] = yl_ref[0, pl.ds(r0, win), :]
            onehots.append(onehot_t((w + cap_c).astype(F32), win, e))
        finish(onehots, ycat_ref[...])

    @pl.when((i >= nct) & jnp.logical_not(fits))
    def _():
        finish([onehot_t(float(cap_c), cap_l, e) for e in range(E)], yl_ref[0])


def _scatter_windows(slot, ctx_len, tm, cap_l, win):
    B, E, S = slot.shape
    ntl = (S - ctx_len) // tm
    cnt = (slot[:, :, ctx_len:] >= 0).reshape(B, E, ntl, tm).sum(-1).astype(jnp.int32)
    lo = jnp.cumsum(cnt, axis=-1) - cnt
    w = jnp.minimum((lo // 16) * 16, cap_l - win)
    fits = jnp.all(lo - w + cnt <= win, axis=1)
    table = jnp.concatenate([jnp.swapaxes(w, 1, 2), fits[..., None].astype(jnp.int32)], axis=-1)
    return table.reshape(-1)


def _scatter(z, modsel, slot, yc, yl, ctx_len, cap_c, win=64, latent_only=False):
    B, S, D = z.shape
    E = slot.shape[1]
    cap_l = yl.shape[1] // E
    tm = _token_tile(ctx_len, S - ctx_len)
    nct = ctx_len // tm
    ntl = (S - ctx_len) // tm
    t0 = nct if latent_only else 0
    win = min(win, cap_l)
    tok = lambda n: pl.BlockSpec((1, tm, n), lambda b, i, w: (b, i + t0, 0))
    whole = lambda a: pl.BlockSpec((1,) + a.shape[1:], lambda b, i, w: (b, 0, 0))
    grid_spec = pltpu.PrefetchScalarGridSpec(
        num_scalar_prefetch=1,
        grid=(B, S // tm - t0),
        in_specs=[tok(D),
                  pl.BlockSpec((1, 1, N_MOD, D), lambda b, i, w: (b, jnp.where(i + t0 >= nct, 1, 0), 0, 0)),
                  pl.BlockSpec((1, E, tm), lambda b, i, w: (b, 0, i + t0)), whole(yc), whole(yl)],
        out_specs=pl.BlockSpec((1, tm, D), lambda b, i, w: (b, i, 0)),
        scratch_shapes=[pltpu.VMEM((E * win, D), BF16)])
    return pl.pallas_call(
        functools.partial(_scatter_kernel, cap_c=cap_c, cap_l=cap_l, nct=nct, ntl=ntl, win=win, tile0=t0),
        grid_spec=grid_spec,
        out_shape=jax.ShapeDtypeStruct((B, S - t0 * tm, D), F32),
        compiler_params=_params("parallel", "parallel"),
        name="moe_scatter",
    )(_scatter_windows(slot, ctx_len, tm, cap_l, win), z, modsel, slot, yc, yl)


def kernel(x, c, ctx, c_ctx, ada_w, ada_b, norm1_g, norm2_g, w_in, ssm_lambda_re, ssm_lambda_im, ssm_log_dt, ssm_b_re, ssm_b_im, ssm_c_re, ssm_c_im, ssm_d, ssm_glu_w, ssm_glu_b, rwkv_mu, rwkv_w0, rwkv_w2, rwkv_a0, rwkv_a2, rwkv_g2, rwkv_k_k, rwkv_k_a, rwkv_r_k, rwkv_ln_w, rwkv_ln_b, mla_q_norm, mla_kv_norm, mla_w_uq, mla_w_ukv, mla_qn_nope, mla_kn_nope, mla_qn_rope, mla_kn_rope, w_branch, w_out, router_w, moe_w1, moe_w3, moe_w2):
    B, T, D = x.shape
    CTX = ctx.shape[1]
    S = CTX + T
    depth = ada_w.shape[0]
    cap_c = EC_CAPACITY * CTX // N_EXPERTS
    cap_l = EC_CAPACITY * T // N_EXPERTS
    assert D == D_MODEL and T % GRID_W == 0 and cap_c % 8 == 0 and cap_l % 8 == 0 and B % 8 == 0

    rows = -(-(B + 1) // 8) * 8
    cc = jnp.concatenate([c, c_ctx[None, :], jnp.zeros((rows - B - 1, D), F32)], axis=0)
    mods = _mods(cc, ada_w, ada_b).reshape(depth, rows, N_MOD, D)
    mod_lat = mods[:, :B]
    mod_ctx = jnp.broadcast_to(mods[:, B:B + 1], mod_lat.shape)
    modsel = jnp.stack([mod_ctx, mod_lat], axis=2)

    tables = _rope_tables(CTX, T)

    z = jnp.concatenate([ctx, x], axis=1)
    w_in_b = w_in.astype(BF16)
    w1_b, w3_b, w2_b = moe_w1.astype(BF16), moe_w3.astype(BF16), moe_w2.astype(BF16)
    for l in range(depth):
        g1, g2 = norm1_g[l].reshape(1, D), norm2_g[l].reshape(1, D)
        wl = w_in_b[l]
        w_ssm = wl[:, :RWKV_OFF]
        w_rwkv = wl[:, RWKV_OFF:MLA_OFF]
        w_mla = jnp.pad(wl[:, MLA_OFF:GATE_OFF], ((0, 0), (0, MLA_IN_PAD - MLA_IN)))
        w_gate = wl[:, GATE_OFF:]
        mp = _mla_params(mla_q_norm[l], mla_kv_norm[l], mla_w_uq[l], mla_w_ukv[l], mla_qn_nope[l],
                         mla_kn_nope[l], mla_qn_rope[l], mla_kn_rope[l])
        p_ssm, p_rwkv, q, k, v = _input_proj(z, g1, modsel[l], w_ssm, w_rwkv, w_mla, tables, mp, CTX)

        prep = _ssm_prep(ssm_lambda_re[l], ssm_lambda_im[l], ssm_log_dt[l], ssm_b_re[l], ssm_b_im[l],
                         ssm_c_re[l], ssm_c_im[l])
        y_ssm = _ssm_scan(p_ssm, prep, ssm_d[l], B, CTX)

        rp = _rwkv_params(rwkv_mu[l], rwkv_w0[l], rwkv_w2[l], rwkv_a0[l], rwkv_a2[l], rwkv_g2[l],
                          rwkv_k_k[l], rwkv_k_a[l], rwkv_r_k[l], rwkv_ln_w[l], rwkv_ln_b[l])
        y_fwd = _rwkv_dir(p_rwkv, None, rp, 0, CTX)
        y_rwkv = _rwkv_dir(p_rwkv, y_fwd, rp, 1, CTX)

        y_mla = _attention(q, k, v, CTX)

        z, h2, logits_t = _merge(z, g1, g2, modsel[l], y_ssm, y_rwkv, y_mla, w_gate,
                                 w_branch[l].astype(BF16), w_out[l].astype(BF16),
                                 ssm_glu_w[l].astype(BF16), ssm_glu_b[l].reshape(1, -1),
                                 router_w[l].T, CTX)
        slot, gate = _route(logits_t, CTX, cap_c, cap_l)
        yc, yl = _experts(h2, slot, gate, w1_b, w3_b, w2_b, l, cap_c + cap_l, cap_c, CTX)
        z = _scatter(z, modsel[l], slot, yc, yl, CTX, cap_c, latent_only=(l == depth - 1))
    return z
```
